```python
import jax, jax.numpy as jnp
from jax import lax
import numpy as np

D_MODEL = 1024
BATCH = 4
SEQ = 4096
DEPTH = 4

GRID_W = 64
CTX_LEN = 256
EPS = 1e-6
ROPE_BASE = 10000.0
NEG_INF = -1e30

NA_HEADS = 8
NA_HEAD_DIM = 64
NA_KH = 8
NA_KW = 16
MLA_HEADS = 8
MLA_NOPE = 64
MLA_ROPE = 32
MLA_V = 64
MLA_Q_LORA = 256
MLA_KV_LORA = 128
MLA_Q_BLOCK = 128
SWA_HEADS = 8
SWA_KV_HEADS = 2
SWA_HEAD_DIM = 64
SWA_WINDOW = 128
SWA_BLOCK = 128
N_BRANCH = 3
BRANCH_W = NA_HEADS * NA_HEAD_DIM
N_EXPERTS = 32
TOP_K = 4
D_FF = D_MODEL
SWIGLU_ALPHA = 1.702
SWIGLU_LIMIT = 7.0
MOE_BLOCK = 256

IN_SIZES = (NA_HEADS * NA_HEAD_DIM, NA_HEADS * NA_HEAD_DIM, NA_HEADS * NA_HEAD_DIM,
            MLA_Q_LORA, MLA_KV_LORA, MLA_ROPE,
            SWA_HEADS * SWA_HEAD_DIM, SWA_KV_HEADS * SWA_HEAD_DIM, SWA_KV_HEADS * SWA_HEAD_DIM,
            N_BRANCH * D_MODEL)
IN_COLS = sum(IN_SIZES)

kernel_name = "hybrid_parallel_mixer_moe_dit"


def rmsnorm(x, g):
    xf = x.astype(jnp.float32)
    y = xf * lax.rsqrt(jnp.mean(xf * xf, axis=-1, keepdims=True) + EPS)
    return (y * g.astype(jnp.float32)).astype(x.dtype)


def modulate(h, shift, scale):
    return h * (1.0 + scale) + shift


def rope_angles(n, rot_dim):
    t = jnp.arange(n, dtype=jnp.int32)
    row = (t // GRID_W).astype(jnp.float32)
    col = (t % GRID_W).astype(jnp.float32)
    per_axis = rot_dim // 2
    inv = ROPE_BASE ** (-jnp.arange(0, per_axis, 2, dtype=jnp.float32) / per_axis)
    ang = jnp.concatenate([row[:, None] * inv[None], col[:, None] * inv[None]], axis=-1)
    return jnp.cos(ang), jnp.sin(ang)


def apply_rope(x, cos, sin):
    x1, x2 = jnp.split(x, 2, axis=-1)
    c = cos[:, None, :].astype(x.dtype)
    s = sin[:, None, :].astype(x.dtype)
    return jnp.concatenate([x1 * c - x2 * s, x1 * s + x2 * c], axis=-1)


def dense_attention(q, k, v, sink=None):
    scale = q.shape[-1] ** -0.5
    s = jnp.einsum('bqkgd,bskd->bkgqs', q, k).astype(jnp.float32) * scale
    if sink is not None:
        sk = jnp.broadcast_to(sink.astype(jnp.float32)[None, :, :, None, None], s.shape[:-1] + (1,))
        s = jnp.concatenate([s, sk], axis=-1)
    p = jax.nn.softmax(s, axis=-1)[..., :k.shape[1]].astype(v.dtype)
    return jnp.einsum('bkgqs,bskd->bqkgd', p, v)


def neighbourhood_attention(q, k, v, kc, vc, rpb):
    B, N, H, Dh = q.shape
    rows = N // GRID_W
    kh = min(NA_KH, rows)
    nk = kh * NA_KW
    scale = Dh ** -0.5
    grid = lambda t: t.reshape(B, rows, GRID_W, H, Dh)
    qg, kg, vg = grid(q), grid(k), grid(v)
    row_start = jnp.clip(jnp.arange(rows) - kh // 2, 0, rows - kh)
    cols = jnp.arange(GRID_W)
    col_start = jnp.clip(cols - NA_KW // 2, 0, GRID_W - NA_KW)
    col_win = col_start[:, None] + jnp.arange(NA_KW)[None, :]
    rpb_cols = rpb[:, :, col_win - cols[:, None] + NA_KW - 1]

    def row_block(r):
        r0 = row_start[r]

        def gather(t):
            slab = lax.dynamic_slice_in_dim(t, r0, kh, axis=1)
            nb = slab[:, :, col_win]
            return nb.transpose(0, 2, 1, 3, 4, 5).reshape(B, GRID_W, nk, H, Dh)

        kn, vn = gather(kg), gather(vg)
        qr = lax.dynamic_index_in_dim(qg, r, axis=1, keepdims=False)
        bias = rpb_cols[:, r0 + jnp.arange(kh) - r + NA_KH - 1]
        bias = bias.transpose(0, 2, 1, 3).reshape(H, GRID_W, nk)
        s_nb = jnp.einsum('bwhd,bwjhd->bhwj', qr, kn).astype(jnp.float32) * scale + bias.astype(jnp.float32)
        s_c = jnp.einsum('bwhd,bchd->bhwc', qr, kc).astype(jnp.float32) * scale
        p = jax.nn.softmax(jnp.concatenate([s_nb, s_c], axis=-1), axis=-1).astype(v.dtype)
        return (jnp.einsum('bhwj,bwjhd->bwhd', p[..., :nk], vn)
                + jnp.einsum('bhwc,bchd->bwhd', p[..., nk:], vc))

    o = lax.map(row_block, jnp.arange(rows))
    return o.transpose(1, 0, 2, 3, 4).reshape(B, N, H * Dh)


def mla_global_attention(q, k, v, kc, vc):
    B, N, H, Dq = q.shape
    scale = Dq ** -0.5
    k_all = jnp.concatenate([kc, k], axis=1)
    v_all = jnp.concatenate([vc, v], axis=1)
    nb = N // MLA_Q_BLOCK
    qb = q.reshape(B, nb, MLA_Q_BLOCK, H, Dq).transpose(1, 0, 2, 3, 4)

    def q_block(qi):
        s = jnp.einsum('bqhd,bkhd->bhqk', qi, k_all).astype(jnp.float32) * scale
        p = jax.nn.softmax(s, axis=-1).astype(v_all.dtype)
        return jnp.einsum('bhqk,bkhd->bqhd', p, v_all)

    o = lax.map(q_block, qb)
    return o.transpose(1, 0, 2, 3, 4).reshape(B, N, H * v.shape[-1])


def window_gqa_attention(q, k, v, kc, vc, sink):
    B, N, H, Dh = q.shape
    KV = k.shape[2]
    G = H // KV
    nb = N // SWA_BLOCK
    scale = Dh ** -0.5
    qb = q.reshape(B, nb, SWA_BLOCK, KV, G, Dh)
    pad = ((0, 0), (SWA_BLOCK, SWA_BLOCK), (0, 0), (0, 0))

    def band(t):
        tp = jnp.pad(t, pad).reshape(B, nb + 2, SWA_BLOCK, KV, Dh)
        return jnp.concatenate([tp[:, 0:nb], tp[:, 1:nb + 1], tp[:, 2:nb + 2]], axis=2)

    kb, vb = band(k), band(v)
    qpos = jnp.arange(N).reshape(nb, SWA_BLOCK)
    kpos = jnp.arange(nb)[:, None] * SWA_BLOCK - SWA_BLOCK + jnp.arange(3 * SWA_BLOCK)[None, :]
    mask = ((jnp.abs(qpos[:, :, None] - kpos[:, None, :]) <= SWA_WINDOW)
            & (kpos[:, None, :] >= 0) & (kpos[:, None, :] < N))
    s_band = jnp.einsum('bnqkgd,bnskd->bnkgqs', qb, kb).astype(jnp.float32) * scale
    s_band = jnp.where(mask[None, :, None, None], s_band, NEG_INF)
    s_ctx = jnp.einsum('bnqkgd,bckd->bnkgqc', qb, kc).astype(jnp.float32) * scale
    s_sink = jnp.broadcast_to(sink.reshape(KV, G).astype(jnp.float32)[None, None, :, :, None, None],
                              s_band.shape[:-1] + (1,))
    p = jax.nn.softmax(jnp.concatenate([s_band, s_ctx, s_sink], axis=-1), axis=-1).astype(v.dtype)
    nbk = 3 * SWA_BLOCK
    o = (jnp.einsum('bnkgqs,bnskd->bnqkgd', p[..., :nbk], vb)
         + jnp.einsum('bnkgqc,bckd->bnqkgd', p[..., nbk:nbk + kc.shape[1]], vc))
    return o.reshape(B, N, H * Dh)


def merge_branches(o, g, w_branch, w_out):
    proj = jnp.einsum('bnie,ied->bnid', o, w_branch)
    gate = jax.nn.sigmoid(g.reshape(g.shape[:2] + (N_BRANCH, D_MODEL)))
    return jnp.sum(gate * proj, axis=2) @ w_out


def token_mixers(h_ctx, h_lat, w_in, na_rpb, mla_q_norm_g, mla_kv_norm_g, mla_w_uq, mla_w_ukv,
                 swa_sink, w_branch, w_out, rope_b, rope_c, with_ctx_out):
    B, Lc, _ = h_ctx.shape
    L = Lc + h_lat.shape[1]
    h = jnp.concatenate([h_ctx, h_lat], axis=1)
    split_points = [int(p) for p in np.cumsum(IN_SIZES)[:-1]]
    a_q, a_k, a_v, b_ql, b_kvl, b_kr, c_q, c_k, c_v, g = jnp.split(h @ w_in, split_points, axis=-1)
    heads = lambda t, n: t.reshape(B, L, n, -1)
    a_q, a_k, a_v = heads(a_q, NA_HEADS), heads(a_k, NA_HEADS), heads(a_v, NA_HEADS)
    bq = heads(rmsnorm(b_ql, mla_q_norm_g) @ mla_w_uq, MLA_HEADS)
    bkv = heads(rmsnorm(b_kvl, mla_kv_norm_g) @ mla_w_ukv, MLA_HEADS)
    q_nope, q_pe = jnp.split(bq, [MLA_NOPE], axis=-1)
    k_nope, b_v = jnp.split(bkv, [MLA_NOPE], axis=-1)
    k_pe = b_kr[:, :, None, :]
    q_pe = jnp.concatenate([q_pe[:, :Lc], apply_rope(q_pe[:, Lc:], *rope_b)], axis=1)
    k_pe = jnp.concatenate([k_pe[:, :Lc], apply_rope(k_pe[:, Lc:], *rope_b)], axis=1)
    b_q = jnp.concatenate([q_nope, q_pe], axis=-1)
    b_k = jnp.concatenate([k_nope, jnp.broadcast_to(k_pe, k_nope.shape[:3] + (MLA_ROPE,))], axis=-1)
    c_q, c_k, c_v = heads(c_q, SWA_HEADS), heads(c_k, SWA_KV_HEADS), heads(c_v, SWA_KV_HEADS)
    c_q = jnp.concatenate([c_q[:, :Lc], apply_rope(c_q[:, Lc:], *rope_c)], axis=1)
    c_k = jnp.concatenate([c_k[:, :Lc], apply_rope(c_k[:, Lc:], *rope_c)], axis=1)

    o_a = neighbourhood_attention(a_q[:, Lc:], a_k[:, Lc:], a_v[:, Lc:], a_k[:, :Lc], a_v[:, :Lc], na_rpb)
    o_b = mla_global_attention(b_q[:, Lc:], b_k[:, Lc:], b_v[:, Lc:], b_k[:, :Lc], b_v[:, :Lc])
    o_c = window_gqa_attention(c_q[:, Lc:], c_k[:, Lc:], c_v[:, Lc:], c_k[:, :Lc], c_v[:, :Lc], swa_sink)
    y_lat = merge_branches(jnp.stack([o_a, o_b, o_c], axis=2), g[:, Lc:], w_branch, w_out)
    y_ctx = None
    if with_ctx_out:
        G = SWA_HEADS // SWA_KV_HEADS
        oa_c = dense_attention(a_q[:, :Lc, :, None], a_k[:, :Lc], a_v[:, :Lc]).reshape(B, Lc, BRANCH_W)
        ob_c = dense_attention(b_q[:, :Lc, :, None], b_k[:, :Lc], b_v[:, :Lc]).reshape(B, Lc, BRANCH_W)
        oc_c = dense_attention(c_q[:, :Lc].reshape(B, Lc, SWA_KV_HEADS, G, SWA_HEAD_DIM),
                               c_k[:, :Lc], c_v[:, :Lc], swa_sink.reshape(SWA_KV_HEADS, G)).reshape(B, Lc, BRANCH_W)
        y_ctx = merge_branches(jnp.stack([oa_c, ob_c, oc_c], axis=2), g[:, :Lc], w_branch, w_out)
    return y_lat, y_ctx


def moe_ffn(h, router_w, router_b, w_gu, b_gu, w_dn, b_dn):
    T, D = h.shape
    logits = (h @ router_w + router_b).astype(jnp.float32)
    top_v, top_e = lax.top_k(logits, TOP_K)
    top_p = jax.nn.softmax(top_v, axis=-1)
    A = T * TOP_K
    flat_e = top_e.reshape(A)
    order = jnp.argsort(flat_e)
    s_e = flat_e[order]
    s_tok = (order // TOP_K).astype(jnp.int32)
    s_p = top_p.reshape(A)[order].astype(h.dtype)
    counts = jnp.zeros((N_EXPERTS,), jnp.int32).at[flat_e].add(1)
    padded = (counts + MOE_BLOCK - 1) // MOE_BLOCK * MOE_BLOCK
    pad_end = jnp.cumsum(padded)
    pad_start = pad_end - padded
    grp_start = jnp.cumsum(counts) - counts
    dest = pad_start[s_e] + jnp.arange(A, dtype=jnp.int32) - grp_start[s_e]
    n_blocks = A // MOE_BLOCK + N_EXPERTS + 1
    row_tok = jnp.full((n_blocks * MOE_BLOCK,), T, jnp.int32).at[dest].set(s_tok)
    h_pad = jnp.concatenate([h, jnp.zeros((1, D), h.dtype)], axis=0)
    xs = h_pad[row_tok].reshape(n_blocks, MOE_BLOCK, D)
    blk_start = jnp.arange(n_blocks, dtype=jnp.int32) * MOE_BLOCK
    blk_e = jnp.minimum(jnp.searchsorted(pad_end, blk_start, side='right'), N_EXPERTS - 1)

    def expert_block(args):
        xb, e = args
        gu = xb @ w_gu[e] + b_gu[e]
        x_glu, x_lin = jnp.split(gu, 2, axis=-1)
        x_glu = jnp.minimum(x_glu, SWIGLU_LIMIT)
        x_lin = jnp.clip(x_lin, -SWIGLU_LIMIT, SWIGLU_LIMIT)
        act = x_glu * jax.nn.sigmoid(SWIGLU_ALPHA * x_glu) * (x_lin + 1.0)
        return act @ w_dn[e] + b_dn[e]

    ys = lax.map(expert_block, (xs, blk_e)).reshape(-1, D)
    return jax.ops.segment_sum(ys[dest] * s_p[:, None], s_tok, num_segments=T)


def setup_inputs(seed: int = 0) -> dict:
    key = jax.random.key(seed)
    ks = jax.random.split(key, 24)
    f32 = jnp.float32
    nrm = lambda k, shape, s: jax.random.normal(k, shape, f32) * s
    return {
        "x": nrm(ks[0], (BATCH, SEQ, D_MODEL), 1.0),
        "c": nrm(ks[1], (BATCH, D_MODEL), 1.0),
        "ctx": nrm(ks[2], (BATCH, CTX_LEN, D_MODEL), 1.0),
        "c_ctx": nrm(ks[3], (D_MODEL,), 1.0),
        "norm1_g": 1.0 + nrm(ks[4], (DEPTH, D_MODEL), 0.01),
        "norm2_g": 1.0 + nrm(ks[5], (DEPTH, D_MODEL), 0.01),
        "w_mod": nrm(ks[6], (DEPTH, D_MODEL, 6 * D_MODEL), 0.5 * D_MODEL ** -0.5),
        "b_mod": nrm(ks[7], (DEPTH, 6 * D_MODEL), 0.01),
        "w_in": nrm(ks[8], (DEPTH, D_MODEL, IN_COLS), D_MODEL ** -0.5),
        "na_rpb": nrm(ks[9], (DEPTH, NA_HEADS, 2 * NA_KH - 1, 2 * NA_KW - 1), 0.1),
        "mla_q_norm_g": 1.0 + nrm(ks[10], (DEPTH, MLA_Q_LORA), 0.01),
        "mla_kv_norm_g": 1.0 + nrm(ks[11], (DEPTH, MLA_KV_LORA), 0.01),
        "mla_w_uq": nrm(ks[12], (DEPTH, MLA_Q_LORA, MLA_HEADS * (MLA_NOPE + MLA_ROPE)), MLA_Q_LORA ** -0.5),
        "mla_w_ukv": nrm(ks[13], (DEPTH, MLA_KV_LORA, MLA_HEADS * (MLA_NOPE + MLA_V)), MLA_KV_LORA ** -0.5),
        "swa_sink": nrm(ks[14], (DEPTH, SWA_HEADS), 1.0),
        "w_branch": nrm(ks[15], (DEPTH, N_BRANCH, BRANCH_W, D_MODEL), BRANCH_W ** -0.5),
        "w_out": nrm(ks[16], (DEPTH, D_MODEL, D_MODEL), D_MODEL ** -0.5),
        "router_w": nrm(ks[17], (DEPTH, D_MODEL, N_EXPERTS), D_MODEL ** -0.5),
        "router_b": nrm(ks[18], (DEPTH, N_EXPERTS), 0.01),
        "expert_w_gate_up": nrm(ks[19], (DEPTH, N_EXPERTS, D_MODEL, 2 * D_FF), D_MODEL ** -0.5),
        "expert_b_gate_up": nrm(ks[20], (DEPTH, N_EXPERTS, 2 * D_FF), 0.01),
        "expert_w_down": nrm(ks[21], (DEPTH, N_EXPERTS, D_FF, D_MODEL), D_FF ** -0.5),
        "expert_b_down": nrm(ks[22], (DEPTH, N_EXPERTS, D_MODEL), 0.01),
        "final_norm_g": 1.0 + nrm(ks[23], (D_MODEL,), 0.01),
    }


def reference(x, c, ctx, c_ctx, norm1_g, norm2_g, w_mod, b_mod, w_in, na_rpb, mla_q_norm_g,
              mla_kv_norm_g, mla_w_uq, mla_w_ukv, swa_sink, w_branch, w_out, router_w, router_b,
              expert_w_gate_up, expert_b_gate_up, expert_w_down, expert_b_down, final_norm_g):
    B, N, D = x.shape
    Lc = ctx.shape[1]
    rope_b = rope_angles(N, MLA_ROPE)
    rope_c = rope_angles(N, SWA_HEAD_DIM)
    silu_c = jax.nn.silu(c)
    silu_cc = jax.nn.silu(c_ctx)
    x_c = ctx
    for l in range(DEPTH):
        with_ctx = l < DEPTH - 1
        mod = silu_c @ w_mod[l] + b_mod[l]
        mod_c = silu_cc @ w_mod[l] + b_mod[l]
        sh1, sc1, g1, sh2, sc2, g2 = [m[:, None, :] for m in jnp.split(mod, 6, axis=-1)]
        csh1, csc1, cg1, csh2, csc2, cg2 = jnp.split(mod_c, 6, axis=-1)
        h_lat = modulate(rmsnorm(x, norm1_g[l]), sh1, sc1)
        h_ctx = modulate(rmsnorm(x_c, norm1_g[l]), csh1, csc1)
        y_lat, y_ctx = token_mixers(h_ctx, h_lat, w_in[l], na_rpb[l], mla_q_norm_g[l], mla_kv_norm_g[l],
                                    mla_w_uq[l], mla_w_ukv[l], swa_sink[l], w_branch[l], w_out[l],
                                    rope_b, rope_c, with_ctx)
        x = x + g1 * y_lat
        h2 = modulate(rmsnorm(x, norm2_g[l]), sh2, sc2)
        if with_ctx:
            x_c = x_c + cg1 * y_ctx
            h2c = modulate(rmsnorm(x_c, norm2_g[l]), csh2, csc2)
            h2 = jnp.concatenate([h2c, h2], axis=1)
        L = h2.shape[1]
        y = moe_ffn(h2.reshape(B * L, D), router_w[l], router_b[l], expert_w_gate_up[l],
                    expert_b_gate_up[l], expert_w_down[l], expert_b_down[l]).reshape(B, L, D)
        x = x + g2 * y[:, L - N:]
        if with_ctx:
            x_c = x_c + cg2 * y[:, :Lc]
    return rmsnorm(x, final_norm_g)
```

```python
import functools

import numpy as np
import jax
import jax.numpy as jnp
from jax import lax
from jax.experimental import pallas as pl
from jax.experimental.pallas import tpu as pltpu

GRID_W = 64
EPS = 1e-6
ROPE_BASE = 10000.0
NEG_INF = -1e30
LANES = 128

NA_HEADS = 8
NA_HEAD_DIM = 64
NA_KH = 8
NA_KW = 16
NA_QROWS = 4
NA_SLAB = 12
MLA_HEADS = 8
MLA_NOPE = 64
MLA_ROPE = 32
MLA_V = 64
MLA_Q_LORA = 256
MLA_KV_LORA = 128
SWA_HEADS = 8
SWA_KV_HEADS = 2
SWA_HEAD_DIM = 64
SWA_WINDOW = 128
N_BRANCH = 3
BRANCH_W = 512
N_EXPERTS = 32
TOP_K = 4
SWIGLU_ALPHA = 1.702
SWIGLU_LIMIT = 7.0

TM = 256
MOE_TM = 512
VMEM_LIMIT = 56 * 1024 * 1024

BF16 = jnp.bfloat16
F32 = jnp.float32


def _dot(a, b):
    return jnp.dot(a, b, preferred_element_type=F32)


def _dot_nt(a, b):
    return lax.dot_general(a, b, (((1,), (1,)), ((), ())), preferred_element_type=F32)


def _params(n_axes, vmem=VMEM_LIMIT):
    return pltpu.CompilerParams(dimension_semantics=("arbitrary",) * n_axes, vmem_limit_bytes=vmem)


def _rms(x):
    return x * lax.rsqrt(jnp.mean(x * x, axis=-1, keepdims=True) + EPS)


def _sigmoid(x):
    return 1.0 / (1.0 + jnp.exp(-x))


def _mod_kernel(c_ref, w_ref, b_ref, o_ref):
    c = c_ref[...]
    s = (c * _sigmoid(c)).astype(BF16)
    o_ref[0] = _dot(s, w_ref[0].astype(BF16)) + b_ref[0]


def _modulation(cvec, w_mod, b_mod):
    depth, d, n6 = w_mod.shape
    tn = n6 // 4
    return pl.pallas_call(
        _mod_kernel,
        out_shape=jax.ShapeDtypeStruct((depth, 8, n6), F32),
        grid=(depth, n6 // tn),
        in_specs=[pl.BlockSpec((8, d), lambda l, j: (0, 0)),
                  pl.BlockSpec((1, d, tn), lambda l, j: (l, 0, j)),
                  pl.BlockSpec((1, 1, tn), lambda l, j: (l, 0, j))],
        out_specs=pl.BlockSpec((1, 8, tn), lambda l, j: (l, 0, j)),
        compiler_params=_params(2),
        name="modulation",
    )(cvec, w_mod, b_mod.reshape(depth, 1, n6))


def _rope_groups(x, tab_ref, shift):
    cos, sdn, sup = tab_ref[0], tab_ref[1], tab_ref[2]
    outs = []
    for g in range(x.shape[1] // LANES):
        xg = x[:, g * LANES:(g + 1) * LANES]
        outs.append(xg * cos + pltpu.roll(xg, LANES - shift, 1) * sdn + pltpu.roll(xg, shift, 1) * sup)
    return outs[0] if len(outs) == 1 else jnp.concatenate(outs, axis=1)


def _inproj_kernel(x_ref, mod_ref, g1_ref, w_ref, wuq_ref, wukv_ref, gq_ref, gkv_ref, rb_ref, rc_ref,
                   aq_ref, ak_ref, av_ref, bq_ref, bk_ref, bv_ref, cq_ref, ck_ref, cv_ref, gate_ref):
    x = x_ref[0]
    mod = mod_ref[0, 0]
    h = (_rms(x) * g1_ref[...]) * (1.0 + mod[1:2]) + mod[0:1]
    hb = h.astype(BF16)
    acc = _dot(hb, w_ref[:, 0:1536])
    aq_ref[0] = (acc[:, 0:512] * 0.125).astype(BF16)
    ak_ref[0] = acc[:, 512:1024].astype(BF16)
    av_ref[0] = acc[:, 1024:1536].astype(BF16)
    acc = _dot(hb, w_ref[:, 1536:2048])
    qn = (_rms(acc[:, 0:256]) * gq_ref[...]).astype(BF16)
    kvn = (_rms(acc[:, 256:384]) * gkv_ref[...]).astype(BF16)
    kr = _rope_groups(acc[:, 384:512], rb_ref, MLA_ROPE // 2)
    q = _rope_groups(_dot(qn, wuq_ref[...]), rb_ref, MLA_ROPE // 2)
    bq_ref[0] = (q * float((MLA_NOPE + MLA_ROPE) ** -0.5)).astype(BF16)
    kv = _dot(kvn, wukv_ref[...])
    bk_ref[0] = (kv[:, 0:1024] + jnp.concatenate([kr] * MLA_HEADS, axis=1)).astype(BF16)
    bv_ref[0] = kv[:, 1024:1536].astype(BF16)
    acc = _dot(hb, w_ref[:, 2048:3072])
    cq_ref[0] = (_rope_groups(acc[:, 0:512], rc_ref, SWA_HEAD_DIM // 2) * 0.125).astype(BF16)
    ck_ref[0] = _rope_groups(acc[:, 512:768], rc_ref, SWA_HEAD_DIM // 2).astype(BF16)
    cv_ref[0] = acc[:, 768:1024].astype(BF16)
    gate_ref[0] = _sigmoid(_dot(hb, w_ref[:, 3072:])).astype(BF16)


def _inproj(x_all, modsel, g1, w, wuq, wukv, gq, gkv, rope_b, rope_c):
    b, l, d = x_all.shape
    nb = l // TM
    row = lambda width: pl.BlockSpec((1, TM, width), lambda i, j: (i, j, 0))
    full = lambda a: pl.BlockSpec(a.shape, lambda i, j: (0,) * a.ndim)
    widths = (512, 512, 512, 1024, 1024, 512, 512, 256, 256, N_BRANCH * d)
    return pl.pallas_call(
        _inproj_kernel,
        out_shape=[jax.ShapeDtypeStruct((b, l, wd), BF16) for wd in widths],
        grid=(b, nb),
        in_specs=[row(d),
                  pl.BlockSpec((1, 1, 6, d), lambda i, j: (i, jnp.minimum(j, 1), 0, 0)),
                  full(g1), full(w), full(wuq), full(wukv), full(gq), full(gkv),
                  pl.BlockSpec((3, TM, LANES), lambda i, j: (0, j, 0)),
                  pl.BlockSpec((3, TM, LANES), lambda i, j: (0, j, 0))],
        out_specs=[row(wd) for wd in widths],
        compiler_params=_params(2),
        name="inproj",
    )(x_all, modsel, g1, w, wuq, wukv, gq, gkv, rope_b, rope_c)


def _lane_lo():
    return lax.broadcasted_iota(jnp.int32, (1, LANES), 1) < (LANES // 2)


def _split_heads(qp, lo):
    zero = jnp.zeros_like(qp)
    return jnp.where(lo, qp, zero), jnp.where(lo, zero, qp)


def _softmax_pv(score_parts, value_parts, extra_logit=None):
    m = score_parts[0].max(axis=-1, keepdims=True)
    for s in score_parts[1:]:
        m = jnp.maximum(m, s.max(axis=-1, keepdims=True))
    if extra_logit is not None:
        m = jnp.maximum(m, extra_logit)
    den = None
    acc = None
    for s, v in zip(score_parts, value_parts):
        e = jnp.exp(s - m)
        d = e.sum(axis=-1, keepdims=True)
        den = d if den is None else den + d
        o = _dot(e.astype(BF16), v)
        acc = o if acc is None else acc + o
    if extra_logit is not None:
        den = den + jnp.exp(extra_logit - m)
    return acc / den


def _na_kernel(q_ref, k_ref, v_ref, bias_ref, o_ref, *, ctx, rows):
    j = pl.program_id(1)
    lo = _lane_lo()

    @pl.when(j == 0)
    def _():
        for jp in range(NA_HEADS // 2):
            sl = slice(jp * LANES, (jp + 1) * LANES)
            kc, vc = k_ref[0, 0:ctx, sl], v_ref[0, 0:ctx, sl]
            outs = [_softmax_pv([_dot_nt(qm, kc)], [vc]) for qm in _split_heads(q_ref[0, :, sl], lo)]
            o_ref[0, :, sl] = jnp.where(lo, outs[0], outs[1]).astype(BF16)

    @pl.when(j > 0)
    def _():
        r = (j - 1) * NA_QROWS
        s0 = jnp.clip(r - NA_KH // 2, 0, rows - NA_SLAB)
        start = pl.multiple_of(ctx + s0 * GRID_W, GRID_W)
        for jp in range(NA_HEADS // 2):
            sl = slice(jp * LANES, (jp + 1) * LANES)
            kc, vc = k_ref[0, 0:ctx, sl], v_ref[0, 0:ctx, sl]
            kn = k_ref[0, pl.ds(start, NA_SLAB * GRID_W), sl]
            vn = v_ref[0, pl.ds(start, NA_SLAB * GRID_W), sl]
            outs = []
            for half, qm in enumerate(_split_heads(q_ref[0, :, sl], lo)):
                s_n = _dot_nt(qm, kn) + bias_ref[0, 2 * jp + half]
                outs.append(_softmax_pv([_dot_nt(qm, kc), s_n], [vc, vn]))
            o_ref[0, :, sl] = jnp.where(lo, outs[0], outs[1]).astype(BF16)


def _na_attention(aq, ak, av, bias, ctx):
    b, l, w = aq.shape
    nb = l // TM
    rows = (l - ctx) // GRID_W
    last = nb - 1

    def bias_map(i, j):
        return (jnp.where(j <= 1, 0, jnp.where(j == last, 2, 1)), 0, 0, 0)

    return pl.pallas_call(
        functools.partial(_na_kernel, ctx=ctx, rows=rows),
        out_shape=jax.ShapeDtypeStruct((b, l, w), BF16),
        grid=(b, nb),
        in_specs=[pl.BlockSpec((1, TM, w), lambda i, j: (i, j, 0)),
                  pl.BlockSpec((1, l, w), lambda i, j: (i, 0, 0)),
                  pl.BlockSpec((1, l, w), lambda i, j: (i, 0, 0)),
                  pl.BlockSpec((1, NA_HEADS, TM, NA_SLAB * GRID_W), bias_map)],
        out_specs=pl.BlockSpec((1, TM, w), lambda i, j: (i, j, 0)),
        compiler_params=_params(2),
        name="na_attention",
    )(aq, ak, av, bias)


def _na_bias_tables(na_rpb, rows):
    depth = na_rpb.shape[0]
    qc = np.arange(GRID_W)[:, None]
    kc = np.arange(GRID_W)[None, :]
    c0 = np.clip(qc - NA_KW // 2, 0, GRID_W - NA_KW)
    col_ok = (kc >= c0) & (kc < c0 + NA_KW)
    col_idx = np.clip(kc - qc + NA_KW - 1, 0, 2 * NA_KW - 2)
    bc = jnp.take(na_rpb, jnp.asarray(col_idx.reshape(-1)), axis=3).reshape(
        depth, NA_HEADS, 2 * NA_KH - 1, GRID_W, GRID_W)
    i_idx = np.zeros((3, NA_QROWS, NA_SLAB), np.int32)
    ok = np.zeros((3, NA_QROWS, NA_SLAB, GRID_W, GRID_W), bool)
    for case, (r, s) in enumerate(((0, 0), (NA_KH // 2, 0), (rows - NA_QROWS, rows - NA_SLAB))):
        for a in range(NA_QROWS):
            qr = r + a
            r0 = min(max(qr - NA_KH // 2, 0), rows - NA_KH)
            for c in range(NA_SLAB):
                kr = s + c
                inside = r0 <= kr < r0 + NA_KH
                i_idx[case, a, c] = min(max(kr - qr + NA_KH - 1, 0), 2 * NA_KH - 2)
                ok[case, a, c] = col_ok & inside
    t = jnp.take(bc, jnp.asarray(i_idx.reshape(-1)), axis=2).reshape(
        depth, NA_HEADS, 3, NA_QROWS, NA_SLAB, GRID_W, GRID_W)
    t = jnp.where(jnp.asarray(ok)[None, None], t, NEG_INF)
    t = t.transpose(0, 2, 1, 3, 5, 4, 6)
    return t.reshape(depth, 3, NA_HEADS, NA_QROWS * GRID_W, NA_SLAB * GRID_W)


def _mla_kernel(q_ref, k_ref, v_ref, o_ref, *, ctx):
    j = pl.program_id(1)
    lo = _lane_lo()

    def run(nkeys):
        for jp in range(MLA_HEADS // 2):
            vsl = slice(jp * LANES, (jp + 1) * LANES)
            vp = v_ref[0, 0:nkeys, vsl]
            outs = []
            for half in range(2):
                hsl = slice((2 * jp + half) * LANES, (2 * jp + half + 1) * LANES)
                s = _dot_nt(q_ref[0, :, hsl], k_ref[0, 0:nkeys, hsl])
                outs.append(_softmax_pv([s], [vp]))
            o_ref[0, :, vsl] = jnp.where(lo, outs[0], outs[1]).astype(BF16)

    @pl.when(j == 0)
    def _():
        run(ctx)

    @pl.when(j > 0)
    def _():
        run(k_ref.shape[1])


def _mla_attention(bq, bk, bv, ctx):
    b, l, wq = bq.shape
    wv = bv.shape[2]
    return pl.pallas_call(
        functools.partial(_mla_kernel, ctx=ctx),
        out_shape=jax.ShapeDtypeStruct((b, l, wv), BF16),
        grid=(b, l // TM),
        in_specs=[pl.BlockSpec((1, TM, wq), lambda i, j: (i, j, 0)),
                  pl.BlockSpec((1, l, wq), lambda i, j: (i, 0, 0)),
                  pl.BlockSpec((1, l, wv), lambda i, j: (i, 0, 0))],
        out_specs=pl.BlockSpec((1, TM, wv), lambda i, j: (i, j, 0)),
        compiler_params=_params(2),
        name="mla_attention",
    )(bq, bk, bv)


def _swa_kernel(sink_ref, q_ref, k_ref, v_ref, o_ref, *, ctx, n_lat):
    j = pl.program_id(1)
    lo = _lane_lo()
    group = SWA_HEADS // SWA_KV_HEADS
    band = TM + 2 * SWA_WINDOW

    def stacked_q(kv):
        parts = []
        for jp in range(kv * group // 2, (kv + 1) * group // 2):
            parts.extend(_split_heads(q_ref[0, :, jp * LANES:(jp + 1) * LANES], lo))
        return jnp.concatenate(parts, axis=0)

    def finish(kv, score_parts, value_parts):
        outs = []
        for g in range(group):
            rs = slice(g * TM, (g + 1) * TM)
            sink = jnp.full((1, 1), sink_ref[kv * group + g], F32)
            outs.append(_softmax_pv([s[rs] for s in score_parts], value_parts, extra_logit=sink))
        for p in range(group // 2):
            jp = kv * group // 2 + p
            o_ref[0, :, jp * LANES:(jp + 1) * LANES] = jnp.where(lo, outs[2 * p], outs[2 * p + 1]).astype(BF16)

    @pl.when(j == 0)
    def _():
        for kv in range(SWA_KV_HEADS):
            sl = slice(kv * LANES, (kv + 1) * LANES)
            kc, vc = k_ref[0, 0:ctx, sl], v_ref[0, 0:ctx, sl]
            finish(kv, [_dot_nt(stacked_q(kv), kc)], [vc])

    @pl.when(j > 0)
    def _():
        q0 = (j - 1) * TM
        s0 = jnp.clip(q0 - SWA_WINDOW, 0, n_lat - band)
        start = pl.multiple_of(ctx + s0, SWA_WINDOW)
        qpos = q0 + (lax.broadcasted_iota(jnp.int32, (group * TM, band), 0) & (TM - 1))
        kpos = s0 + lax.broadcasted_iota(jnp.int32, (group * TM, band), 1)
        keep = jnp.abs(qpos - kpos) <= SWA_WINDOW
        for kv in range(SWA_KV_HEADS):
            sl = slice(kv * LANES, (kv + 1) * LANES)
            kc, vc = k_ref[0, 0:ctx, sl], v_ref[0, 0:ctx, sl]
            kb, vb = k_ref[0, pl.ds(start, band), sl], v_ref[0, pl.ds(start, band), sl]
            qs = stacked_q(kv)
            s_band = jnp.where(keep, _dot_nt(qs, kb), NEG_INF)
            finish(kv, [_dot_nt(qs, kc), s_band], [vc, vb])


def _swa_attention(sink, cq, ck, cv, ctx):
    b, l, w = cq.shape
    wk = ck.shape[2]
    grid_spec = pltpu.PrefetchScalarGridSpec(
        num_scalar_prefetch=1,
        grid=(b, l // TM),
        in_specs=[pl.BlockSpec((1, TM, w), lambda i, j, s: (i, j, 0)),
                  pl.BlockSpec((1, l, wk), lambda i, j, s: (i, 0, 0)),
                  pl.BlockSpec((1, l, wk), lambda i, j, s: (i, 0, 0))],
        out_specs=pl.BlockSpec((1, TM, w), lambda i, j, s: (i, j, 0)),
    )
    return pl.pallas_call(
        functools.partial(_swa_kernel, ctx=ctx, n_lat=l - ctx),
        out_shape=jax.ShapeDtypeStruct((b, l, w), BF16),
        grid_spec=grid_spec,
        compiler_params=_params(2),
        name="swa_attention",
    )(sink, cq, ck, cv)


def _merge_kernel(oa_ref, ob_ref, oc_ref, gate_ref, x_ref, mod_ref, wbr_ref, wout_ref, g2_ref,
                  rw_ref, rb_ref, xo_ref, h2_ref, lg_ref):
    d = x_ref.shape[2]
    mod = mod_ref[0, 0]
    mix = None
    for i, o_ref in enumerate((oa_ref, ob_ref, oc_ref)):
        t = gate_ref[0, :, i * d:(i + 1) * d].astype(F32) * _dot(o_ref[0], wbr_ref[i])
        mix = t if mix is None else mix + t
    y = _dot(mix.astype(BF16), wout_ref[...])
    x = x_ref[0] + mod[2:3] * y
    xo_ref[0] = x
    h2 = (_rms(x) * g2_ref[...]) * (1.0 + mod[4:5]) + mod[3:4]
    h2b = h2.astype(BF16)
    h2_ref[0] = h2b
    lg_ref[0] = _dot(h2b, rw_ref[...]) + rb_ref[...]


def _merge(oa, ob, oc, gate, x_all, modsel, wbr, wout, g2, rw, rb):
    b, l, d = x_all.shape
    row = lambda width: pl.BlockSpec((1, TM, width), lambda i, j: (i, j, 0))
    full = lambda a: pl.BlockSpec(a.shape, lambda i, j: (0,) * a.ndim)
    return pl.pallas_call(
        _merge_kernel,
        out_shape=[jax.ShapeDtypeStruct((b, l, d), F32),
                   jax.ShapeDtypeStruct((b, l, d), BF16),
                   jax.ShapeDtypeStruct((b, l, LANES), F32)],
        grid=(b, l // TM),
        in_specs=[row(BRANCH_W), row(BRANCH_W), row(BRANCH_W), row(N_BRANCH * d), row(d),
                  pl.BlockSpec((1, 1, 6, d), lambda i, j: (i, jnp.minimum(j, 1), 0, 0)),
                  full(wbr), full(wout), full(g2), full(rw), full(rb)],
        out_specs=[row(d), row(d), row(LANES)],
        compiler_params=_params(2),
        name="merge",
    )(oa, ob, oc, gate, x_all, modsel, wbr, wout, g2, rw, rb)


def _moe_kernel(be_ref, nu_ref, x_ref, p_ref, wgu_ref, bgu_ref, wdn_ref, bdn_ref, y_ref, wgu_s, wdn_s):
    i = pl.program_id(0)
    ff = wdn_ref.shape[1]

    @pl.when(i < nu_ref[0])
    def _():
        prev = be_ref[jnp.maximum(i - 1, 0)]

        @pl.when(jnp.logical_or(i == 0, be_ref[i] != prev))
        def _():
            wgu_s[...] = wgu_ref[0].astype(BF16)
            wdn_s[...] = wdn_ref[0].astype(BF16)

        gu = _dot(x_ref[...], wgu_s[...]) + bgu_ref[0]
        glu = jnp.minimum(gu[:, :ff], SWIGLU_LIMIT)
        lin = jnp.clip(gu[:, ff:], -SWIGLU_LIMIT, SWIGLU_LIMIT)
        act = glu * _sigmoid(SWIGLU_ALPHA * glu) * (lin + 1.0)
        y = _dot(act.astype(BF16), wdn_s[...]) + bdn_ref[0]
        y_ref[...] = y * p_ref[...]

    @pl.when(i >= nu_ref[0])
    def _():
        y_ref[...] = jnp.zeros_like(y_ref)


def _moe_experts(blk_e, n_used, xs, row_p, w_gu, b_gu, w_dn, b_dn):
    n_rows, d = xs.shape
    n_exp, _, ff2 = w_gu.shape
    ff = ff2 // 2
    n_blocks = n_rows // MOE_TM

    def row_map(i, be, nu):
        return (jnp.minimum(i, nu[0] - 1), 0)

    def w_map(i, be, nu):
        return (be[jnp.minimum(i, nu[0] - 1)], 0, 0)

    grid_spec = pltpu.PrefetchScalarGridSpec(
        num_scalar_prefetch=2,
        grid=(n_blocks,),
        in_specs=[pl.BlockSpec((MOE_TM, d), row_map),
                  pl.BlockSpec((MOE_TM, 1), row_map),
                  pl.BlockSpec((1, d, ff2), w_map),
                  pl.BlockSpec((1, 1, ff2), w_map),
                  pl.BlockSpec((1, ff, d), w_map),
                  pl.BlockSpec((1, 1, d), w_map)],
        out_specs=pl.BlockSpec((MOE_TM, d), lambda i, be, nu: (i, 0)),
        scratch_shapes=[pltpu.VMEM((d, ff2), BF16), pltpu.VMEM((ff, d), BF16)],
    )
    return pl.pallas_call(
        _moe_kernel,
        out_shape=jax.ShapeDtypeStruct((n_rows, d), F32),
        grid_spec=grid_spec,
        compiler_params=_params(1),
        name="moe_experts",
    )(blk_e, n_used, xs, row_p, w_gu, b_gu.reshape(n_exp, 1, ff2), w_dn, b_dn.reshape(n_exp, 1, d))


def _route(logits):
    t, n_exp = logits.shape
    top_v, top_e = lax.top_k(logits, TOP_K)
    top_p = jax.nn.softmax(top_v, axis=-1)
    a = t * TOP_K
    flat_e = top_e.reshape(a)
    onehot = (flat_e[:, None] == jnp.arange(n_exp, dtype=jnp.int32)[None, :]).astype(jnp.int32)
    csum = jnp.cumsum(onehot, axis=0)
    rank = jnp.take_along_axis(csum, flat_e[:, None], axis=1)[:, 0] - 1
    counts = csum[-1]
    padded = (counts + MOE_TM - 1) // MOE_TM * MOE_TM
    pad_end = jnp.cumsum(padded)
    pad_start = pad_end - padded
    dest = pad_start[flat_e] + rank
    n_blocks = -(-a // MOE_TM) + n_exp
    n_rows = n_blocks * MOE_TM
    tok = jnp.arange(a, dtype=jnp.int32) // TOP_K
    row_tok = jnp.zeros((n_rows,), jnp.int32).at[dest].set(tok)
    row_p = jnp.zeros((n_rows,), F32).at[dest].set(top_p.reshape(a))
    blk_start = jnp.arange(n_blocks, dtype=jnp.int32) * MOE_TM
    blk_e = jnp.minimum(jnp.searchsorted(pad_end, blk_start, side='right'), n_exp - 1).astype(jnp.int32)
    n_used = (pad_end[-1] // MOE_TM).astype(jnp.int32).reshape(1)
    return row_tok, row_p.reshape(n_rows, 1), dest.reshape(t, TOP_K), blk_e, n_used


def _resid_kernel(x_ref, y_ref, mod_ref, o_ref):
    o_ref[0] = x_ref[0] + mod_ref[0, 0][5:6] * y_ref[0]


def _final_kernel(x_ref, y_ref, mod_ref, g_ref, o_ref):
    x = x_ref[0] + mod_ref[0, 0][5:6] * y_ref[0]
    o_ref[0] = _rms(x) * g_ref[...]


def _residual(x_all, y, modsel, final_g=None):
    b, l, d = x_all.shape
    ly = y.shape[1]
    off = (l - ly) // TM
    xspec = pl.BlockSpec((1, TM, d), lambda i, j: (i, j + off, 0))
    yspec = pl.BlockSpec((1, TM, d), lambda i, j: (i, j, 0))
    mspec = pl.BlockSpec((1, 1, 6, d), lambda i, j: (i, jnp.minimum(j + off, 1), 0, 0))
    if final_g is None:
        return pl.pallas_call(
            _resid_kernel, out_shape=jax.ShapeDtypeStruct((b, ly, d), F32), grid=(b, ly // TM),
            in_specs=[xspec, yspec, mspec], out_specs=yspec, compiler_params=_params(2), name="residual",
        )(x_all, y, modsel)
    return pl.pallas_call(
        _final_kernel, out_shape=jax.ShapeDtypeStruct((b, ly, d), F32), grid=(b, ly // TM),
        in_specs=[xspec, yspec, mspec, pl.BlockSpec((1, d), lambda i, j: (0, 0))],
        out_specs=yspec, compiler_params=_params(2), name="residual_final_norm",
    )(x_all, y, modsel, final_g)


def _rope_tables(n, ctx, rot_dim, group_pattern):
    t = jnp.arange(n, dtype=jnp.int32)
    row = (t // GRID_W).astype(F32)
    col = (t % GRID_W).astype(F32)
    per_axis = rot_dim // 2
    inv = ROPE_BASE ** (-jnp.arange(0, per_axis, 2, dtype=F32) / per_axis)
    ang = jnp.concatenate([row[:, None] * inv[None], col[:, None] * inv[None]], axis=-1)
    cos, sin = jnp.cos(ang), jnp.sin(ang)
    half = rot_dim // 2
    c = jnp.ones((n, LANES), F32)
    sdn = jnp.zeros((n, LANES), F32)
    sup = jnp.zeros((n, LANES), F32)
    for off in group_pattern:
        c = c.at[:, off:off + half].set(cos).at[:, off + half:off + rot_dim].set(cos)
        sdn = sdn.at[:, off:off + half].set(-sin)
        sup = sup.at[:, off + half:off + rot_dim].set(sin)
    tab = jnp.stack([c, sdn, sup])
    ident = jnp.stack([jnp.ones((ctx, LANES), F32), jnp.zeros((ctx, LANES), F32), jnp.zeros((ctx, LANES), F32)])
    return jnp.concatenate([ident, tab], axis=1)


def _prep_weights(w_in, mla_w_uq, mla_w_ukv):
    depth, d, _ = w_in.shape
    sizes = (512, 512, 512, MLA_Q_LORA, MLA_KV_LORA, MLA_ROPE, 512, 128, 128, N_BRANCH * d)
    offs = np.concatenate([[0], np.cumsum(sizes)])
    seg = [w_in[:, :, int(offs[i]):int(offs[i + 1])] for i in range(len(sizes))]
    aq, ak, av, ql, kvl, kr, cq, ck, cv, g = seg
    kr128 = jnp.pad(kr, ((0, 0), (0, 0), (MLA_NOPE, LANES - MLA_NOPE - MLA_ROPE)))
    dup = lambda t: jnp.concatenate([t[..., :64], t[..., :64], t[..., 64:], t[..., 64:]], axis=-1)
    w = jnp.concatenate([aq, ak, av, ql, kvl, kr128, cq, dup(ck), dup(cv), g], axis=-1).astype(BF16)
    uq = mla_w_uq.reshape(depth, MLA_Q_LORA, MLA_HEADS, MLA_NOPE + MLA_ROPE)
    uq = jnp.pad(uq, ((0, 0), (0, 0), (0, 0), (0, LANES - MLA_NOPE - MLA_ROPE)))
    uq = uq.reshape(depth, MLA_Q_LORA, MLA_HEADS * LANES).astype(BF16)
    ukv = mla_w_ukv.reshape(depth, MLA_KV_LORA, MLA_HEADS, MLA_NOPE + MLA_V)
    uk = jnp.pad(ukv[..., :MLA_NOPE], ((0, 0), (0, 0), (0, 0), (0, LANES - MLA_NOPE)))
    uk = uk.reshape(depth, MLA_KV_LORA, MLA_HEADS * LANES)
    uv = ukv[..., MLA_NOPE:].reshape(depth, MLA_KV_LORA, MLA_HEADS * MLA_V)
    ukv = jnp.concatenate([uk, uv], axis=-1).astype(BF16)
    return w, uq, ukv


def kernel(x, c, ctx, c_ctx, norm1_g, norm2_g, w_mod, b_mod, w_in, na_rpb, mla_q_norm_g, mla_kv_norm_g, mla_w_uq, mla_w_ukv, swa_sink, w_branch, w_out, router_w, router_b, expert_w_gate_up, expert_b_gate_up, expert_w_down, expert_b_down, final_norm_g):
    b, n, d = x.shape
    lc = ctx.shape[1]
    l = lc + n
    depth = w_in.shape[0]
    n_exp = router_w.shape[2]
    assert lc == TM and n % TM == 0 and TM == NA_QROWS * GRID_W

    cvec = jnp.zeros((8, d), F32).at[:b].set(c).at[b].set(c_ctx)
    mod = _modulation(cvec, w_mod, b_mod)
    mod_lat = mod[:, :b].reshape(depth, b, 1, 6, d)
    mod_ctx = jnp.broadcast_to(mod[:, b].reshape(depth, 1, 1, 6, d), (depth, b, 1, 6, d))
    modsel = jnp.concatenate([mod_ctx, mod_lat], axis=2)

    w_all, uq_all, ukv_all = _prep_weights(w_in, mla_w_uq, mla_w_ukv)
    wbr_all = w_branch.astype(BF16)
    wout_all = w_out.astype(BF16)
    rw_all = jnp.pad(router_w, ((0, 0), (0, 0), (0, LANES - n_exp))).astype(BF16)
    rb_all = jnp.pad(router_b, ((0, 0), (0, LANES - n_exp))).reshape(depth, 1, LANES)
    rope_b = _rope_tables(n, lc, MLA_ROPE, (MLA_NOPE,))
    rope_c = _rope_tables(n, lc, SWA_HEAD_DIM, (0, SWA_HEAD_DIM))
    bias_all = _na_bias_tables(na_rpb, n // GRID_W)

    x_all = jnp.concatenate([ctx, x], axis=1)
    out = None
    for li in range(depth):
        last = li == depth - 1
        ms = modsel[li]
        aq, ak, av, bq, bk, bv, cq, ck, cv, gate = _inproj(
            x_all, ms, norm1_g[li].reshape(1, d), w_all[li], uq_all[li], ukv_all[li],
            mla_q_norm_g[li].reshape(1, -1), mla_kv_norm_g[li].reshape(1, -1), rope_b, rope_c)
        oa = _na_attention(aq, ak, av, bias_all[li], lc)
        ob = _mla_attention(bq, bk, bv, lc)
        oc = _swa_attention(swa_sink[li], cq, ck, cv, lc)
        x_all, h2, logits = _merge(oa, ob, oc, gate, x_all, ms, wbr_all[li], wout_all[li],
                                   norm2_g[li].reshape(1, d), rw_all[li], rb_all[li])
        skip = lc if last else 0
        lg = logits[:, skip:, :n_exp].reshape(b * (l - skip), n_exp)
        row_tok, row_p, dest, blk_e, n_used = _route(lg)
        tok_full = (row_tok // (l - skip)) * l + skip + row_tok % (l - skip)
        xs = jnp.take(h2.reshape(b * l, d), tok_full, axis=0)
        ys = _moe_experts(blk_e, n_used, xs, row_p, expert_w_gate_up[li], expert_b_gate_up[li],
                          expert_w_down[li], expert_b_down[li])
        y = jnp.take(ys, dest.reshape(-1), axis=0).reshape(b, l - skip, TOP_K, d).sum(axis=2)
        if last:
            out = _residual(x_all, y, ms, final_norm_g.reshape(1, d))
        else:
            x_all = _residual(x_all, y, ms)
    return out
```

```python
import functools

import numpy as np
import jax
import jax.numpy as jnp
from jax import lax
from jax.experimental import pallas as pl
from jax.experimental.pallas import tpu as pltpu
from jax.experimental.pallas import tpu_sc as plsc

GRID_W = 64
EPS = 1e-6
ROPE_BASE = 10000.0
NEG_INF = -1e30
LANES = 128

NA_HEADS = 8
NA_HEAD_DIM = 64
NA_KH = 8
NA_KW = 16
NA_QROWS = 4
NA_SLAB = 12
MLA_HEADS = 8
MLA_NOPE = 64
MLA_ROPE = 32
MLA_V = 64
MLA_Q_LORA = 256
MLA_KV_LORA = 128
SWA_HEADS = 8
SWA_KV_HEADS = 2
SWA_HEAD_DIM = 64
SWA_WINDOW = 128
N_BRANCH = 3
BRANCH_W = 512
N_EXPERTS = 32
TOP_K = 4
SWIGLU_ALPHA = 1.702
SWIGLU_LIMIT = 7.0

TM = 256
MOE_TM = 512
SC_CORES = 2
SC_SUBCORES = 16
SC_SCATTER_ROWS = 128
SC_GATHER_ROWS = 64
VMEM_LIMIT = 56 * 1024 * 1024

BF16 = jnp.bfloat16
F32 = jnp.float32


def _dot(a, b):
    return jnp.dot(a, b, preferred_element_type=F32)


def _dot_nt(a, b):
    return lax.dot_general(a, b, (((1,), (1,)), ((), ())), preferred_element_type=F32)


def _params(n_axes, vmem=VMEM_LIMIT):
    return pltpu.CompilerParams(dimension_semantics=("arbitrary",) * n_axes, vmem_limit_bytes=vmem)


def _rms(x):
    return x * lax.rsqrt(jnp.mean(x * x, axis=-1, keepdims=True) + EPS)


def _sigmoid(x):
    return 1.0 / (1.0 + jnp.exp(-x))


def _pack_rows(xb):
    half = xb.shape[1] // 2
    lo = pltpu.bitcast(xb[:, :half].astype(F32), jnp.int32)
    hi = pltpu.bitcast(xb[:, half:].astype(F32), jnp.int32)
    return (hi & jnp.int32(-65536)) | lax.shift_right_logical(lo, 16)


def _unpack_rows(w):
    lo = pltpu.bitcast(lax.shift_left(w, 16), F32).astype(BF16)
    hi = pltpu.bitcast(w & jnp.int32(-65536), F32).astype(BF16)
    return jnp.concatenate([lo, hi], axis=1)


def _mod_kernel(c_ref, w_ref, b_ref, o_ref):
    c = c_ref[...]
    s = (c * _sigmoid(c)).astype(BF16)
    o_ref[0] = _dot(s, w_ref[0].astype(BF16)) + b_ref[0]


def _modulation(cvec, w_mod, b_mod):
    depth, d, n6 = w_mod.shape
    tn = n6 // 4
    return pl.pallas_call(
        _mod_kernel,
        out_shape=jax.ShapeDtypeStruct((depth, 8, n6), F32),
        grid=(depth, n6 // tn),
        in_specs=[pl.BlockSpec((8, d), lambda l, j: (0, 0)),
                  pl.BlockSpec((1, d, tn), lambda l, j: (l, 0, j)),
                  pl.BlockSpec((1, 1, tn), lambda l, j: (l, 0, j))],
        out_specs=pl.BlockSpec((1, 8, tn), lambda l, j: (l, 0, j)),
        compiler_params=_params(2),
        name="modulation",
    )(cvec, w_mod, b_mod.reshape(depth, 1, n6))


def _rope_groups(x, tab_ref, shift):
    cos, sdn, sup = tab_ref[0], tab_ref[1], tab_ref[2]
    outs = []
    for g in range(x.shape[1] // LANES):
        xg = x[:, g * LANES:(g + 1) * LANES]
        outs.append(xg * cos + pltpu.roll(xg, LANES - shift, 1) * sdn + pltpu.roll(xg, shift, 1) * sup)
    return outs[0] if len(outs) == 1 else jnp.concatenate(outs, axis=1)


def _inproj_kernel(x_ref, mod_ref, g1_ref, w_ref, wuq_ref, wukv_ref, gq_ref, gkv_ref, rb_ref, rc_ref,
                   aq_ref, ak_ref, av_ref, bq_ref, bk_ref, bv_ref, cq_ref, ck_ref, cv_ref, gate_ref):
    x = x_ref[0]
    mod = mod_ref[0, 0]
    h = (_rms(x) * g1_ref[...]) * (1.0 + mod[1:2]) + mod[0:1]
    hb = h.astype(BF16)
    acc = _dot(hb, w_ref[:, 0:1536])
    aq_ref[0] = (acc[:, 0:512] * 0.125).astype(BF16)
    ak_ref[0] = acc[:, 512:1024].astype(BF16)
    av_ref[0] = acc[:, 1024:1536].astype(BF16)
    acc = _dot(hb, w_ref[:, 1536:2048])
    qn = (_rms(acc[:, 0:256]) * gq_ref[...]).astype(BF16)
    kvn = (_rms(acc[:, 256:384]) * gkv_ref[...]).astype(BF16)
    kr = _rope_groups(acc[:, 384:512], rb_ref, MLA_ROPE // 2)
    q = _rope_groups(_dot(qn, wuq_ref[...]), rb_ref, MLA_ROPE // 2)
    bq_ref[0] = (q * float((MLA_NOPE + MLA_ROPE) ** -0.5)).astype(BF16)
    kv = _dot(kvn, wukv_ref[...])
    bk_ref[0] = (kv[:, 0:1024] + jnp.concatenate([kr] * MLA_HEADS, axis=1)).astype(BF16)
    bv_ref[0] = kv[:, 1024:1536].astype(BF16)
    acc = _dot(hb, w_ref[:, 2048:3072])
    cq_ref[0] = (_rope_groups(acc[:, 0:512], rc_ref, SWA_HEAD_DIM // 2) * 0.125).astype(BF16)
    ck_ref[0] = _rope_groups(acc[:, 512:768], rc_ref, SWA_HEAD_DIM // 2).astype(BF16)
    cv_ref[0] = acc[:, 768:1024].astype(BF16)
    gate_ref[0] = _sigmoid(_dot(hb, w_ref[:, 3072:])).astype(BF16)


def _inproj(x_all, modsel, g1, w, wuq, wukv, gq, gkv, rope_b, rope_c):
    b, l, d = x_all.shape
    nb = l // TM
    row = lambda width: pl.BlockSpec((1, TM, width), lambda i, j: (i, j, 0))
    full = lambda a: pl.BlockSpec(a.shape, lambda i, j: (0,) * a.ndim)
    widths = (512, 512, 512, 1024, 1024, 512, 512, 256, 256, N_BRANCH * d)
    return pl.pallas_call(
        _inproj_kernel,
        out_shape=[jax.ShapeDtypeStruct((b, l, wd), BF16) for wd in widths],
        grid=(b, nb),
        in_specs=[row(d),
                  pl.BlockSpec((1, 1, 6, d), lambda i, j: (i, jnp.minimum(j, 1), 0, 0)),
                  full(g1), full(w), full(wuq), full(wukv), full(gq), full(gkv),
                  pl.BlockSpec((3, TM, LANES), lambda i, j: (0, j, 0)),
                  pl.BlockSpec((3, TM, LANES), lambda i, j: (0, j, 0))],
        out_specs=[row(wd) for wd in widths],
        compiler_params=_params(2),
        name="inproj",
    )(x_all, modsel, g1, w, wuq, wukv, gq, gkv, rope_b, rope_c)


def _lane_lo():
    return lax.broadcasted_iota(jnp.int32, (1, LANES), 1) < (LANES // 2)


def _split_heads(qp, lo):
    zero = jnp.zeros_like(qp)
    return jnp.where(lo, qp, zero), jnp.where(lo, zero, qp)


def _softmax_pv(score_parts, value_parts, extra_logit=None):
    m = score_parts[0].max(axis=-1, keepdims=True)
    for s in score_parts[1:]:
        m = jnp.maximum(m, s.max(axis=-1, keepdims=True))
    if extra_logit is not None:
        m = jnp.maximum(m, extra_logit)
    den = None
    acc = None
    for s, v in zip(score_parts, value_parts):
        e = jnp.exp(s - m)
        d = e.sum(axis=-1, keepdims=True)
        den = d if den is None else den + d
        o = _dot(e.astype(BF16), v)
        acc = o if acc is None else acc + o
    if extra_logit is not None:
        den = den + jnp.exp(extra_logit - m)
    return acc / den


def _na_kernel(q_ref, k_ref, v_ref, bias_ref, o_ref, *, ctx, rows):
    j = pl.program_id(1)
    lo = _lane_lo()

    @pl.when(j == 0)
    def _():
        for jp in range(NA_HEADS // 2):
            sl = slice(jp * LANES, (jp + 1) * LANES)
            kc, vc = k_ref[0, 0:ctx, sl], v_ref[0, 0:ctx, sl]
            outs = [_softmax_pv([_dot_nt(qm, kc)], [vc]) for qm in _split_heads(q_ref[0, :, sl], lo)]
            o_ref[0, :, sl] = jnp.where(lo, outs[0], outs[1]).astype(BF16)

    @pl.when(j > 0)
    def _():
        r = (j - 1) * NA_QROWS
        s0 = jnp.clip(r - NA_KH // 2, 0, rows - NA_SLAB)
        start = pl.multiple_of(ctx + s0 * GRID_W, GRID_W)
        for jp in range(NA_HEADS // 2):
            sl = slice(jp * LANES, (jp + 1) * LANES)
            kc, vc = k_ref[0, 0:ctx, sl], v_ref[0, 0:ctx, sl]
            kn = k_ref[0, pl.ds(start, NA_SLAB * GRID_W), sl]
            vn = v_ref[0, pl.ds(start, NA_SLAB * GRID_W), sl]
            outs = []
            for half, qm in enumerate(_split_heads(q_ref[0, :, sl], lo)):
                s_n = _dot_nt(qm, kn) + bias_ref[0, 2 * jp + half]
                outs.append(_softmax_pv([_dot_nt(qm, kc), s_n], [vc, vn]))
            o_ref[0, :, sl] = jnp.where(lo, outs[0], outs[1]).astype(BF16)


def _na_attention(aq, ak, av, bias, ctx):
    b, l, w = aq.shape
    nb = l // TM
    rows = (l - ctx) // GRID_W
    last = nb - 1

    def bias_map(i, j):
        return (jnp.where(j <= 1, 0, jnp.where(j == last, 2, 1)), 0, 0, 0)

    return pl.pallas_call(
        functools.partial(_na_kernel, ctx=ctx, rows=rows),
        out_shape=jax.ShapeDtypeStruct((b, l, w), BF16),
        grid=(b, nb),
        in_specs=[pl.BlockSpec((1, TM, w), lambda i, j: (i, j, 0)),
                  pl.BlockSpec((1, l, w), lambda i, j: (i, 0, 0)),
                  pl.BlockSpec((1, l, w), lambda i, j: (i, 0, 0)),
                  pl.BlockSpec((1, NA_HEADS, TM, NA_SLAB * GRID_W), bias_map)],
        out_specs=pl.BlockSpec((1, TM, w), lambda i, j: (i, j, 0)),
        compiler_params=_params(2),
        name="na_attention",
    )(aq, ak, av, bias)


def _na_bias_tables(na_rpb, rows):
    depth = na_rpb.shape[0]
    qc = np.arange(GRID_W)[:, None]
    kc = np.arange(GRID_W)[None, :]
    c0 = np.clip(qc - NA_KW // 2, 0, GRID_W - NA_KW)
    col_ok = (kc >= c0) & (kc < c0 + NA_KW)
    col_idx = np.clip(kc - qc + NA_KW - 1, 0, 2 * NA_KW - 2)
    bc = jnp.take(na_rpb, jnp.asarray(col_idx.reshape(-1)), axis=3).reshape(
        depth, NA_HEADS, 2 * NA_KH - 1, GRID_W, GRID_W)
    i_idx = np.zeros((3, NA_QROWS, NA_SLAB), np.int32)
    ok = np.zeros((3, NA_QROWS, NA_SLAB, GRID_W, GRID_W), bool)
    for case, (r, s) in enumerate(((0, 0), (NA_KH // 2, 0), (rows - NA_QROWS, rows - NA_SLAB))):
        for a in range(NA_QROWS):
            qr = r + a
            r0 = min(max(qr - NA_KH // 2, 0), rows - NA_KH)
            for c in range(NA_SLAB):
                kr = s + c
                inside = r0 <= kr < r0 + NA_KH
                i_idx[case, a, c] = min(max(kr - qr + NA_KH - 1, 0), 2 * NA_KH - 2)
                ok[case, a, c] = col_ok & inside
    t = jnp.take(bc, jnp.asarray(i_idx.reshape(-1)), axis=2).reshape(
        depth, NA_HEADS, 3, NA_QROWS, NA_SLAB, GRID_W, GRID_W)
    t = jnp.where(jnp.asarray(ok)[None, None], t, NEG_INF)
    t = t.transpose(0, 2, 1, 3, 5, 4, 6)
    return t.reshape(depth, 3, NA_HEADS, NA_QROWS * GRID_W, NA_SLAB * GRID_W)


def _mla_kernel(q_ref, k_ref, v_ref, o_ref, *, ctx):
    j = pl.program_id(1)
    lo = _lane_lo()

    def run(nkeys):
        for jp in range(MLA_HEADS // 2):
            vsl = slice(jp * LANES, (jp + 1) * LANES)
            vp = v_ref[0, 0:nkeys, vsl]
            outs = []
            for half in range(2):
                hsl = slice((2 * jp + half) * LANES, (2 * jp + half + 1) * LANES)
                s = _dot_nt(q_ref[0, :, hsl], k_ref[0, 0:nkeys, hsl])
                outs.append(_softmax_pv([s], [vp]))
            o_ref[0, :, vsl] = jnp.where(lo, outs[0], outs[1]).astype(BF16)

    @pl.when(j == 0)
    def _():
        run(ctx)

    @pl.when(j > 0)
    def _():
        run(k_ref.shape[1])


def _mla_attention(bq, bk, bv, ctx):
    b, l, wq = bq.shape
    wv = bv.shape[2]
    return pl.pallas_call(
        functools.partial(_mla_kernel, ctx=ctx),
        out_shape=jax.ShapeDtypeStruct((b, l, wv), BF16),
        grid=(b, l // TM),
        in_specs=[pl.BlockSpec((1, TM, wq), lambda i, j: (i, j, 0)),
                  pl.BlockSpec((1, l, wq), lambda i, j: (i, 0, 0)),
                  pl.BlockSpec((1, l, wv), lambda i, j: (i, 0, 0))],
        out_specs=pl.BlockSpec((1, TM, wv), lambda i, j: (i, j, 0)),
        compiler_params=_params(2),
        name="mla_attention",
    )(bq, bk, bv)


def _swa_kernel(sink_ref, q_ref, k_ref, v_ref, o_ref, *, ctx, n_lat):
    j = pl.program_id(1)
    lo = _lane_lo()
    group = SWA_HEADS // SWA_KV_HEADS
    band = TM + 2 * SWA_WINDOW

    def stacked_q(kv):
        parts = []
        for jp in range(kv * group // 2, (kv + 1) * group // 2):
            parts.extend(_split_heads(q_ref[0, :, jp * LANES:(jp + 1) * LANES], lo))
        return jnp.concatenate(parts, axis=0)

    def finish(kv, score_parts, value_parts):
        outs = []
        for g in range(group):
            rs = slice(g * TM, (g + 1) * TM)
            sink = jnp.full((1, 1), sink_ref[kv * group + g], F32)
            outs.append(_softmax_pv([s[rs] for s in score_parts], value_parts, extra_logit=sink))
        for p in range(group // 2):
            jp = kv * group // 2 + p
            o_ref[0, :, jp * LANES:(jp + 1) * LANES] = jnp.where(lo, outs[2 * p], outs[2 * p + 1]).astype(BF16)

    @pl.when(j == 0)
    def _():
        for kv in range(SWA_KV_HEADS):
            sl = slice(kv * LANES, (kv + 1) * LANES)
            kc, vc = k_ref[0, 0:ctx, sl], v_ref[0, 0:ctx, sl]
            finish(kv, [_dot_nt(stacked_q(kv), kc)], [vc])

    @pl.when(j > 0)
    def _():
        q0 = (j - 1) * TM
        s0 = jnp.clip(q0 - SWA_WINDOW, 0, n_lat - band)
        start = pl.multiple_of(ctx + s0, SWA_WINDOW)
        qpos = q0 + (lax.broadcasted_iota(jnp.int32, (group * TM, band), 0) & (TM - 1))
        kpos = s0 + lax.broadcasted_iota(jnp.int32, (group * TM, band), 1)
        keep = jnp.abs(qpos - kpos) <= SWA_WINDOW
        for kv in range(SWA_KV_HEADS):
            sl = slice(kv * LANES, (kv + 1) * LANES)
            kc, vc = k_ref[0, 0:ctx, sl], v_ref[0, 0:ctx, sl]
            kb, vb = k_ref[0, pl.ds(start, band), sl], v_ref[0, pl.ds(start, band), sl]
            qs = stacked_q(kv)
            s_band = jnp.where(keep, _dot_nt(qs, kb), NEG_INF)
            finish(kv, [_dot_nt(qs, kc), s_band], [vc, vb])


def _swa_attention(sink, cq, ck, cv, ctx):
    b, l, w = cq.shape
    wk = ck.shape[2]
    grid_spec = pltpu.PrefetchScalarGridSpec(
        num_scalar_prefetch=1,
        grid=(b, l // TM),
        in_specs=[pl.BlockSpec((1, TM, w), lambda i, j, s: (i, j, 0)),
                  pl.BlockSpec((1, l, wk), lambda i, j, s: (i, 0, 0)),
                  pl.BlockSpec((1, l, wk), lambda i, j, s: (i, 0, 0))],
        out_specs=pl.BlockSpec((1, TM, w), lambda i, j, s: (i, j, 0)),
    )
    return pl.pallas_call(
        functools.partial(_swa_kernel, ctx=ctx, n_lat=l - ctx),
        out_shape=jax.ShapeDtypeStruct((b, l, w), BF16),
        grid_spec=grid_spec,
        compiler_params=_params(2),
        name="swa_attention",
    )(sink, cq, ck, cv)


def _merge_kernel(oa_ref, ob_ref, oc_ref, gate_ref, x_ref, mod_ref, wbr_ref, wout_ref, g2_ref,
                  rw_ref, rb_ref, xo_ref, h2_ref, route_ref, cnt_ref, run_ref, *, n_exp):
    d = x_ref.shape[2]

    @pl.when(jnp.logical_and(pl.program_id(0) == 0, pl.program_id(1) == 0))
    def _():
        run_ref[...] = jnp.zeros_like(run_ref)

    mod = mod_ref[0, 0]
    mix = None
    for i, o_ref in enumerate((oa_ref, ob_ref, oc_ref)):
        t = gate_ref[0, :, i * d:(i + 1) * d].astype(F32) * _dot(o_ref[0], wbr_ref[i])
        mix = t if mix is None else mix + t
    y = _dot(mix.astype(BF16), wout_ref[...])
    x = x_ref[0] + mod[2:3] * y
    xo_ref[0] = x
    h2 = (_rms(x) * g2_ref[...]) * (1.0 + mod[4:5]) + mod[3:4]
    h2b = h2.astype(BF16)
    h2_ref[0] = _pack_rows(h2b)
    logits = _dot(h2b, rw_ref[...]) + rb_ref[...]
    lane = lax.broadcasted_iota(jnp.int32, logits.shape, 1).astype(F32)
    work = jnp.where(lane < n_exp, logits, -jnp.inf)
    ids, vals = [], []
    for _ in range(TOP_K):
        m = work.max(axis=-1, keepdims=True)
        idx = jnp.where(work == m, lane, float(LANES)).min(axis=-1, keepdims=True)
        ids.append(idx)
        vals.append(m)
        work = jnp.where(lane == idx, -jnp.inf, work)
    ex = [jnp.exp(v - vals[0]) for v in vals]
    den = ex[0] + ex[1] + ex[2] + ex[3]
    hits = jnp.zeros(logits.shape, F32)
    for idx in ids:
        hits = hits + jnp.where(lane == idx, 1.0, 0.0)
    r = lax.broadcasted_iota(jnp.int32, (TM, TM), 0)
    c = lax.broadcasted_iota(jnp.int32, (TM, TM), 1)
    tri = jnp.where(c < r, 1.0, 0.0).astype(BF16)
    before = _dot(tri, hits.astype(BF16)) + run_ref[0:1]
    route = jnp.zeros(logits.shape, F32)
    for k in range(TOP_K):
        rank = jnp.where(lane == ids[k], before, 0.0).sum(axis=-1, keepdims=True)
        route = jnp.where(lane == k, ids[k], route)
        route = jnp.where(lane == TOP_K + k, ex[k] / den, route)
        route = jnp.where(lane == 2 * TOP_K + k, rank, route)
    route_ref[0] = route
    run_ref[...] = run_ref[...] + hits.sum(axis=0, keepdims=True)
    cnt_ref[...] = run_ref[...]


def _merge(oa, ob, oc, gate, x_all, modsel, wbr, wout, g2, rw, rb, n_exp):
    b, l, d = x_all.shape
    row = lambda width: pl.BlockSpec((1, TM, width), lambda i, j: (i, j, 0))
    full = lambda a: pl.BlockSpec(a.shape, lambda i, j: (0,) * a.ndim)
    return pl.pallas_call(
        functools.partial(_merge_kernel, n_exp=n_exp),
        out_shape=[jax.ShapeDtypeStruct((b, l, d), F32),
                   jax.ShapeDtypeStruct((b, l, d // 2), jnp.int32),
                   jax.ShapeDtypeStruct((b, l, LANES), F32),
                   jax.ShapeDtypeStruct((8, LANES), F32)],
        grid=(b, l // TM),
        in_specs=[row(BRANCH_W), row(BRANCH_W), row(BRANCH_W), row(N_BRANCH * d), row(d),
                  pl.BlockSpec((1, 1, 6, d), lambda i, j: (i, jnp.minimum(j, 1), 0, 0)),
                  full(wbr), full(wout), full(g2), full(rw), full(rb)],
        out_specs=[row(d), row(d // 2), row(LANES), pl.BlockSpec((8, LANES), lambda i, j: (0, 0))],
        scratch_shapes=[pltpu.VMEM((8, LANES), F32)],
        compiler_params=_params(2),
        name="merge",
    )(oa, ob, oc, gate, x_all, modsel, wbr, wout, g2, rw, rb)


def _moe_kernel(be_ref, nu_ref, x_ref, wgu_ref, bgu_ref, wdn_ref, bdn_ref, y_ref, wgu_s, wdn_s):
    i = pl.program_id(0)
    ff = wdn_ref.shape[1]

    @pl.when(i < nu_ref[0])
    def _():
        prev = be_ref[jnp.maximum(i - 1, 0)]

        @pl.when(jnp.logical_or(i == 0, be_ref[i] != prev))
        def _():
            wgu_s[...] = wgu_ref[0].astype(BF16)
            wdn_s[...] = wdn_ref[0].astype(BF16)

        gu = _dot(_unpack_rows(x_ref[...]), wgu_s[...]) + bgu_ref[0]
        glu = jnp.minimum(gu[:, :ff], SWIGLU_LIMIT)
        lin = jnp.clip(gu[:, ff:], -SWIGLU_LIMIT, SWIGLU_LIMIT)
        act = glu * _sigmoid(SWIGLU_ALPHA * glu) * (lin + 1.0)
        y_ref[...] = _dot(act.astype(BF16), wdn_s[...]) + bdn_ref[0]


def _moe_experts(blk_e, n_used, xs, w_gu, b_gu, w_dn, b_dn):
    n_rows, packed_w = xs.shape
    n_exp, d, ff2 = w_gu.shape
    ff = ff2 // 2
    n_blocks = n_rows // MOE_TM

    def row_map(i, be, nu):
        return (jnp.minimum(i, nu[0] - 1), 0)

    def w_map(i, be, nu):
        return (be[jnp.minimum(i, nu[0] - 1)], 0, 0)

    grid_spec = pltpu.PrefetchScalarGridSpec(
        num_scalar_prefetch=2,
        grid=(n_blocks,),
        in_specs=[pl.BlockSpec((MOE_TM, packed_w), row_map),
                  pl.BlockSpec((1, d, ff2), w_map),
                  pl.BlockSpec((1, 1, ff2), w_map),
                  pl.BlockSpec((1, ff, d), w_map),
                  pl.BlockSpec((1, 1, d), w_map)],
        out_specs=pl.BlockSpec((MOE_TM, d), row_map),
        scratch_shapes=[pltpu.VMEM((d, ff2), BF16), pltpu.VMEM((ff, d), BF16)],
    )
    return pl.pallas_call(
        _moe_kernel,
        out_shape=jax.ShapeDtypeStruct((n_rows, d), F32),
        grid_spec=grid_spec,
        compiler_params=_params(1),
        name="moe_experts",
    )(blk_e, n_used, xs, w_gu, b_gu.reshape(n_exp, 1, ff2), w_dn, b_dn.reshape(n_exp, 1, d))


def _layout(route, cnt, n_exp):
    b, l, _ = route.shape
    t = b * l
    ids = route[..., 0:TOP_K].astype(jnp.int32)
    rank = route[..., 2 * TOP_K:3 * TOP_K].astype(jnp.int32)
    counts = cnt[0, :n_exp].astype(jnp.int32)
    padded = (counts + MOE_TM - 1) // MOE_TM * MOE_TM
    pad_end = jnp.cumsum(padded)
    pad_start = pad_end - padded
    onehot = ids[..., None] == jnp.arange(n_exp, dtype=jnp.int32)
    dest = jnp.sum(jnp.where(onehot, pad_start, 0), axis=-1) + rank
    dest = dest.reshape(t, TOP_K).T
    n_blocks = -(-t * TOP_K // MOE_TM) + n_exp
    blk_start = jnp.arange(n_blocks, dtype=jnp.int32) * MOE_TM
    blk_e = jnp.minimum(jnp.sum(blk_start[:, None] >= pad_end[None, :], axis=1), n_exp - 1).astype(jnp.int32)
    n_used = (pad_end[-1] // MOE_TM).astype(jnp.int32).reshape(1)
    return dest, blk_e, n_used, n_blocks * MOE_TM


def _sc_mesh():
    return plsc.VectorSubcoreMesh(core_axis_name="c", subcore_axis_name="s",
                                  num_cores=SC_CORES, num_subcores=SC_SUBCORES)


def _sc_worker():
    return lax.axis_index("s") * SC_CORES + lax.axis_index("c")


def _sc_scatter_rows(x, idx, n_rows):
    t, d = x.shape
    n_idx = idx.shape[0]
    workers = SC_CORES * SC_SUBCORES
    n_chunks = n_idx // (workers * SC_SCATTER_ROWS)
    assert n_chunks * workers * SC_SCATTER_ROWS == n_idx and t % SC_SCATTER_ROWS == 0
    idx3 = idx.reshape(workers, n_chunks, SC_SCATTER_ROWS)

    @pl.kernel(out_type=jax.ShapeDtypeStruct((n_rows, d), x.dtype), mesh=_sc_mesh(),
               scratch_types=[pltpu.VMEM((n_chunks, SC_SCATTER_ROWS), jnp.int32),
                              pltpu.VMEM((SC_SCATTER_ROWS, d), x.dtype),
                              pltpu.SemaphoreType.DMA])
    def scatter_kernel(x_hbm, i_hbm, o_hbm, idx_v, rows_v, sem):
        wid = _sc_worker()
        pltpu.sync_copy(i_hbm.at[wid], idx_v)

        @pl.loop(0, n_chunks)
        def _(j):
            src = pl.multiple_of(((wid * n_chunks + j) * SC_SCATTER_ROWS) % t, SC_SCATTER_ROWS)
            pltpu.sync_copy(x_hbm.at[pl.ds(src, SC_SCATTER_ROWS)], rows_v)
            pltpu.async_copy(rows_v, o_hbm.at[idx_v.at[j]], sem).wait()

    return scatter_kernel(x, idx3)


def _sc_gather_rows(x, idx):
    d = x.shape[1]
    n_idx = idx.shape[0]
    workers = SC_CORES * SC_SUBCORES
    per_worker = n_idx // workers
    n_chunks = per_worker // SC_GATHER_ROWS
    assert n_chunks * workers * SC_GATHER_ROWS == n_idx

    @pl.kernel(out_type=jax.ShapeDtypeStruct((n_idx, d), x.dtype), mesh=_sc_mesh(),
               scratch_types=[pltpu.VMEM((per_worker,), jnp.int32),
                              pltpu.VMEM((SC_GATHER_ROWS, d), x.dtype),
                              pltpu.SemaphoreType.DMA])
    def gather_kernel(x_hbm, i_hbm, o_hbm, idx_v, rows_v, sem):
        base = _sc_worker() * per_worker
        pltpu.sync_copy(i_hbm.at[pl.ds(base, per_worker)], idx_v)

        @pl.loop(0, n_chunks)
        def _(j):
            off = pl.multiple_of(j * SC_GATHER_ROWS, SC_GATHER_ROWS)
            pltpu.async_copy(x_hbm.at[idx_v.at[pl.ds(off, SC_GATHER_ROWS)]], rows_v, sem).wait()
            pltpu.sync_copy(rows_v, o_hbm.at[pl.ds(base + off, SC_GATHER_ROWS)])

    return gather_kernel(x, idx)


def _combine(y4_ref, route_ref):
    route = route_ref[0]
    y = None
    for k in range(TOP_K):
        t = route[:, TOP_K + k:TOP_K + k + 1] * y4_ref[k, 0]
        y = t if y is None else y + t
    return y


def _resid_kernel(x_ref, y4_ref, route_ref, mod_ref, o_ref):
    o_ref[0] = x_ref[0] + mod_ref[0, 0][5:6] * _combine(y4_ref, route_ref)


def _final_kernel(x_ref, y4_ref, route_ref, mod_ref, g_ref, o_ref):
    x = x_ref[0] + mod_ref[0, 0][5:6] * _combine(y4_ref, route_ref)
    o_ref[0] = _rms(x) * g_ref[...]


def _residual(x_all, y4, route, modsel, skip, final_g=None):
    b, l, d = x_all.shape
    off = skip // TM
    lo = l - skip
    xspec = pl.BlockSpec((1, TM, d), lambda i, j: (i, j + off, 0))
    yspec = pl.BlockSpec((TOP_K, 1, TM, d), lambda i, j: (0, i, j + off, 0))
    rspec = pl.BlockSpec((1, TM, LANES), lambda i, j: (i, j + off, 0))
    mspec = pl.BlockSpec((1, 1, 6, d), lambda i, j: (i, jnp.minimum(j + off, 1), 0, 0))
    ospec = pl.BlockSpec((1, TM, d), lambda i, j: (i, j, 0))
    if final_g is None:
        return pl.pallas_call(
            _resid_kernel, out_shape=jax.ShapeDtypeStruct((b, lo, d), F32), grid=(b, lo // TM),
            in_specs=[xspec, yspec, rspec, mspec], out_specs=ospec, compiler_params=_params(2), name="residual",
        )(x_all, y4, route, modsel)
    return pl.pallas_call(
        _final_kernel, out_shape=jax.ShapeDtypeStruct((b, lo, d), F32), grid=(b, lo // TM),
        in_specs=[xspec, yspec, rspec, mspec, pl.BlockSpec((1, d), lambda i, j: (0, 0))],
        out_specs=ospec, compiler_params=_params(2), name="residual_final_norm",
    )(x_all, y4, route, modsel, final_g)


def _rope_tables(n, ctx, rot_dim, group_pattern):
    t = jnp.arange(n, dtype=jnp.int32)
    row = (t // GRID_W).astype(F32)
    col = (t % GRID_W).astype(F32)
    per_axis = rot_dim // 2
    inv = ROPE_BASE ** (-jnp.arange(0, per_axis, 2, dtype=F32) / per_axis)
    ang = jnp.concatenate([row[:, None] * inv[None], col[:, None] * inv[None]], axis=-1)
    cos, sin = jnp.cos(ang), jnp.sin(ang)
    half = rot_dim // 2
    c = jnp.ones((n, LANES), F32)
    sdn = jnp.zeros((n, LANES), F32)
    sup = jnp.zeros((n, LANES), F32)
    for off in group_pattern:
        c = c.at[:, off:off + half].set(cos).at[:, off + half:off + rot_dim].set(cos)
        sdn = sdn.at[:, off:off + half].set(-sin)
        sup = sup.at[:, off + half:off + rot_dim].set(sin)
    tab = jnp.stack([c, sdn, sup])
    ident = jnp.stack([jnp.ones((ctx, LANES), F32), jnp.zeros((ctx, LANES), F32), jnp.zeros((ctx, LANES), F32)])
    return jnp.concatenate([ident, tab], axis=1)


def _prep_weights(w_in, mla_w_uq, mla_w_ukv):
    depth, d, _ = w_in.shape
    sizes = (512, 512, 512, MLA_Q_LORA, MLA_KV_LORA, MLA_ROPE, 512, 128, 128, N_BRANCH * d)
    offs = np.concatenate([[0], np.cumsum(sizes)])
    seg = [w_in[:, :, int(offs[i]):int(offs[i + 1])] for i in range(len(sizes))]
    aq, ak, av, ql, kvl, kr, cq, ck, cv, g = seg
    kr128 = jnp.pad(kr, ((0, 0), (0, 0), (MLA_NOPE, LANES - MLA_NOPE - MLA_ROPE)))
    dup = lambda t: jnp.concatenate([t[..., :64], t[..., :64], t[..., 64:], t[..., 64:]], axis=-1)
    w = jnp.concatenate([aq, ak, av, ql, kvl, kr128, cq, dup(ck), dup(cv), g], axis=-1).astype(BF16)
    uq = mla_w_uq.reshape(depth, MLA_Q_LORA, MLA_HEADS, MLA_NOPE + MLA_ROPE)
    uq = jnp.pad(uq, ((0, 0), (0, 0), (0, 0), (0, LANES - MLA_NOPE - MLA_ROPE)))
    uq = uq.reshape(depth, MLA_Q_LORA, MLA_HEADS * LANES).astype(BF16)
    ukv = mla_w_ukv.reshape(depth, MLA_KV_LORA, MLA_HEADS, MLA_NOPE + MLA_V)
    uk = jnp.pad(ukv[..., :MLA_NOPE], ((0, 0), (0, 0), (0, 0), (0, LANES - MLA_NOPE)))
    uk = uk.reshape(depth, MLA_KV_LORA, MLA_HEADS * LANES)
    uv = ukv[..., MLA_NOPE:].reshape(depth, MLA_KV_LORA, MLA_HEADS * MLA_V)
    ukv = jnp.concatenate([uk, uv], axis=-1).astype(BF16)
    return w, uq, ukv


def kernel(x, c, ctx, c_ctx, norm1_g, norm2_g, w_mod, b_mod, w_in, na_rpb, mla_q_norm_g, mla_kv_norm_g, mla_w_uq, mla_w_ukv, swa_sink, w_branch, w_out, router_w, router_b, expert_w_gate_up, expert_b_gate_up, expert_w_down, expert_b_down, final_norm_g):
    b, n, d = x.shape
    lc = ctx.shape[1]
    l = lc + n
    depth = w_in.shape[0]
    n_exp = router_w.shape[2]
    assert lc == TM and n % TM == 0 and TM == NA_QROWS * GRID_W

    cvec = jnp.zeros((8, d), F32).at[:b].set(c).at[b].set(c_ctx)
    mod = _modulation(cvec, w_mod, b_mod)
    mod_lat = mod[:, :b].reshape(depth, b, 1, 6, d)
    mod_ctx = jnp.broadcast_to(mod[:, b].reshape(depth, 1, 1, 6, d), (depth, b, 1, 6, d))
    modsel = jnp.concatenate([mod_ctx, mod_lat], axis=2)

    w_all, uq_all, ukv_all = _prep_weights(w_in, mla_w_uq, mla_w_ukv)
    wbr_all = w_branch.astype(BF16)
    wout_all = w_out.astype(BF16)
    rw_all = jnp.pad(router_w, ((0, 0), (0, 0), (0, LANES - n_exp))).astype(BF16)
    rb_all = jnp.pad(router_b, ((0, 0), (0, LANES - n_exp))).reshape(depth, 1, LANES)
    rope_b = _rope_tables(n, lc, MLA_ROPE, (MLA_NOPE,))
    rope_c = _rope_tables(n, lc, SWA_HEAD_DIM, (0, SWA_HEAD_DIM))
    bias_all = _na_bias_tables(na_rpb, n // GRID_W)

    x_all = jnp.concatenate([ctx, x], axis=1)
    out = None
    for li in range(depth):
        last = li == depth - 1
        ms = modsel[li]
        aq, ak, av, bq, bk, bv, cq, ck, cv, gate = _inproj(
            x_all, ms, norm1_g[li].reshape(1, d), w_all[li], uq_all[li], ukv_all[li],
            mla_q_norm_g[li].reshape(1, -1), mla_kv_norm_g[li].reshape(1, -1), rope_b, rope_c)
        oa = _na_attention(aq, ak, av, bias_all[li], lc)
        ob = _mla_attention(bq, bk, bv, lc)
        oc = _swa_attention(swa_sink[li], cq, ck, cv, lc)
        x_all, h2, route, cnt = _merge(oa, ob, oc, gate, x_all, ms, wbr_all[li], wout_all[li],
                                       norm2_g[li].reshape(1, d), rw_all[li], rb_all[li], n_exp)
        dest, blk_e, n_used, n_rows = _layout(route, cnt, n_exp)
        dest = dest.reshape(TOP_K * b * l)
        xs = _sc_scatter_rows(h2.reshape(b * l, d // 2), dest, n_rows)
        ys = _moe_experts(blk_e, n_used, xs, expert_w_gate_up[li], expert_b_gate_up[li],
                          expert_w_down[li], expert_b_down[li])
        y4 = _sc_gather_rows(ys, dest).reshape(TOP_K, b, l, d)
        if last:
            out = _residual(x_all, y4, route, ms, lc, final_norm_g.reshape(1, d))
        else:
            x_all = _residual(x_all, y4, route, ms, 0)
    return out
```

```python
import functools

import numpy as np
import jax
import jax.numpy as jnp
from jax import lax
from jax.experimental import pallas as pl
from jax.experimental.pallas import tpu as pltpu
from jax.experimental.pallas import tpu_sc as plsc

GRID_W = 64
EPS = 1e-6
ROPE_BASE = 10000.0
NEG_INF = -1e30
LOG2E = 1.4426950408889634
LANES = 128

NA_HEADS = 8
NA_HEAD_DIM = 64
NA_KH = 8
NA_KW = 16
NA_QROWS = 4
NA_SLAB = 12
MLA_HEADS = 8
MLA_NOPE = 64
MLA_ROPE = 32
MLA_V = 64
MLA_Q_LORA = 256
MLA_KV_LORA = 128
SWA_HEADS = 8
SWA_KV_HEADS = 2
SWA_HEAD_DIM = 64
SWA_WINDOW = 128
N_BRANCH = 3
BRANCH_W = 512
N_EXPERTS = 32
TOP_K = 4
SWIGLU_ALPHA = 1.702
SWIGLU_LIMIT = 7.0

TM = 256
MOE_TM = 512
SC_CORES = 2
SC_SUBCORES = 16
SC_SCATTER_ROWS = 128
SC_GATHER_ROWS = 64
VMEM_LIMIT = 56 * 1024 * 1024

BF16 = jnp.bfloat16
F32 = jnp.float32


def _dot(a, b):
    return jnp.dot(a, b, preferred_element_type=F32)


def _dot_nt(a, b):
    return lax.dot_general(a, b, (((1,), (1,)), ((), ())), preferred_element_type=F32)


def _params(n_axes, vmem=VMEM_LIMIT):
    return pltpu.CompilerParams(dimension_semantics=("arbitrary",) * n_axes, vmem_limit_bytes=vmem)


def _layer_spec(a, li):
    return pl.BlockSpec((None,) + a.shape[1:], lambda *_: (li,) + (0,) * (a.ndim - 1))


def _mod_spec(li, off=0):
    return lambda d: pl.BlockSpec((None, 1, 1, 6, d), lambda i, j: (li, i, jnp.minimum(j + off, 1), 0, 0))


def _rms(x):
    return x * lax.rsqrt(jnp.mean(x * x, axis=-1, keepdims=True) + EPS)


def _sigmoid(x):
    return 1.0 / (1.0 + jnp.exp(-x))


def _pack_rows(xb):
    half = xb.shape[1] // 2
    lo = pltpu.bitcast(xb[:, :half].astype(F32), jnp.int32)
    hi = pltpu.bitcast(xb[:, half:].astype(F32), jnp.int32)
    return (hi & jnp.int32(-65536)) | lax.shift_right_logical(lo, 16)


def _unpack_rows(w):
    lo = pltpu.bitcast(lax.shift_left(w, 16), F32).astype(BF16)
    hi = pltpu.bitcast(w & jnp.int32(-65536), F32).astype(BF16)
    return jnp.concatenate([lo, hi], axis=1)


def _mod_kernel(c_ref, w_ref, b_ref, o_ref):
    c = c_ref[...]
    s = (c * _sigmoid(c)).astype(BF16)
    o_ref[0] = _dot(s, w_ref[0].astype(BF16)) + b_ref[0]


def _modulation(cvec, w_mod, b_mod):
    depth, d, n6 = w_mod.shape
    tn = n6 // 4
    return pl.pallas_call(
        _mod_kernel,
        out_shape=jax.ShapeDtypeStruct((depth, 8, n6), F32),
        grid=(depth, n6 // tn),
        in_specs=[pl.BlockSpec((8, d), lambda l, j: (0, 0)),
                  pl.BlockSpec((1, d, tn), lambda l, j: (l, 0, j)),
                  pl.BlockSpec((1, 1, tn), lambda l, j: (l, 0, j))],
        out_specs=pl.BlockSpec((1, 8, tn), lambda l, j: (l, 0, j)),
        compiler_params=_params(2),
        name="modulation",
    )(cvec, w_mod, b_mod.reshape(depth, 1, n6))


def _rope_groups(x, tab_ref, shift):
    cos, sdn, sup = tab_ref[0], tab_ref[1], tab_ref[2]
    outs = []
    for g in range(x.shape[1] // LANES):
        xg = x[:, g * LANES:(g + 1) * LANES]
        outs.append(xg * cos + pltpu.roll(xg, LANES - shift, 1) * sdn + pltpu.roll(xg, shift, 1) * sup)
    return outs[0] if len(outs) == 1 else jnp.concatenate(outs, axis=1)


def _inproj_kernel(x_ref, mod_ref, g1_ref, w_ref, wuq_ref, wukv_ref, gq_ref, gkv_ref, rb_ref, rc_ref,
                   aq_ref, ak_ref, av_ref, bq_ref, bk_ref, bv_ref, cq_ref, ck_ref, cv_ref, gate_ref):
    x = x_ref[0]
    mod = mod_ref[0, 0]
    h = (_rms(x) * g1_ref[...]) * (1.0 + mod[1:2]) + mod[0:1]
    hb = h.astype(BF16)
    acc = _dot(hb, w_ref[:, 0:1536])
    aq_ref[0] = (acc[:, 0:512] * (NA_HEAD_DIM ** -0.5 * LOG2E)).astype(BF16)
    ak_ref[0] = acc[:, 512:1024].astype(BF16)
    av_ref[0] = acc[:, 1024:1536].astype(BF16)
    acc = _dot(hb, w_ref[:, 1536:2048])
    qn = (_rms(acc[:, 0:256]) * gq_ref[...]).astype(BF16)
    kvn = (_rms(acc[:, 256:384]) * gkv_ref[...]).astype(BF16)
    kr = _rope_groups(acc[:, 384:512], rb_ref, MLA_ROPE // 2)
    q = _rope_groups(_dot(qn, wuq_ref[...]), rb_ref, MLA_ROPE // 2)
    bq_ref[0] = (q * ((MLA_NOPE + MLA_ROPE) ** -0.5 * LOG2E)).astype(BF16)
    kv = _dot(kvn, wukv_ref[...])
    bk_ref[0] = (kv[:, 0:1024] + jnp.concatenate([kr] * MLA_HEADS, axis=1)).astype(BF16)
    bv_ref[0] = kv[:, 1024:1536].astype(BF16)
    acc = _dot(hb, w_ref[:, 2048:3072])
    cq_ref[0] = (_rope_groups(acc[:, 0:512], rc_ref, SWA_HEAD_DIM // 2) * (SWA_HEAD_DIM ** -0.5 * LOG2E)).astype(BF16)
    ck_ref[0] = _rope_groups(acc[:, 512:768], rc_ref, SWA_HEAD_DIM // 2).astype(BF16)
    cv_ref[0] = acc[:, 768:1024].astype(BF16)
    gate_ref[0] = _sigmoid(_dot(hb, w_ref[:, 3072:])).astype(BF16)


def _inproj(li, x_all, modsel, g1, w, wuq, wukv, gq, gkv, rope_b, rope_c):
    b, l, d = x_all.shape
    nb = l // TM
    row = lambda width: pl.BlockSpec((1, TM, width), lambda i, j: (i, j, 0))
    full = lambda a: _layer_spec(a, li)
    widths = (512, 512, 512, 1024, 1024, 512, 512, 256, 256, N_BRANCH * d)
    return pl.pallas_call(
        _inproj_kernel,
        out_shape=[jax.ShapeDtypeStruct((b, l, wd), BF16) for wd in widths],
        grid=(b, nb),
        in_specs=[row(d), _mod_spec(li)(d),
                  full(g1), full(w), full(wuq), full(wukv), full(gq), full(gkv),
                  pl.BlockSpec((3, TM, LANES), lambda i, j: (0, j, 0)),
                  pl.BlockSpec((3, TM, LANES), lambda i, j: (0, j, 0))],
        out_specs=[row(wd) for wd in widths],
        compiler_params=_params(2),
        name="inproj",
    )(x_all, modsel, g1, w, wuq, wukv, gq, gkv, rope_b, rope_c)


def _lane_lo():
    return lax.broadcasted_iota(jnp.int32, (1, LANES), 1) < (LANES // 2)


def _split_heads(qp, lo):
    zero = jnp.zeros_like(qp)
    return jnp.where(lo, qp, zero), jnp.where(lo, zero, qp)


def _softmax_pv(score_parts, value_parts, extra_logit=None):
    m = score_parts[0].max(axis=-1, keepdims=True)
    for s in score_parts[1:]:
        m = jnp.maximum(m, s.max(axis=-1, keepdims=True))
    if extra_logit is not None:
        m = jnp.maximum(m, extra_logit)
    den = None
    acc = None
    for s, v in zip(score_parts, value_parts):
        e = jnp.exp2(s - m)
        d = e.sum(axis=-1, keepdims=True)
        den = d if den is None else den + d
        o = _dot(e.astype(BF16), v)
        acc = o if acc is None else acc + o
    if extra_logit is not None:
        den = den + jnp.exp2(extra_logit - m)
    return acc / den


def _na_kernel(q_ref, k_ref, v_ref, bias_ref, o_ref, *, ctx, rows):
    j = pl.program_id(1)
    lo = _lane_lo()

    @pl.when(j == 0)
    def _():
        for jp in range(NA_HEADS // 2):
            sl = slice(jp * LANES, (jp + 1) * LANES)
            kc, vc = k_ref[0, 0:ctx, sl], v_ref[0, 0:ctx, sl]
            outs = [_softmax_pv([_dot_nt(qm, kc)], [vc]) for qm in _split_heads(q_ref[0, :, sl], lo)]
            o_ref[0, :, sl] = jnp.where(lo, outs[0], outs[1]).astype(BF16)

    @pl.when(j > 0)
    def _():
        r = (j - 1) * NA_QROWS
        s0 = jnp.clip(r - NA_KH // 2, 0, rows - NA_SLAB)
        start = pl.multiple_of(ctx + s0 * GRID_W, GRID_W)
        for jp in range(NA_HEADS // 2):
            sl = slice(jp * LANES, (jp + 1) * LANES)
            kc, vc = k_ref[0, 0:ctx, sl], v_ref[0, 0:ctx, sl]
            kn = k_ref[0, pl.ds(start, NA_SLAB * GRID_W), sl]
            vn = v_ref[0, pl.ds(start, NA_SLAB * GRID_W), sl]
            outs = []
            for half, qm in enumerate(_split_heads(q_ref[0, :, sl], lo)):
                s_n = _dot_nt(qm, kn) + bias_ref[0, 2 * jp + half]
                outs.append(_softmax_pv([_dot_nt(qm, kc), s_n], [vc, vn]))
            o_ref[0, :, sl] = jnp.where(lo, outs[0], outs[1]).astype(BF16)


def _na_attention(li, aq, ak, av, bias, ctx):
    b, l, w = aq.shape
    nb = l // TM
    rows = (l - ctx) // GRID_W
    last = nb - 1

    def bias_map(i, j):
        return (li, jnp.where(j <= 1, 0, jnp.where(j == last, 2, 1)), 0, 0, 0)

    return pl.pallas_call(
        functools.partial(_na_kernel, ctx=ctx, rows=rows),
        out_shape=jax.ShapeDtypeStruct((b, l, w), BF16),
        grid=(b, nb),
        in_specs=[pl.BlockSpec((1, TM, w), lambda i, j: (i, j, 0)),
                  pl.BlockSpec((1, l, w), lambda i, j: (i, 0, 0)),
                  pl.BlockSpec((1, l, w), lambda i, j: (i, 0, 0)),
                  pl.BlockSpec((None, 1, NA_HEADS, TM, NA_SLAB * GRID_W), bias_map)],
        out_specs=pl.BlockSpec((1, TM, w), lambda i, j: (i, j, 0)),
        compiler_params=_params(2),
        name="na_attention",
    )(aq, ak, av, bias)


def _na_bias_tables(na_rpb, rows):
    depth = na_rpb.shape[0]
    qc = np.arange(GRID_W)[:, None]
    kc = np.arange(GRID_W)[None, :]
    c0 = np.clip(qc - NA_KW // 2, 0, GRID_W - NA_KW)
    col_ok = (kc >= c0) & (kc < c0 + NA_KW)
    col_idx = np.clip(kc - qc + NA_KW - 1, 0, 2 * NA_KW - 2)
    bc = jnp.take(na_rpb, jnp.asarray(col_idx.reshape(-1)), axis=3).reshape(
        depth, NA_HEADS, 2 * NA_KH - 1, GRID_W, GRID_W)
    i_idx = np.zeros((3, NA_QROWS, NA_SLAB), np.int32)
    ok = np.zeros((3, NA_QROWS, NA_SLAB, GRID_W, GRID_W), bool)
    for case, (r, s) in enumerate(((0, 0), (NA_KH // 2, 0), (rows - NA_QROWS, rows - NA_SLAB))):
        for a in range(NA_QROWS):
            qr = r + a
            r0 = min(max(qr - NA_KH // 2, 0), rows - NA_KH)
            for c in range(NA_SLAB):
                kr = s + c
                inside = r0 <= kr < r0 + NA_KH
                i_idx[case, a, c] = min(max(kr - qr + NA_KH - 1, 0), 2 * NA_KH - 2)
                ok[case, a, c] = col_ok & inside
    t = jnp.take(bc, jnp.asarray(i_idx.reshape(-1)), axis=2).reshape(
        depth, NA_HEADS, 3, NA_QROWS, NA_SLAB, GRID_W, GRID_W)
    t = jnp.where(jnp.asarray(ok)[None, None], t * LOG2E, NEG_INF)
    t = t.transpose(0, 2, 1, 3, 5, 4, 6)
    return t.reshape(depth, 3, NA_HEADS, NA_QROWS * GRID_W, NA_SLAB * GRID_W)


def _mla_kernel(q_ref, k_ref, v_ref, o_ref, *, ctx):
    j = pl.program_id(1)
    lo = _lane_lo()

    def run(nkeys):
        for jp in range(MLA_HEADS // 2):
            vsl = slice(jp * LANES, (jp + 1) * LANES)
            vp = v_ref[0, 0:nkeys, vsl]
            outs = []
            for half in range(2):
                hsl = slice((2 * jp + half) * LANES, (2 * jp + half + 1) * LANES)
                s = _dot_nt(q_ref[0, :, hsl], k_ref[0, 0:nkeys, hsl])
                outs.append(_softmax_pv([s], [vp]))
            o_ref[0, :, vsl] = jnp.where(lo, outs[0], outs[1]).astype(BF16)

    @pl.when(j == 0)
    def _():
        run(ctx)

    @pl.when(j > 0)
    def _():
        run(k_ref.shape[1])


def _mla_attention(bq, bk, bv, ctx):
    b, l, wq = bq.shape
    wv = bv.shape[2]
    return pl.pallas_call(
        functools.partial(_mla_kernel, ctx=ctx),
        out_shape=jax.ShapeDtypeStruct((b, l, wv), BF16),
        grid=(b, l // TM),
        in_specs=[pl.BlockSpec((1, TM, wq), lambda i, j: (i, j, 0)),
                  pl.BlockSpec((1, l, wq), lambda i, j: (i, 0, 0)),
                  pl.BlockSpec((1, l, wv), lambda i, j: (i, 0, 0))],
        out_specs=pl.BlockSpec((1, TM, wv), lambda i, j: (i, j, 0)),
        compiler_params=_params(2),
        name="mla_attention",
    )(bq, bk, bv)


def _swa_kernel(sink_ref, q_ref, k_ref, v_ref, o_ref, *, ctx, n_lat):
    j = pl.program_id(1)
    lo = _lane_lo()
    group = SWA_HEADS // SWA_KV_HEADS
    band = TM + 2 * SWA_WINDOW

    def stacked_q(kv):
        parts = []
        for jp in range(kv * group // 2, (kv + 1) * group // 2):
            parts.extend(_split_heads(q_ref[0, :, jp * LANES:(jp + 1) * LANES], lo))
        return jnp.concatenate(parts, axis=0)

    def finish(kv, score_parts, value_parts):
        outs = []
        for g in range(group):
            rs = slice(g * TM, (g + 1) * TM)
            sink = jnp.full((1, 1), sink_ref[kv * group + g] * LOG2E, F32)
            outs.append(_softmax_pv([s[rs] for s in score_parts], value_parts, extra_logit=sink))
        for p in range(group // 2):
            jp = kv * group // 2 + p
            o_ref[0, :, jp * LANES:(jp + 1) * LANES] = jnp.where(lo, outs[2 * p], outs[2 * p + 1]).astype(BF16)

    @pl.when(j == 0)
    def _():
        for kv in range(SWA_KV_HEADS):
            sl = slice(kv * LANES, (kv + 1) * LANES)
            kc, vc = k_ref[0, 0:ctx, sl], v_ref[0, 0:ctx, sl]
            finish(kv, [_dot_nt(stacked_q(kv), kc)], [vc])

    @pl.when(j > 0)
    def _():
        q0 = (j - 1) * TM
        s0 = jnp.clip(q0 - SWA_WINDOW, 0, n_lat - band)
        start = pl.multiple_of(ctx + s0, SWA_WINDOW)
        qpos = q0 + (lax.broadcasted_iota(jnp.int32, (group * TM, band), 0) & (TM - 1))
        kpos = s0 + lax.broadcasted_iota(jnp.int32, (group * TM, band), 1)
        keep = jnp.abs(qpos - kpos) <= SWA_WINDOW
        for kv in range(SWA_KV_HEADS):
            sl = slice(kv * LANES, (kv + 1) * LANES)
            kc, vc = k_ref[0, 0:ctx, sl], v_ref[0, 0:ctx, sl]
            kb, vb = k_ref[0, pl.ds(start, band), sl], v_ref[0, pl.ds(start, band), sl]
            qs = stacked_q(kv)
            s_band = jnp.where(keep, _dot_nt(qs, kb), NEG_INF)
            finish(kv, [_dot_nt(qs, kc), s_band], [vc, vb])


def _swa_attention(sink, cq, ck, cv, ctx):
    b, l, w = cq.shape
    wk = ck.shape[2]
    grid_spec = pltpu.PrefetchScalarGridSpec(
        num_scalar_prefetch=1,
        grid=(b, l // TM),
        in_specs=[pl.BlockSpec((1, TM, w), lambda i, j, s: (i, j, 0)),
                  pl.BlockSpec((1, l, wk), lambda i, j, s: (i, 0, 0)),
                  pl.BlockSpec((1, l, wk), lambda i, j, s: (i, 0, 0))],
        out_specs=pl.BlockSpec((1, TM, w), lambda i, j, s: (i, j, 0)),
    )
    return pl.pallas_call(
        functools.partial(_swa_kernel, ctx=ctx, n_lat=l - ctx),
        out_shape=jax.ShapeDtypeStruct((b, l, w), BF16),
        grid_spec=grid_spec,
        compiler_params=_params(2),
        name="swa_attention",
    )(sink, cq, ck, cv)


def _merge_kernel(oa_ref, ob_ref, oc_ref, gate_ref, x_ref, mod_ref, wbr_ref, wout_ref, g2_ref,
                  rw_ref, rb_ref, xo_ref, h2_ref, route_ref, cnt_ref, run_ref, *, n_exp):
    d = x_ref.shape[2]

    @pl.when(jnp.logical_and(pl.program_id(0) == 0, pl.program_id(1) == 0))
    def _():
        run_ref[...] = jnp.zeros_like(run_ref)

    mod = mod_ref[0, 0]
    mix = None
    for i, o_ref in enumerate((oa_ref, ob_ref, oc_ref)):
        t = gate_ref[0, :, i * d:(i + 1) * d].astype(F32) * _dot(o_ref[0], wbr_ref[i])
        mix = t if mix is None else mix + t
    y = _dot(mix.astype(BF16), wout_ref[...])
    x = x_ref[0] + mod[2:3] * y
    xo_ref[0] = x
    h2 = (_rms(x) * g2_ref[...]) * (1.0 + mod[4:5]) + mod[3:4]
    h2b = h2.astype(BF16)
    h2_ref[0] = _pack_rows(h2b)
    logits = _dot(h2b, rw_ref[...]) + rb_ref[...]
    lane = lax.broadcasted_iota(jnp.int32, logits.shape, 1).astype(F32)
    work = jnp.where(lane < n_exp, logits, -jnp.inf)
    ids, vals = [], []
    for _ in range(TOP_K):
        m = work.max(axis=-1, keepdims=True)
        idx = jnp.where(work == m, lane, float(LANES)).min(axis=-1, keepdims=True)
        ids.append(idx)
        vals.append(m)
        work = jnp.where(lane == idx, -jnp.inf, work)
    ex = [jnp.exp(v - vals[0]) for v in vals]
    den = ex[0] + ex[1] + ex[2] + ex[3]
    hits = jnp.zeros(logits.shape, F32)
    for idx in ids:
        hits = hits + jnp.where(lane == idx, 1.0, 0.0)
    r = lax.broadcasted_iota(jnp.int32, (TM, TM), 0)
    c = lax.broadcasted_iota(jnp.int32, (TM, TM), 1)
    tri = jnp.where(c < r, 1.0, 0.0).astype(BF16)
    before = _dot(tri, hits.astype(BF16)) + run_ref[0:1]
    route = jnp.zeros(logits.shape, F32)
    for k in range(TOP_K):
        rank = jnp.where(lane == ids[k], before, 0.0).sum(axis=-1, keepdims=True)
        route = jnp.where(lane == k, ids[k], route)
        route = jnp.where(lane == TOP_K + k, ex[k] / den, route)
        route = jnp.where(lane == 2 * TOP_K + k, rank, route)
    route_ref[0] = route
    run_ref[...] = run_ref[...] + hits.sum(axis=0, keepdims=True)
    cnt_ref[...] = run_ref[...]


def _merge(li, oa, ob, oc, gate, x_all, modsel, wbr, wout, g2, rw, rb, n_exp):
    b, l, d = x_all.shape
    row = lambda width: pl.BlockSpec((1, TM, width), lambda i, j: (i, j, 0))
    full = lambda a: _layer_spec(a, li)
    return pl.pallas_call(
        functools.partial(_merge_kernel, n_exp=n_exp),
        out_shape=[jax.ShapeDtypeStruct((b, l, d), F32),
                   jax.ShapeDtypeStruct((b, l, d // 2), jnp.int32),
                   jax.ShapeDtypeStruct((b, l, LANES), F32),
                   jax.ShapeDtypeStruct((8, LANES), F32)],
        grid=(b, l // TM),
        in_specs=[row(BRANCH_W), row(BRANCH_W), row(BRANCH_W), row(N_BRANCH * d), row(d), _mod_spec(li)(d),
                  full(wbr), full(wout), full(g2), full(rw), full(rb)],
        out_specs=[row(d), row(d // 2), row(LANES), pl.BlockSpec((8, LANES), lambda i, j: (0, 0))],
        scratch_shapes=[pltpu.VMEM((8, LANES), F32)],
        compiler_params=_params(2),
        name="merge",
    )(oa, ob, oc, gate, x_all, modsel, wbr, wout, g2, rw, rb)


def _moe_kernel(be_ref, nu_ref, x_ref, wgu_ref, bgu_ref, wdn_ref, bdn_ref, y_ref, wgu_s, wdn_s):
    i = pl.program_id(0)
    ff = wdn_ref.shape[1]

    @pl.when(i < nu_ref[0])
    def _():
        prev = be_ref[jnp.maximum(i - 1, 0)]

        @pl.when(jnp.logical_or(i == 0, be_ref[i] != prev))
        def _():
            wgu_s[...] = wgu_ref[0].astype(BF16)
            wdn_s[...] = wdn_ref[0].astype(BF16)

        gu = _dot(_unpack_rows(x_ref[...]), wgu_s[...]) + bgu_ref[0]
        glu = jnp.minimum(gu[:, :ff], SWIGLU_LIMIT)
        lin = jnp.clip(gu[:, ff:], -SWIGLU_LIMIT, SWIGLU_LIMIT)
        act = glu * _sigmoid(SWIGLU_ALPHA * glu) * (lin + 1.0)
        y_ref[...] = _dot(act.astype(BF16), wdn_s[...]) + bdn_ref[0]


def _moe_experts(li, blk_e, n_used, xs, w_gu, b_gu, w_dn, b_dn):
    n_rows, packed_w = xs.shape
    depth, n_exp, d, ff2 = w_gu.shape
    ff = ff2 // 2
    n_blocks = n_rows // MOE_TM

    def row_map(i, be, nu):
        return (jnp.minimum(i, nu[0] - 1), 0)

    def w_map(i, be, nu):
        return (li, be[jnp.minimum(i, nu[0] - 1)], 0, 0)

    grid_spec = pltpu.PrefetchScalarGridSpec(
        num_scalar_prefetch=2,
        grid=(n_blocks,),
        in_specs=[pl.BlockSpec((MOE_TM, packed_w), row_map),
                  pl.BlockSpec((None, 1, d, ff2), w_map),
                  pl.BlockSpec((None, 1, 1, ff2), w_map),
                  pl.BlockSpec((None, 1, ff, d), w_map),
                  pl.BlockSpec((None, 1, 1, d), w_map)],
        out_specs=pl.BlockSpec((MOE_TM, d), row_map),
        scratch_shapes=[pltpu.VMEM((d, ff2), BF16), pltpu.VMEM((ff, d), BF16)],
    )
    return pl.pallas_call(
        _moe_kernel,
        out_shape=jax.ShapeDtypeStruct((n_rows, d), F32),
        grid_spec=grid_spec,
        compiler_params=_params(1),
        name="moe_experts",
    )(blk_e, n_used, xs, w_gu, b_gu.reshape(depth, n_exp, 1, ff2), w_dn, b_dn.reshape(depth, n_exp, 1, d))


def _layout(route, cnt, n_exp):
    b, l, _ = route.shape
    t = b * l
    ids = route[..., 0:TOP_K].astype(jnp.int32)
    rank = route[..., 2 * TOP_K:3 * TOP_K].astype(jnp.int32)
    counts = cnt[0, :n_exp].astype(jnp.int32)
    padded = (counts + MOE_TM - 1) // MOE_TM * MOE_TM
    pad_end = jnp.cumsum(padded)
    pad_start = pad_end - padded
    onehot = ids[..., None] == jnp.arange(n_exp, dtype=jnp.int32)
    dest = jnp.sum(jnp.where(onehot, pad_start, 0), axis=-1) + rank
    dest = dest.reshape(t, TOP_K).T
    n_blocks = -(-t * TOP_K // MOE_TM) + n_exp
    blk_start = jnp.arange(n_blocks, dtype=jnp.int32) * MOE_TM
    blk_e = jnp.minimum(jnp.sum(blk_start[:, None] >= pad_end[None, :], axis=1), n_exp - 1).astype(jnp.int32)
    n_used = (pad_end[-1] // MOE_TM).astype(jnp.int32).reshape(1)
    return dest, blk_e, n_used, n_blocks * MOE_TM


def _sc_mesh():
    return plsc.VectorSubcoreMesh(core_axis_name="c", subcore_axis_name="s",
                                  num_cores=SC_CORES, num_subcores=SC_SUBCORES)


def _sc_worker():
    return lax.axis_index("s") * SC_CORES + lax.axis_index("c")


def _sc_scatter_rows(x, idx, n_rows):
    t, d = x.shape
    n_idx = idx.shape[0]
    workers = SC_CORES * SC_SUBCORES
    n_chunks = n_idx // (workers * SC_SCATTER_ROWS)
    assert n_chunks * workers * SC_SCATTER_ROWS == n_idx and t % SC_SCATTER_ROWS == 0
    idx3 = idx.reshape(workers, n_chunks, SC_SCATTER_ROWS)

    @pl.kernel(out_type=jax.ShapeDtypeStruct((n_rows, d), x.dtype), mesh=_sc_mesh(),
               scratch_types=[pltpu.VMEM((n_chunks, SC_SCATTER_ROWS), jnp.int32),
                              pltpu.VMEM((SC_SCATTER_ROWS, d), x.dtype),
                              pltpu.SemaphoreType.DMA])
    def scatter_kernel(x_hbm, i_hbm, o_hbm, idx_v, rows_v, sem):
        wid = _sc_worker()
        pltpu.sync_copy(i_hbm.at[wid], idx_v)

        @pl.loop(0, n_chunks)
        def _(j):
            src = pl.multiple_of(((wid * n_chunks + j) * SC_SCATTER_ROWS) % t, SC_SCATTER_ROWS)
            pltpu.sync_copy(x_hbm.at[pl.ds(src, SC_SCATTER_ROWS)], rows_v)
            pltpu.async_copy(rows_v, o_hbm.at[idx_v.at[j]], sem).wait()

    return scatter_kernel(x, idx3)


def _sc_gather_rows(x, idx):
    d = x.shape[1]
    n_idx = idx.shape[0]
    workers = SC_CORES * SC_SUBCORES
    per_worker = n_idx // workers
    n_chunks = per_worker // SC_GATHER_ROWS
    assert n_chunks * workers * SC_GATHER_ROWS == n_idx

    @pl.kernel(out_type=jax.ShapeDtypeStruct((n_idx, d), x.dtype), mesh=_sc_mesh(),
               scratch_types=[pltpu.VMEM((per_worker,), jnp.int32),
                              pltpu.VMEM((SC_GATHER_ROWS, d), x.dtype),
                              pltpu.SemaphoreType.DMA])
    def gather_kernel(x_hbm, i_hbm, o_hbm, idx_v, rows_v, sem):
        base = _sc_worker() * per_worker
        pltpu.sync_copy(i_hbm.at[pl.ds(base, per_worker)], idx_v)

        @pl.loop(0, n_chunks)
        def _(j):
            off = pl.multiple_of(j * SC_GATHER_ROWS, SC_GATHER_ROWS)
            pltpu.async_copy(x_hbm.at[idx_v.at[pl.ds(off, SC_GATHER_ROWS)]], rows_v, sem).wait()
            pltpu.sync_copy(rows_v, o_hbm.at[pl.ds(base + off, SC_GATHER_ROWS)])

    return gather_kernel(x, idx)


def _combine(y4_ref, route_ref):
    route = route_ref[0]
    y = None
    for k in range(TOP_K):
        t = route[:, TOP_K + k:TOP_K + k + 1] * y4_ref[k, 0]
        y = t if y is None else y + t
    return y


def _resid_kernel(x_ref, y4_ref, route_ref, mod_ref, o_ref):
    o_ref[0] = x_ref[0] + mod_ref[0, 0][5:6] * _combine(y4_ref, route_ref)


def _final_kernel(x_ref, y4_ref, route_ref, mod_ref, g_ref, o_ref):
    x = x_ref[0] + mod_ref[0, 0][5:6] * _combine(y4_ref, route_ref)
    o_ref[0] = _rms(x) * g_ref[...]


def _residual(li, x_all, y4, route, modsel, skip, final_g=None):
    b, l, d = x_all.shape
    off = skip // TM
    lo = l - skip
    xspec = pl.BlockSpec((1, TM, d), lambda i, j: (i, j + off, 0))
    yspec = pl.BlockSpec((TOP_K, 1, TM, d), lambda i, j: (0, i, j + off, 0))
    rspec = pl.BlockSpec((1, TM, LANES), lambda i, j: (i, j + off, 0))
    mspec = _mod_spec(li, off)(d)
    ospec = pl.BlockSpec((1, TM, d), lambda i, j: (i, j, 0))
    if final_g is None:
        return pl.pallas_call(
            _resid_kernel, out_shape=jax.ShapeDtypeStruct((b, lo, d), F32), grid=(b, lo // TM),
            in_specs=[xspec, yspec, rspec, mspec], out_specs=ospec, compiler_params=_params(2), name="residual",
        )(x_all, y4, route, modsel)
    return pl.pallas_call(
        _final_kernel, out_shape=jax.ShapeDtypeStruct((b, lo, d), F32), grid=(b, lo // TM),
        in_specs=[xspec, yspec, rspec, mspec, pl.BlockSpec((1, d), lambda i, j: (0, 0))],
        out_specs=ospec, compiler_params=_params(2), name="residual_final_norm",
    )(x_all, y4, route, modsel, final_g)


def _rope_tables(n, ctx, rot_dim, group_pattern):
    t = jnp.arange(n, dtype=jnp.int32)
    row = (t // GRID_W).astype(F32)
    col = (t % GRID_W).astype(F32)
    per_axis = rot_dim // 2
    inv = ROPE_BASE ** (-jnp.arange(0, per_axis, 2, dtype=F32) / per_axis)
    ang = jnp.concatenate([row[:, None] * inv[None], col[:, None] * inv[None]], axis=-1)
    cos, sin = jnp.cos(ang), jnp.sin(ang)
    half = rot_dim // 2
    c = jnp.ones((n, LANES), F32)
    sdn = jnp.zeros((n, LANES), F32)
    sup = jnp.zeros((n, LANES), F32)
    for off in group_pattern:
        c = c.at[:, off:off + half].set(cos).at[:, off + half:off + rot_dim].set(cos)
        sdn = sdn.at[:, off:off + half].set(-sin)
        sup = sup.at[:, off + half:off + rot_dim].set(sin)
    tab = jnp.stack([c, sdn, sup])
    ident = jnp.stack([jnp.ones((ctx, LANES), F32), jnp.zeros((ctx, LANES), F32), jnp.zeros((ctx, LANES), F32)])
    return jnp.concatenate([ident, tab], axis=1)


def _wprep_kernel(w_ref, o_ref):
    x = w_ref[0]
    rows = x.shape[0]
    o = 3 * 512 + MLA_Q_LORA + MLA_KV_LORA
    zeros = lambda n: jnp.zeros((rows, n), F32)
    kr = jnp.concatenate([zeros(MLA_NOPE), x[:, o:o + MLA_ROPE], zeros(LANES - MLA_NOPE - MLA_ROPE)], axis=1)
    o += MLA_ROPE
    cq = x[:, o:o + 512]
    k0, k1 = x[:, o + 512:o + 576], x[:, o + 576:o + 640]
    v0, v1 = x[:, o + 640:o + 704], x[:, o + 704:o + 768]
    g = x[:, o + 768:]
    o_ref[0] = jnp.concatenate([x[:, :3 * 512 + MLA_Q_LORA + MLA_KV_LORA], kr, cq, k0, k0, k1, k1, v0, v0, v1, v1, g],
                               axis=1).astype(BF16)


def _wprep(w_in):
    depth, d, cols = w_in.shape
    out_cols = cols - MLA_ROPE + LANES + 2 * 128
    tr = 128
    return pl.pallas_call(
        _wprep_kernel,
        out_shape=jax.ShapeDtypeStruct((depth, d, out_cols), BF16),
        grid=(depth, d // tr),
        in_specs=[pl.BlockSpec((1, tr, cols), lambda l, i: (l, i, 0))],
        out_specs=pl.BlockSpec((1, tr, out_cols), lambda l, i: (l, i, 0)),
        compiler_params=_params(2),
        name="wprep",
    )(w_in)


def _prep_weights(w_in, mla_w_uq, mla_w_ukv):
    depth = w_in.shape[0]
    w = _wprep(w_in)
    uq = mla_w_uq.reshape(depth, MLA_Q_LORA, MLA_HEADS, MLA_NOPE + MLA_ROPE)
    uq = jnp.pad(uq, ((0, 0), (0, 0), (0, 0), (0, LANES - MLA_NOPE - MLA_ROPE)))
    uq = uq.reshape(depth, MLA_Q_LORA, MLA_HEADS * LANES).astype(BF16)
    ukv = mla_w_ukv.reshape(depth, MLA_KV_LORA, MLA_HEADS, MLA_NOPE + MLA_V)
    uk = jnp.pad(ukv[..., :MLA_NOPE], ((0, 0), (0, 0), (0, 0), (0, LANES - MLA_NOPE)))
    uk = uk.reshape(depth, MLA_KV_LORA, MLA_HEADS * LANES)
    uv = ukv[..., MLA_NOPE:].reshape(depth, MLA_KV_LORA, MLA_HEADS * MLA_V)
    ukv = jnp.concatenate([uk, uv], axis=-1).astype(BF16)
    return w, uq, ukv


def kernel(x, c, ctx, c_ctx, norm1_g, norm2_g, w_mod, b_mod, w_in, na_rpb, mla_q_norm_g, mla_kv_norm_g, mla_w_uq, mla_w_ukv, swa_sink, w_branch, w_out, router_w, router_b, expert_w_gate_up, expert_b_gate_up, expert_w_down, expert_b_down, final_norm_g):
    b, n, d = x.shape
    lc = ctx.shape[1]
    l = lc + n
    depth = w_in.shape[0]
    n_exp = router_w.shape[2]
    assert lc == TM and n % TM == 0 and TM == NA_QROWS * GRID_W

    cvec = jnp.zeros((8, d), F32).at[:b].set(c).at[b].set(c_ctx)
    mod = _modulation(cvec, w_mod, b_mod)
    mod_lat = mod[:, :b].reshape(depth, b, 1, 6, d)
    mod_ctx = jnp.broadcast_to(mod[:, b].reshape(depth, 1, 1, 6, d), (depth, b, 1, 6, d))
    modsel = jnp.concatenate([mod_ctx, mod_lat], axis=2)

    w_all, uq_all, ukv_all = _prep_weights(w_in, mla_w_uq, mla_w_ukv)
    wbr_all = w_branch.astype(BF16)
    wout_all = w_out.astype(BF16)
    rw_all = jnp.pad(router_w, ((0, 0), (0, 0), (0, LANES - n_exp))).astype(BF16)
    rb_all = jnp.pad(router_b, ((0, 0), (0, LANES - n_exp))).reshape(depth, 1, LANES)
    rope_b = _rope_tables(n, lc, MLA_ROPE, (MLA_NOPE,))
    rope_c = _rope_tables(n, lc, SWA_HEAD_DIM, (0, SWA_HEAD_DIM))
    bias_all = _na_bias_tables(na_rpb, n // GRID_W)

    g1_all = norm1_g.reshape(depth, 1, d)
    g2_all = norm2_g.reshape(depth, 1, d)
    gq_all = mla_q_norm_g.reshape(depth, 1, -1)
    gkv_all = mla_kv_norm_g.reshape(depth, 1, -1)

    x_all = jnp.concatenate([ctx, x], axis=1)
    out = None
    for li in range(depth):
        last = li == depth - 1
        ms = modsel
        aq, ak, av, bq, bk, bv, cq, ck, cv, gate = _inproj(
            li, x_all, ms, g1_all, w_all, uq_all, ukv_all, gq_all, gkv_all, rope_b, rope_c)
        oa = _na_attention(li, aq, ak, av, bias_all, lc)
        ob = _mla_attention(bq, bk, bv, lc)
        oc = _swa_attention(swa_sink[li], cq, ck, cv, lc)
        x_all, h2, route, cnt = _merge(li, oa, ob, oc, gate, x_all, ms, wbr_all, wout_all,
                                       g2_all, rw_all, rb_all, n_exp)
        dest, blk_e, n_used, n_rows = _layout(route, cnt, n_exp)
        dest = dest.reshape(TOP_K * b * l)
        xs = _sc_scatter_rows(h2.reshape(b * l, d // 2), dest, n_rows)
        ys = _moe_experts(li, blk_e, n_used, xs, expert_w_gate_up, expert_b_gate_up,
                          expert_w_down, expert_b_down)
        y4 = _sc_gather_rows(ys, dest).reshape(TOP_K, b, l, d)
        if last:
            out = _residual(li, x_all, y4, route, ms, lc, final_norm_g.reshape(1, d))
        else:
            x_all = _residual(li, x_all, y4, route, ms, 0)
    return out
```

```python
import functools

import numpy as np
import jax
import jax.numpy as jnp
from jax import lax
from jax.experimental import pallas as pl
from jax.experimental.pallas import tpu as pltpu
from jax.experimental.pallas import tpu_sc as plsc

GRID_W = 64
EPS = 1e-6
ROPE_BASE = 10000.0
NEG_INF = -1e30
LOG2E = 1.4426950408889634
LANES = 128

NA_HEADS = 8
NA_HEAD_DIM = 64
NA_KH = 8
NA_KW = 16
NA_QROWS = 4
NA_SLAB = 12
MLA_HEADS = 8
MLA_NOPE = 64
MLA_ROPE = 32
MLA_V = 64
MLA_Q_LORA = 256
MLA_KV_LORA = 128
SWA_HEADS = 8
SWA_KV_HEADS = 2
SWA_HEAD_DIM = 64
SWA_WINDOW = 128
N_BRANCH = 3
BRANCH_W = 512
N_EXPERTS = 32
TOP_K = 4
SWIGLU_ALPHA = 1.702
SWIGLU_LIMIT = 7.0

TM = 256
MOE_TM = 512
SC_CORES = 2
SC_SUBCORES = 16
SC_SCATTER_ROWS = 128
SC_GATHER_ROWS = 64
VMEM_LIMIT = 56 * 1024 * 1024

BF16 = jnp.bfloat16
F32 = jnp.float32


def _dot(a, b):
    return jnp.dot(a, b, preferred_element_type=F32)


def _dot_nt(a, b):
    return lax.dot_general(a, b, (((1,), (1,)), ((), ())), preferred_element_type=F32)


def _params(n_axes, vmem=VMEM_LIMIT):
    return pltpu.CompilerParams(dimension_semantics=("arbitrary",) * n_axes, vmem_limit_bytes=vmem)


def _layer_spec(a, li):
    return pl.BlockSpec((None,) + a.shape[1:], lambda *_: (li,) + (0,) * (a.ndim - 1))


def _mod_spec(li, off=0):
    return lambda d: pl.BlockSpec((None, 1, 1, 6, d), lambda i, j: (li, i, jnp.minimum(j + off, 1), 0, 0))


def _rms(x):
    return x * lax.rsqrt(jnp.mean(x * x, axis=-1, keepdims=True) + EPS)


def _sigmoid(x):
    return 1.0 / (1.0 + jnp.exp(-x))


def _pack_rows(xb):
    half = xb.shape[1] // 2
    lo = pltpu.bitcast(xb[:, :half].astype(F32), jnp.int32)
    hi = pltpu.bitcast(xb[:, half:].astype(F32), jnp.int32)
    return (hi & jnp.int32(-65536)) | lax.shift_right_logical(lo, 16)


def _unpack_rows(w):
    lo = pltpu.bitcast(lax.shift_left(w, 16), F32).astype(BF16)
    hi = pltpu.bitcast(w & jnp.int32(-65536), F32).astype(BF16)
    return jnp.concatenate([lo, hi], axis=1)


def _mod_kernel(c_ref, w_ref, b_ref, o_ref):
    c = c_ref[...]
    s = (c * _sigmoid(c)).astype(BF16)
    o_ref[0] = _dot(s, w_ref[0].astype(BF16)) + b_ref[0]


def _modulation(cvec, w_mod, b_mod):
    depth, d, n6 = w_mod.shape
    tn = n6 // 4
    return pl.pallas_call(
        _mod_kernel,
        out_shape=jax.ShapeDtypeStruct((depth, 8, n6), F32),
        grid=(depth, n6 // tn),
        in_specs=[pl.BlockSpec((8, d), lambda l, j: (0, 0)),
                  pl.BlockSpec((1, d, tn), lambda l, j: (l, 0, j)),
                  pl.BlockSpec((1, 1, tn), lambda l, j: (l, 0, j))],
        out_specs=pl.BlockSpec((1, 8, tn), lambda l, j: (l, 0, j)),
        compiler_params=_params(2),
        name="modulation",
    )(cvec, w_mod, b_mod.reshape(depth, 1, n6))


def _rope_groups(x, tab_ref, shift):
    cos, sdn, sup = tab_ref[0], tab_ref[1], tab_ref[2]
    outs = []
    for g in range(x.shape[1] // LANES):
        xg = x[:, g * LANES:(g + 1) * LANES]
        outs.append(xg * cos + pltpu.roll(xg, LANES - shift, 1) * sdn + pltpu.roll(xg, shift, 1) * sup)
    return outs[0] if len(outs) == 1 else jnp.concatenate(outs, axis=1)


def _inproj_kernel(*refs, with_moe):
    if with_moe:
        x_ref, y4_ref, route_ref, modp_ref = refs[:4]
        refs = refs[4:]
        (mod_ref, g1_ref, w_ref, wuq_ref, wukv_ref, gq_ref, gkv_ref, rb_ref, rc_ref, xo_ref,
         aq_ref, ak_ref, av_ref, bq_ref, bk_ref, bv_ref, cq_ref, ck_ref, cv_ref, gate_ref) = refs
        x = x_ref[0] + modp_ref[0, 0][5:6] * _combine(y4_ref, route_ref)
        xo_ref[0] = x
    else:
        (x_ref, mod_ref, g1_ref, w_ref, wuq_ref, wukv_ref, gq_ref, gkv_ref, rb_ref, rc_ref,
         aq_ref, ak_ref, av_ref, bq_ref, bk_ref, bv_ref, cq_ref, ck_ref, cv_ref, gate_ref) = refs
        x = x_ref[0]
    mod = mod_ref[0, 0]
    h = (_rms(x) * g1_ref[...]) * (1.0 + mod[1:2]) + mod[0:1]
    hb = h.astype(BF16)
    acc = _dot(hb, w_ref[:, 0:1536])
    aq_ref[0] = (acc[:, 0:512] * (NA_HEAD_DIM ** -0.5 * LOG2E)).astype(BF16)
    ak_ref[0] = acc[:, 512:1024].astype(BF16)
    av_ref[0] = acc[:, 1024:1536].astype(BF16)
    acc = _dot(hb, w_ref[:, 1536:2048])
    qn = (_rms(acc[:, 0:256]) * gq_ref[...]).astype(BF16)
    kvn = (_rms(acc[:, 256:384]) * gkv_ref[...]).astype(BF16)
    kr = _rope_groups(acc[:, 384:512], rb_ref, MLA_ROPE // 2)
    q = _rope_groups(_dot(qn, wuq_ref[...]), rb_ref, MLA_ROPE // 2)
    bq_ref[0] = (q * ((MLA_NOPE + MLA_ROPE) ** -0.5 * LOG2E)).astype(BF16)
    kv = _dot(kvn, wukv_ref[...])
    bk_ref[0] = (kv[:, 0:1024] + jnp.concatenate([kr] * MLA_HEADS, axis=1)).astype(BF16)
    bv_ref[0] = kv[:, 1024:1536].astype(BF16)
    acc = _dot(hb, w_ref[:, 2048:3072])
    cq_ref[0] = (_rope_groups(acc[:, 0:512], rc_ref, SWA_HEAD_DIM // 2) * (SWA_HEAD_DIM ** -0.5 * LOG2E)).astype(BF16)
    ck_ref[0] = _rope_groups(acc[:, 512:768], rc_ref, SWA_HEAD_DIM // 2).astype(BF16)
    cv_ref[0] = acc[:, 768:1024].astype(BF16)
    gate_ref[0] = _sigmoid(_dot(hb, w_ref[:, 3072:])).astype(BF16)


def _inproj(li, x_all, moe, modsel, g1, w, wuq, wukv, gq, gkv, rope_b, rope_c):
    b, l, d = x_all.shape
    nb = l // TM
    row = lambda width: pl.BlockSpec((1, TM, width), lambda i, j: (i, j, 0))
    full = lambda a: _layer_spec(a, li)
    widths = (512, 512, 512, 1024, 1024, 512, 512, 256, 256, N_BRANCH * d)
    in_specs = [row(d)]
    out_specs = [row(wd) for wd in widths]
    out_shape = [jax.ShapeDtypeStruct((b, l, wd), BF16) for wd in widths]
    args = [x_all]
    if moe is not None:
        y4, route = moe
        in_specs += [pl.BlockSpec((TOP_K, 1, TM, y4.shape[3]), lambda i, j: (0, i, j, 0)), row(LANES),
                     _mod_spec(li - 1)(d)]
        out_specs = [row(d)] + out_specs
        out_shape = [jax.ShapeDtypeStruct((b, l, d), F32)] + out_shape
        args += [y4, route, modsel]
    in_specs += [_mod_spec(li)(d), full(g1), full(w), full(wuq), full(wukv), full(gq), full(gkv),
                 pl.BlockSpec((3, TM, LANES), lambda i, j: (0, j, 0)),
                 pl.BlockSpec((3, TM, LANES), lambda i, j: (0, j, 0))]
    args += [modsel, g1, w, wuq, wukv, gq, gkv, rope_b, rope_c]
    return pl.pallas_call(
        functools.partial(_inproj_kernel, with_moe=moe is not None),
        out_shape=out_shape,
        grid=(b, nb),
        in_specs=in_specs,
        out_specs=out_specs,
        compiler_params=_params(2),
        name="inproj",
    )(*args)


def _lane_lo():
    return lax.broadcasted_iota(jnp.int32, (1, LANES), 1) < (LANES // 2)


def _split_heads(qp, lo):
    zero = jnp.zeros_like(qp)
    return jnp.where(lo, qp, zero), jnp.where(lo, zero, qp)


def _softmax_pv(score_parts, value_parts, extra_logit=None):
    m = score_parts[0].max(axis=-1, keepdims=True)
    for s in score_parts[1:]:
        m = jnp.maximum(m, s.max(axis=-1, keepdims=True))
    if extra_logit is not None:
        m = jnp.maximum(m, extra_logit)
    den = None
    acc = None
    for s, v in zip(score_parts, value_parts):
        e = jnp.exp2(s - m)
        d = e.sum(axis=-1, keepdims=True)
        den = d if den is None else den + d
        o = _dot(e.astype(BF16), v)
        acc = o if acc is None else acc + o
    if extra_logit is not None:
        den = den + jnp.exp2(extra_logit - m)
    return acc / den


def _na_kernel(q_ref, k_ref, v_ref, bias_ref, o_ref, *, ctx, rows):
    j = pl.program_id(1)
    lo = _lane_lo()

    @pl.when(j == 0)
    def _():
        for jp in range(NA_HEADS // 2):
            sl = slice(jp * LANES, (jp + 1) * LANES)
            kc, vc = k_ref[0, 0:ctx, sl], v_ref[0, 0:ctx, sl]
            outs = [_softmax_pv([_dot_nt(qm, kc)], [vc]) for qm in _split_heads(q_ref[0, :, sl], lo)]
            o_ref[0, :, sl] = jnp.where(lo, outs[0], outs[1]).astype(BF16)

    @pl.when(j > 0)
    def _():
        r = (j - 1) * NA_QROWS
        s0 = jnp.clip(r - NA_KH // 2, 0, rows - NA_SLAB)
        start = pl.multiple_of(ctx + s0 * GRID_W, GRID_W)
        for jp in range(NA_HEADS // 2):
            sl = slice(jp * LANES, (jp + 1) * LANES)
            kc, vc = k_ref[0, 0:ctx, sl], v_ref[0, 0:ctx, sl]
            kn = k_ref[0, pl.ds(start, NA_SLAB * GRID_W), sl]
            vn = v_ref[0, pl.ds(start, NA_SLAB * GRID_W), sl]
            outs = []
            for half, qm in enumerate(_split_heads(q_ref[0, :, sl], lo)):
                s_n = _dot_nt(qm, kn) + bias_ref[0, 2 * jp + half]
                outs.append(_softmax_pv([_dot_nt(qm, kc), s_n], [vc, vn]))
            o_ref[0, :, sl] = jnp.where(lo, outs[0], outs[1]).astype(BF16)


def _na_attention(li, aq, ak, av, bias, ctx):
    b, l, w = aq.shape
    nb = l // TM
    rows = (l - ctx) // GRID_W
    last = nb - 1

    def bias_map(i, j):
        return (li, jnp.where(j <= 1, 0, jnp.where(j == last, 2, 1)), 0, 0, 0)

    return pl.pallas_call(
        functools.partial(_na_kernel, ctx=ctx, rows=rows),
        out_shape=jax.ShapeDtypeStruct((b, l, w), BF16),
        grid=(b, nb),
        in_specs=[pl.BlockSpec((1, TM, w), lambda i, j: (i, j, 0)),
                  pl.BlockSpec((1, l, w), lambda i, j: (i, 0, 0)),
                  pl.BlockSpec((1, l, w), lambda i, j: (i, 0, 0)),
                  pl.BlockSpec((None, 1, NA_HEADS, TM, NA_SLAB * GRID_W), bias_map)],
        out_specs=pl.BlockSpec((1, TM, w), lambda i, j: (i, j, 0)),
        compiler_params=_params(2),
        name="na_attention",
    )(aq, ak, av, bias)


def _na_bias_tables(na_rpb, rows):
    depth = na_rpb.shape[0]
    qc = np.arange(GRID_W)[:, None]
    kc = np.arange(GRID_W)[None, :]
    c0 = np.clip(qc - NA_KW // 2, 0, GRID_W - NA_KW)
    col_ok = (kc >= c0) & (kc < c0 + NA_KW)
    col_idx = np.clip(kc - qc + NA_KW - 1, 0, 2 * NA_KW - 2)
    bc = jnp.take(na_rpb, jnp.asarray(col_idx.reshape(-1)), axis=3).reshape(
        depth, NA_HEADS, 2 * NA_KH - 1, GRID_W, GRID_W)
    i_idx = np.zeros((3, NA_QROWS, NA_SLAB), np.int32)
    ok = np.zeros((3, NA_QROWS, NA_SLAB, GRID_W, GRID_W), bool)
    for case, (r, s) in enumerate(((0, 0), (NA_KH // 2, 0), (rows - NA_QROWS, rows - NA_SLAB))):
        for a in range(NA_QROWS):
            qr = r + a
            r0 = min(max(qr - NA_KH // 2, 0), rows - NA_KH)
            for c in range(NA_SLAB):
                kr = s + c
                inside = r0 <= kr < r0 + NA_KH
                i_idx[case, a, c] = min(max(kr - qr + NA_KH - 1, 0), 2 * NA_KH - 2)
                ok[case, a, c] = col_ok & inside
    t = jnp.take(bc, jnp.asarray(i_idx.reshape(-1)), axis=2).reshape(
        depth, NA_HEADS, 3, NA_QROWS, NA_SLAB, GRID_W, GRID_W)
    t = jnp.where(jnp.asarray(ok)[None, None], t * LOG2E, NEG_INF)
    t = t.transpose(0, 2, 1, 3, 5, 4, 6)
    return t.reshape(depth, 3, NA_HEADS, NA_QROWS * GRID_W, NA_SLAB * GRID_W)


def _mla_kernel(q_ref, k_ref, v_ref, o_ref, *, ctx):
    j = pl.program_id(1)
    lo = _lane_lo()

    def run(nkeys):
        for jp in range(MLA_HEADS // 2):
            vsl = slice(jp * LANES, (jp + 1) * LANES)
            vp = v_ref[0, 0:nkeys, vsl]
            outs = []
            for half in range(2):
                hsl = slice((2 * jp + half) * LANES, (2 * jp + half + 1) * LANES)
                s = _dot_nt(q_ref[0, :, hsl], k_ref[0, 0:nkeys, hsl])
                outs.append(_softmax_pv([s], [vp]))
            o_ref[0, :, vsl] = jnp.where(lo, outs[0], outs[1]).astype(BF16)

    @pl.when(j == 0)
    def _():
        run(ctx)

    @pl.when(j > 0)
    def _():
        run(k_ref.shape[1])


def _mla_attention(bq, bk, bv, ctx):
    b, l, wq = bq.shape
    wv = bv.shape[2]
    return pl.pallas_call(
        functools.partial(_mla_kernel, ctx=ctx),
        out_shape=jax.ShapeDtypeStruct((b, l, wv), BF16),
        grid=(b, l // TM),
        in_specs=[pl.BlockSpec((1, TM, wq), lambda i, j: (i, j, 0)),
                  pl.BlockSpec((1, l, wq), lambda i, j: (i, 0, 0)),
                  pl.BlockSpec((1, l, wv), lambda i, j: (i, 0, 0))],
        out_specs=pl.BlockSpec((1, TM, wv), lambda i, j: (i, j, 0)),
        compiler_params=_params(2),
        name="mla_attention",
    )(bq, bk, bv)


def _swa_kernel(sink_ref, q_ref, k_ref, v_ref, o_ref, *, ctx, n_lat):
    j = pl.program_id(1)
    lo = _lane_lo()
    group = SWA_HEADS // SWA_KV_HEADS
    band = TM + 2 * SWA_WINDOW

    def stacked_q(kv):
        parts = []
        for jp in range(kv * group // 2, (kv + 1) * group // 2):
            parts.extend(_split_heads(q_ref[0, :, jp * LANES:(jp + 1) * LANES], lo))
        return jnp.concatenate(parts, axis=0)

    def finish(kv, score_parts, value_parts):
        outs = []
        for g in range(group):
            rs = slice(g * TM, (g + 1) * TM)
            sink = jnp.full((1, 1), sink_ref[kv * group + g] * LOG2E, F32)
            outs.append(_softmax_pv([s[rs] for s in score_parts], value_parts, extra_logit=sink))
        for p in range(group // 2):
            jp = kv * group // 2 + p
            o_ref[0, :, jp * LANES:(jp + 1) * LANES] = jnp.where(lo, outs[2 * p], outs[2 * p + 1]).astype(BF16)

    @pl.when(j == 0)
    def _():
        for kv in range(SWA_KV_HEADS):
            sl = slice(kv * LANES, (kv + 1) * LANES)
            kc, vc = k_ref[0, 0:ctx, sl], v_ref[0, 0:ctx, sl]
            finish(kv, [_dot_nt(stacked_q(kv), kc)], [vc])

    @pl.when(j > 0)
    def _():
        q0 = (j - 1) * TM
        s0 = jnp.clip(q0 - SWA_WINDOW, 0, n_lat - band)
        start = pl.multiple_of(ctx + s0, SWA_WINDOW)
        qpos = q0 + (lax.broadcasted_iota(jnp.int32, (group * TM, band), 0) & (TM - 1))
        kpos = s0 + lax.broadcasted_iota(jnp.int32, (group * TM, band), 1)
        keep = jnp.abs(qpos - kpos) <= SWA_WINDOW
        for kv in range(SWA_KV_HEADS):
            sl = slice(kv * LANES, (kv + 1) * LANES)
            kc, vc = k_ref[0, 0:ctx, sl], v_ref[0, 0:ctx, sl]
            kb, vb = k_ref[0, pl.ds(start, band), sl], v_ref[0, pl.ds(start, band), sl]
            qs = stacked_q(kv)
            s_band = jnp.where(keep, _dot_nt(qs, kb), NEG_INF)
            finish(kv, [_dot_nt(qs, kc), s_band], [vc, vb])


def _swa_attention(sink, cq, ck, cv, ctx):
    b, l, w = cq.shape
    wk = ck.shape[2]
    grid_spec = pltpu.PrefetchScalarGridSpec(
        num_scalar_prefetch=1,
        grid=(b, l // TM),
        in_specs=[pl.BlockSpec((1, TM, w), lambda i, j, s: (i, j, 0)),
                  pl.BlockSpec((1, l, wk), lambda i, j, s: (i, 0, 0)),
                  pl.BlockSpec((1, l, wk), lambda i, j, s: (i, 0, 0))],
        out_specs=pl.BlockSpec((1, TM, w), lambda i, j, s: (i, j, 0)),
    )
    return pl.pallas_call(
        functools.partial(_swa_kernel, ctx=ctx, n_lat=l - ctx),
        out_shape=jax.ShapeDtypeStruct((b, l, w), BF16),
        grid_spec=grid_spec,
        compiler_params=_params(2),
        name="swa_attention",
    )(sink, cq, ck, cv)


def _merge_kernel(oa_ref, ob_ref, oc_ref, gate_ref, x_ref, mod_ref, wbr_ref, wout_ref, g2_ref,
                  rw_ref, rb_ref, xo_ref, h2_ref, route_ref, cnt_ref, run_ref, *, n_exp):
    d = x_ref.shape[2]

    @pl.when(jnp.logical_and(pl.program_id(0) == 0, pl.program_id(1) == 0))
    def _():
        run_ref[...] = jnp.zeros_like(run_ref)

    mod = mod_ref[0, 0]
    mix = None
    for i, o_ref in enumerate((oa_ref, ob_ref, oc_ref)):
        t = gate_ref[0, :, i * d:(i + 1) * d].astype(F32) * _dot(o_ref[0], wbr_ref[i])
        mix = t if mix is None else mix + t
    y = _dot(mix.astype(BF16), wout_ref[...])
    x = x_ref[0] + mod[2:3] * y
    xo_ref[0] = x
    h2 = (_rms(x) * g2_ref[...]) * (1.0 + mod[4:5]) + mod[3:4]
    h2b = h2.astype(BF16)
    h2_ref[0] = _pack_rows(h2b)
    logits = _dot(h2b, rw_ref[...]) + rb_ref[...]
    lane = lax.broadcasted_iota(jnp.int32, logits.shape, 1).astype(F32)
    work = jnp.where(lane < n_exp, logits, -jnp.inf)
    ids, vals = [], []
    for _ in range(TOP_K):
        m = work.max(axis=-1, keepdims=True)
        idx = jnp.where(work == m, lane, float(LANES)).min(axis=-1, keepdims=True)
        ids.append(idx)
        vals.append(m)
        work = jnp.where(lane == idx, -jnp.inf, work)
    ex = [jnp.exp(v - vals[0]) for v in vals]
    den = ex[0] + ex[1] + ex[2] + ex[3]
    hits = jnp.zeros(logits.shape, F32)
    for idx in ids:
        hits = hits + jnp.where(lane == idx, 1.0, 0.0)
    r = lax.broadcasted_iota(jnp.int32, (TM, TM), 0)
    c = lax.broadcasted_iota(jnp.int32, (TM, TM), 1)
    tri = jnp.where(c < r, 1.0, 0.0).astype(BF16)
    before = _dot(tri, hits.astype(BF16)) + run_ref[0:1]
    route = jnp.zeros(logits.shape, F32)
    for k in range(TOP_K):
        rank = jnp.where(lane == ids[k], before, 0.0).sum(axis=-1, keepdims=True)
        route = jnp.where(lane == k, ids[k], route)
        route = jnp.where(lane == TOP_K + k, ex[k] / den, route)
        route = jnp.where(lane == 2 * TOP_K + k, rank, route)
    route_ref[0] = route
    run_ref[...] = run_ref[...] + hits.sum(axis=0, keepdims=True)
    cnt_ref[...] = run_ref[...]


def _merge(li, oa, ob, oc, gate, x_all, modsel, wbr, wout, g2, rw, rb, n_exp):
    b, l, d = x_all.shape
    row = lambda width: pl.BlockSpec((1, TM, width), lambda i, j: (i, j, 0))
    full = lambda a: _layer_spec(a, li)
    return pl.pallas_call(
        functools.partial(_merge_kernel, n_exp=n_exp),
        out_shape=[jax.ShapeDtypeStruct((b, l, d), F32),
                   jax.ShapeDtypeStruct((b, l, d // 2), jnp.int32),
                   jax.ShapeDtypeStruct((b, l, LANES), F32),
                   jax.ShapeDtypeStruct((8, LANES), F32)],
        grid=(b, l // TM),
        in_specs=[row(BRANCH_W), row(BRANCH_W), row(BRANCH_W), row(N_BRANCH * d), row(d), _mod_spec(li)(d),
                  full(wbr), full(wout), full(g2), full(rw), full(rb)],
        out_specs=[row(d), row(d // 2), row(LANES), pl.BlockSpec((8, LANES), lambda i, j: (0, 0))],
        scratch_shapes=[pltpu.VMEM((8, LANES), F32)],
        compiler_params=_params(2),
        name="merge",
    )(oa, ob, oc, gate, x_all, modsel, wbr, wout, g2, rw, rb)


def _moe_kernel(be_ref, nu_ref, x_ref, wgu_ref, bgu_ref, wdn_ref, bdn_ref, y_ref, wgu_s, wdn_s):
    i = pl.program_id(0)
    ff = wdn_ref.shape[1]

    @pl.when(i < nu_ref[0])
    def _():
        prev = be_ref[jnp.maximum(i - 1, 0)]

        @pl.when(jnp.logical_or(i == 0, be_ref[i] != prev))
        def _():
            wgu_s[...] = wgu_ref[0].astype(BF16)
            wdn_s[...] = wdn_ref[0].astype(BF16)

        gu = _dot(_unpack_rows(x_ref[...]), wgu_s[...]) + bgu_ref[0]
        glu = jnp.minimum(gu[:, :ff], SWIGLU_LIMIT)
        lin = jnp.clip(gu[:, ff:], -SWIGLU_LIMIT, SWIGLU_LIMIT)
        act = glu * _sigmoid(SWIGLU_ALPHA * glu) * (lin + 1.0)
        y_ref[...] = _pack_rows((_dot(act.astype(BF16), wdn_s[...]) + bdn_ref[0]).astype(BF16))


def _moe_experts(li, blk_e, n_used, xs, w_gu, b_gu, w_dn, b_dn):
    n_rows, packed_w = xs.shape
    depth, n_exp, d, ff2 = w_gu.shape
    ff = ff2 // 2
    n_blocks = n_rows // MOE_TM

    def row_map(i, be, nu):
        return (jnp.minimum(i, nu[0] - 1), 0)

    def w_map(i, be, nu):
        return (li, be[jnp.minimum(i, nu[0] - 1)], 0, 0)

    grid_spec = pltpu.PrefetchScalarGridSpec(
        num_scalar_prefetch=2,
        grid=(n_blocks,),
        in_specs=[pl.BlockSpec((MOE_TM, packed_w), row_map),
                  pl.BlockSpec((None, 1, d, ff2), w_map),
                  pl.BlockSpec((None, 1, 1, ff2), w_map),
                  pl.BlockSpec((None, 1, ff, d), w_map),
                  pl.BlockSpec((None, 1, 1, d), w_map)],
        out_specs=pl.BlockSpec((MOE_TM, packed_w), row_map),
        scratch_shapes=[pltpu.VMEM((d, ff2), BF16), pltpu.VMEM((ff, d), BF16)],
    )
    return pl.pallas_call(
        _moe_kernel,
        out_shape=jax.ShapeDtypeStruct((n_rows, packed_w), jnp.int32),
        grid_spec=grid_spec,
        compiler_params=_params(1),
        name="moe_experts",
    )(blk_e, n_used, xs, w_gu, b_gu.reshape(depth, n_exp, 1, ff2), w_dn, b_dn.reshape(depth, n_exp, 1, d))


def _layout(route, cnt, n_exp):
    b, l, _ = route.shape
    t = b * l
    ids = route[..., 0:TOP_K].astype(jnp.int32)
    rank = route[..., 2 * TOP_K:3 * TOP_K].astype(jnp.int32)
    counts = cnt[0, :n_exp].astype(jnp.int32)
    padded = (counts + MOE_TM - 1) // MOE_TM * MOE_TM
    pad_end = jnp.cumsum(padded)
    pad_start = pad_end - padded
    onehot = ids[..., None] == jnp.arange(n_exp, dtype=jnp.int32)
    dest = jnp.sum(jnp.where(onehot, pad_start, 0), axis=-1) + rank
    dest = dest.reshape(t, TOP_K).T
    n_blocks = -(-t * TOP_K // MOE_TM) + n_exp
    blk_start = jnp.arange(n_blocks, dtype=jnp.int32) * MOE_TM
    blk_e = jnp.minimum(jnp.sum(blk_start[:, None] >= pad_end[None, :], axis=1), n_exp - 1).astype(jnp.int32)
    n_used = (pad_end[-1] // MOE_TM).astype(jnp.int32).reshape(1)
    return dest, blk_e, n_used, n_blocks * MOE_TM


def _sc_mesh():
    return plsc.VectorSubcoreMesh(core_axis_name="c", subcore_axis_name="s",
                                  num_cores=SC_CORES, num_subcores=SC_SUBCORES)


def _sc_worker():
    return lax.axis_index("s") * SC_CORES + lax.axis_index("c")


def _sc_scatter_rows(x, idx, n_rows):
    t, d = x.shape
    n_idx = idx.shape[0]
    workers = SC_CORES * SC_SUBCORES
    n_chunks = n_idx // (workers * SC_SCATTER_ROWS)
    assert n_chunks * workers * SC_SCATTER_ROWS == n_idx and t % SC_SCATTER_ROWS == 0
    idx3 = idx.reshape(workers, n_chunks, SC_SCATTER_ROWS)

    @pl.kernel(out_type=jax.ShapeDtypeStruct((n_rows, d), x.dtype), mesh=_sc_mesh(),
               scratch_types=[pltpu.VMEM((n_chunks, SC_SCATTER_ROWS), jnp.int32),
                              pltpu.VMEM((SC_SCATTER_ROWS, d), x.dtype),
                              pltpu.SemaphoreType.DMA])
    def scatter_kernel(x_hbm, i_hbm, o_hbm, idx_v, rows_v, sem):
        wid = _sc_worker()
        pltpu.sync_copy(i_hbm.at[wid], idx_v)

        @pl.loop(0, n_chunks)
        def _(j):
            src = pl.multiple_of(((wid * n_chunks + j) * SC_SCATTER_ROWS) % t, SC_SCATTER_ROWS)
            pltpu.sync_copy(x_hbm.at[pl.ds(src, SC_SCATTER_ROWS)], rows_v)
            pltpu.async_copy(rows_v, o_hbm.at[idx_v.at[j]], sem).wait()

    return scatter_kernel(x, idx3)


def _sc_gather_rows(x, idx):
    d = x.shape[1]
    n_idx = idx.shape[0]
    workers = SC_CORES * SC_SUBCORES
    per_worker = n_idx // workers
    n_chunks = per_worker // SC_GATHER_ROWS
    assert n_chunks * workers * SC_GATHER_ROWS == n_idx

    assert n_chunks % 2 == 0

    @pl.kernel(out_type=jax.ShapeDtypeStruct((n_idx, d), x.dtype), mesh=_sc_mesh(),
               scratch_types=[pltpu.VMEM((per_worker,), jnp.int32),
                              pltpu.VMEM((SC_GATHER_ROWS, d), x.dtype),
                              pltpu.VMEM((SC_GATHER_ROWS, d), x.dtype),
                              pltpu.SemaphoreType.DMA, pltpu.SemaphoreType.DMA])
    def gather_kernel(x_hbm, i_hbm, o_hbm, idx_v, buf0, buf1, sem0, sem1):
        base = _sc_worker() * per_worker
        pltpu.sync_copy(i_hbm.at[pl.ds(base, per_worker)], idx_v)
        bufs, sems = (buf0, buf1), (sem0, sem1)

        def gather(chunk, b):
            off = pl.multiple_of(chunk * SC_GATHER_ROWS, SC_GATHER_ROWS)
            return pltpu.make_async_copy(x_hbm.at[idx_v.at[pl.ds(off, SC_GATHER_ROWS)]], bufs[b], sems[b])

        gather(0, 0).start()

        @pl.loop(0, n_chunks, step=2)
        def _(j):
            for b in range(2):
                chunk = j + b
                gather(chunk, b).wait()
                if b == 0:
                    gather(chunk + 1, 1).start()
                else:
                    @pl.when(chunk + 1 < n_chunks)
                    def _():
                        gather(chunk + 1, 0).start()
                off = pl.multiple_of(chunk * SC_GATHER_ROWS, SC_GATHER_ROWS)
                pltpu.sync_copy(bufs[b], o_hbm.at[pl.ds(base + off, SC_GATHER_ROWS)])

    return gather_kernel(x, idx)


def _combine(y4_ref, route_ref):
    route = route_ref[0]
    y = None
    for k in range(TOP_K):
        t = route[:, TOP_K + k:TOP_K + k + 1] * _unpack_rows(y4_ref[k, 0]).astype(F32)
        y = t if y is None else y + t
    return y


def _final_kernel(x_ref, y4_ref, route_ref, mod_ref, g_ref, o_ref):
    x = x_ref[0] + mod_ref[0, 0][5:6] * _combine(y4_ref, route_ref)
    o_ref[0] = _rms(x) * g_ref[...]


def _final(li, x_all, y4, route, modsel, skip, final_g):
    b, l, d = x_all.shape
    off = skip // TM
    lo = l - skip
    return pl.pallas_call(
        _final_kernel, out_shape=jax.ShapeDtypeStruct((b, lo, d), F32), grid=(b, lo // TM),
        in_specs=[pl.BlockSpec((1, TM, d), lambda i, j: (i, j + off, 0)),
                  pl.BlockSpec((TOP_K, 1, TM, y4.shape[3]), lambda i, j: (0, i, j + off, 0)),
                  pl.BlockSpec((1, TM, LANES), lambda i, j: (i, j + off, 0)),
                  _mod_spec(li, off)(d),
                  pl.BlockSpec((1, d), lambda i, j: (0, 0))],
        out_specs=pl.BlockSpec((1, TM, d), lambda i, j: (i, j, 0)),
        compiler_params=_params(2), name="final_norm",
    )(x_all, y4, route, modsel, final_g)


def _rope_tables(n, ctx, rot_dim, group_pattern):
    t = jnp.arange(n, dtype=jnp.int32)
    row = (t // GRID_W).astype(F32)
    col = (t % GRID_W).astype(F32)
    per_axis = rot_dim // 2
    inv = ROPE_BASE ** (-jnp.arange(0, per_axis, 2, dtype=F32) / per_axis)
    ang = jnp.concatenate([row[:, None] * inv[None], col[:, None] * inv[None]], axis=-1)
    cos, sin = jnp.cos(ang), jnp.sin(ang)
    half = rot_dim // 2
    c = jnp.ones((n, LANES), F32)
    sdn = jnp.zeros((n, LANES), F32)
    sup = jnp.zeros((n, LANES), F32)
    for off in group_pattern:
        c = c.at[:, off:off + half].set(cos).at[:, off + half:off + rot_dim].set(cos)
        sdn = sdn.at[:, off:off + half].set(-sin)
        sup = sup.at[:, off + half:off + rot_dim].set(sin)
    tab = jnp.stack([c, sdn, sup])
    ident = jnp.stack([jnp.ones((ctx, LANES), F32), jnp.zeros((ctx, LANES), F32), jnp.zeros((ctx, LANES), F32)])
    return jnp.concatenate([ident, tab], axis=1)


def _wprep_kernel(w_ref, o_ref):
    x = w_ref[0]
    rows = x.shape[0]
    o = 3 * 512 + MLA_Q_LORA + MLA_KV_LORA
    zeros = lambda n: jnp.zeros((rows, n), F32)
    kr = jnp.concatenate([zeros(MLA_NOPE), x[:, o:o + MLA_ROPE], zeros(LANES - MLA_NOPE - MLA_ROPE)], axis=1)
    o += MLA_ROPE
    cq = x[:, o:o + 512]
    k0, k1 = x[:, o + 512:o + 576], x[:, o + 576:o + 640]
    v0, v1 = x[:, o + 640:o + 704], x[:, o + 704:o + 768]
    g = x[:, o + 768:]
    o_ref[0] = jnp.concatenate([x[:, :3 * 512 + MLA_Q_LORA + MLA_KV_LORA], kr, cq, k0, k0, k1, k1, v0, v0, v1, v1, g],
                               axis=1).astype(BF16)


def _wprep(w_in):
    depth, d, cols = w_in.shape
    out_cols = cols - MLA_ROPE + LANES + 2 * 128
    tr = 128
    return pl.pallas_call(
        _wprep_kernel,
        out_shape=jax.ShapeDtypeStruct((depth, d, out_cols), BF16),
        grid=(depth, d // tr),
        in_specs=[pl.BlockSpec((1, tr, cols), lambda l, i: (l, i, 0))],
        out_specs=pl.BlockSpec((1, tr, out_cols), lambda l, i: (l, i, 0)),
        compiler_params=_params(2),
        name="wprep",
    )(w_in)


def _prep_weights(w_in, mla_w_uq, mla_w_ukv):
    depth = w_in.shape[0]
    w = _wprep(w_in)
    uq = mla_w_uq.reshape(depth, MLA_Q_LORA, MLA_HEADS, MLA_NOPE + MLA_ROPE)
    uq = jnp.pad(uq, ((0, 0), (0, 0), (0, 0), (0, LANES - MLA_NOPE - MLA_ROPE)))
    uq = uq.reshape(depth, MLA_Q_LORA, MLA_HEADS * LANES).astype(BF16)
    ukv = mla_w_ukv.reshape(depth, MLA_KV_LORA, MLA_HEADS, MLA_NOPE + MLA_V)
    uk = jnp.pad(ukv[..., :MLA_NOPE], ((0, 0), (0, 0), (0, 0), (0, LANES - MLA_NOPE)))
    uk = uk.reshape(depth, MLA_KV_LORA, MLA_HEADS * LANES)
    uv = ukv[..., MLA_NOPE:].reshape(depth, MLA_KV_LORA, MLA_HEADS * MLA_V)
    ukv = jnp.concatenate([uk, uv], axis=-1).astype(BF16)
    return w, uq, ukv


def kernel(x, c, ctx, c_ctx, norm1_g, norm2_g, w_mod, b_mod, w_in, na_rpb, mla_q_norm_g, mla_kv_norm_g, mla_w_uq, mla_w_ukv, swa_sink, w_branch, w_out, router_w, router_b, expert_w_gate_up, expert_b_gate_up, expert_w_down, expert_b_down, final_norm_g):
    b, n, d = x.shape
    lc = ctx.shape[1]
    l = lc + n
    depth = w_in.shape[0]
    n_exp = router_w.shape[2]
    assert lc == TM and n % TM == 0 and TM == NA_QROWS * GRID_W

    cvec = jnp.zeros((8, d), F32).at[:b].set(c).at[b].set(c_ctx)
    mod = _modulation(cvec, w_mod, b_mod)
    mod_lat = mod[:, :b].reshape(depth, b, 1, 6, d)
    mod_ctx = jnp.broadcast_to(mod[:, b].reshape(depth, 1, 1, 6, d), (depth, b, 1, 6, d))
    modsel = jnp.concatenate([mod_ctx, mod_lat], axis=2)

    w_all, uq_all, ukv_all = _prep_weights(w_in, mla_w_uq, mla_w_ukv)
    wbr_all = w_branch.astype(BF16)
    wout_all = w_out.astype(BF16)
    rw_all = jnp.pad(router_w, ((0, 0), (0, 0), (0, LANES - n_exp))).astype(BF16)
    rb_all = jnp.pad(router_b, ((0, 0), (0, LANES - n_exp))).reshape(depth, 1, LANES)
    rope_b = _rope_tables(n, lc, MLA_ROPE, (MLA_NOPE,))
    rope_c = _rope_tables(n, lc, SWA_HEAD_DIM, (0, SWA_HEAD_DIM))
    bias_all = _na_bias_tables(na_rpb, n // GRID_W)

    g1_all = norm1_g.reshape(depth, 1, d)
    g2_all = norm2_g.reshape(depth, 1, d)
    gq_all = mla_q_norm_g.reshape(depth, 1, -1)
    gkv_all = mla_kv_norm_g.reshape(depth, 1, -1)

    x_all = jnp.concatenate([ctx, x], axis=1)
    moe = None
    for li in range(depth):
        ms = modsel
        outs = _inproj(li, x_all, moe, ms, g1_all, w_all, uq_all, ukv_all, gq_all, gkv_all, rope_b, rope_c)
        if moe is not None:
            x_all, outs = outs[0], outs[1:]
        aq, ak, av, bq, bk, bv, cq, ck, cv, gate = outs
        oa = _na_attention(li, aq, ak, av, bias_all, lc)
        ob = _mla_attention(bq, bk, bv, lc)
        oc = _swa_attention(swa_sink[li], cq, ck, cv, lc)
        x_all, h2, route, cnt = _merge(li, oa, ob, oc, gate, x_all, ms, wbr_all, wout_all,
                                       g2_all, rw_all, rb_all, n_exp)
        dest, blk_e, n_used, n_rows = _layout(route, cnt, n_exp)
        dest = dest.reshape(TOP_K * b * l)
        xs = _sc_scatter_rows(h2.reshape(b * l, d // 2), dest, n_rows)
        ys = _moe_experts(li, blk_e, n_used, xs, expert_w_gate_up, expert_b_gate_up,
                          expert_w_down, expert_b_down)
        moe = (_sc_gather_rows(ys, dest).reshape(TOP_K, b, l, d // 2), route)
    return _final(depth - 1, x_all, moe[0], moe[1], modsel, lc, final_norm_g.reshape(1, d))
```

```python
import functools

import numpy as np
import jax
import jax.numpy as jnp
from jax import lax
from jax.experimental import pallas as pl
from jax.experimental.pallas import tpu as pltpu
from jax.experimental.pallas import tpu_sc as plsc

GRID_W = 64
EPS = 1e-6
ROPE_BASE = 10000.0
NEG_INF = -1e30
LOG2E = 1.4426950408889634
LANES = 128

NA_HEADS = 8
NA_HEAD_DIM = 64
NA_KH = 8
NA_KW = 16
NA_QROWS = 4
NA_SLAB = 12
MLA_HEADS = 8
MLA_NOPE = 64
MLA_ROPE = 32
MLA_V = 64
MLA_Q_LORA = 256
MLA_KV_LORA = 128
SWA_HEADS = 8
SWA_KV_HEADS = 2
SWA_HEAD_DIM = 64
SWA_WINDOW = 128
N_BRANCH = 3
BRANCH_W = 512
N_EXPERTS = 32
TOP_K = 4
SWIGLU_ALPHA = 1.702
SWIGLU_LIMIT = 7.0

TM = 256
MOE_TM = 512
SC_CORES = 2
SC_SUBCORES = 16
SC_SCATTER_ROWS = 128
SC_GATHER_ROWS = 64
VMEM_LIMIT = 56 * 1024 * 1024

BF16 = jnp.bfloat16
F32 = jnp.float32


def _dot(a, b):
    return jnp.dot(a, b, preferred_element_type=F32)


def _dot_nt(a, b):
    return lax.dot_general(a, b, (((1,), (1,)), ((), ())), preferred_element_type=F32)


def _params(n_axes, vmem=VMEM_LIMIT):
    return pltpu.CompilerParams(dimension_semantics=("arbitrary",) * n_axes, vmem_limit_bytes=vmem)


def _layer_spec(a, li):
    return pl.BlockSpec((None,) + a.shape[1:], lambda *_: (li,) + (0,) * (a.ndim - 1))


def _mod_spec(li, off=0):
    return lambda d: pl.BlockSpec((None, 1, 1, 6, d), lambda i, j: (li, i, jnp.minimum(j + off, 1), 0, 0))


def _rms(x):
    return x * lax.rsqrt(jnp.mean(x * x, axis=-1, keepdims=True) + EPS)


def _sigmoid(x):
    return 1.0 / (1.0 + jnp.exp(-x))


def _pack_rows(xb):
    half = xb.shape[1] // 2
    lo = pltpu.bitcast(xb[:, :half].astype(F32), jnp.int32)
    hi = pltpu.bitcast(xb[:, half:].astype(F32), jnp.int32)
    return (hi & jnp.int32(-65536)) | lax.shift_right_logical(lo, 16)


def _unpack_rows(w):
    lo = pltpu.bitcast(lax.shift_left(w, 16), F32).astype(BF16)
    hi = pltpu.bitcast(w & jnp.int32(-65536), F32).astype(BF16)
    return jnp.concatenate([lo, hi], axis=1)


def _mod_kernel(c_ref, w_ref, b_ref, o_ref):
    c = c_ref[...]
    s = (c * _sigmoid(c)).astype(BF16)
    o_ref[0] = _dot(s, w_ref[0].astype(BF16)) + b_ref[0]


def _modulation(cvec, w_mod, b_mod):
    depth, d, n6 = w_mod.shape
    tn = n6 // 4
    return pl.pallas_call(
        _mod_kernel,
        out_shape=jax.ShapeDtypeStruct((depth, 8, n6), F32),
        grid=(depth, n6 // tn),
        in_specs=[pl.BlockSpec((8, d), lambda l, j: (0, 0)),
                  pl.BlockSpec((1, d, tn), lambda l, j: (l, 0, j)),
                  pl.BlockSpec((1, 1, tn), lambda l, j: (l, 0, j))],
        out_specs=pl.BlockSpec((1, 8, tn), lambda l, j: (l, 0, j)),
        compiler_params=_params(2),
        name="modulation",
    )(cvec, w_mod, b_mod.reshape(depth, 1, n6))


def _rope_groups(x, tab_ref, shift):
    cos, sdn, sup = tab_ref[0], tab_ref[1], tab_ref[2]
    outs = []
    for g in range(x.shape[1] // LANES):
        xg = x[:, g * LANES:(g + 1) * LANES]
        outs.append(xg * cos + pltpu.roll(xg, LANES - shift, 1) * sdn + pltpu.roll(xg, shift, 1) * sup)
    return outs[0] if len(outs) == 1 else jnp.concatenate(outs, axis=1)


def _inproj_kernel(*refs, with_moe):
    if with_moe:
        x_ref, y4_ref, route_ref, modp_ref = refs[:4]
        refs = refs[4:]
        (mod_ref, g1_ref, w_ref, wuq_ref, wukv_ref, gq_ref, gkv_ref, rb_ref, rc_ref, xo_ref,
         aq_ref, ak_ref, av_ref, bq_ref, bk_ref, bv_ref, cq_ref, ck_ref, cv_ref, gate_ref) = refs
        x = x_ref[0] + modp_ref[0, 0][5:6] * _combine(y4_ref, route_ref)
        xo_ref[0] = x
    else:
        (x_ref, mod_ref, g1_ref, w_ref, wuq_ref, wukv_ref, gq_ref, gkv_ref, rb_ref, rc_ref,
         aq_ref, ak_ref, av_ref, bq_ref, bk_ref, bv_ref, cq_ref, ck_ref, cv_ref, gate_ref) = refs
        x = x_ref[0]
    mod = mod_ref[0, 0]
    h = (_rms(x) * g1_ref[...]) * (1.0 + mod[1:2]) + mod[0:1]
    hb = h.astype(BF16)
    acc = _dot(hb, w_ref[:, 0:1536])
    aq_ref[0] = (acc[:, 0:512] * (NA_HEAD_DIM ** -0.5 * LOG2E)).astype(BF16)
    ak_ref[0] = acc[:, 512:1024].astype(BF16)
    av_ref[0] = acc[:, 1024:1536].astype(BF16)
    acc = _dot(hb, w_ref[:, 1536:2048])
    qn = (_rms(acc[:, 0:256]) * gq_ref[...]).astype(BF16)
    kvn = (_rms(acc[:, 256:384]) * gkv_ref[...]).astype(BF16)
    kr = _rope_groups(acc[:, 384:512], rb_ref, MLA_ROPE // 2)
    q = _rope_groups(_dot(qn, wuq_ref[...]), rb_ref, MLA_ROPE // 2)
    bq_ref[0] = (q * ((MLA_NOPE + MLA_ROPE) ** -0.5 * LOG2E)).astype(BF16)
    kv = _dot(kvn, wukv_ref[...])
    bk_ref[0] = (kv[:, 0:1024] + jnp.concatenate([kr] * MLA_HEADS, axis=1)).astype(BF16)
    bv_ref[0] = kv[:, 1024:1536].astype(BF16)
    acc = _dot(hb, w_ref[:, 2048:3072])
    cq_ref[0] = (_rope_groups(acc[:, 0:512], rc_ref, SWA_HEAD_DIM // 2) * (SWA_HEAD_DIM ** -0.5 * LOG2E)).astype(BF16)
    ck_ref[0] = _rope_groups(acc[:, 512:768], rc_ref, SWA_HEAD_DIM // 2).astype(BF16)
    cv_ref[0] = acc[:, 768:1024].astype(BF16)
    gate_ref[0] = _sigmoid(_dot(hb, w_ref[:, 3072:])).astype(BF16)


def _inproj(li, x_all, moe, modsel, g1, w, wuq, wukv, gq, gkv, rope_b, rope_c):
    b, l, d = x_all.shape
    nb = l // TM
    row = lambda width: pl.BlockSpec((1, TM, width), lambda i, j: (i, j, 0))
    full = lambda a: _layer_spec(a, li)
    widths = (512, 512, 512, 1024, 1024, 512, 512, 256, 256, N_BRANCH * d)
    in_specs = [row(d)]
    out_specs = [row(wd) for wd in widths]
    out_shape = [jax.ShapeDtypeStruct((b, l, wd), BF16) for wd in widths]
    args = [x_all]
    if moe is not None:
        y4, route = moe
        in_specs += [pl.BlockSpec((TOP_K, 1, TM, y4.shape[3]), lambda i, j: (0, i, j, 0)), row(LANES),
                     _mod_spec(li - 1)(d)]
        out_specs = [row(d)] + out_specs
        out_shape = [jax.ShapeDtypeStruct((b, l, d), F32)] + out_shape
        args += [y4, route, modsel]
    in_specs += [_mod_spec(li)(d), full(g1), full(w), full(wuq), full(wukv), full(gq), full(gkv),
                 pl.BlockSpec((3, TM, LANES), lambda i, j: (0, j, 0)),
                 pl.BlockSpec((3, TM, LANES), lambda i, j: (0, j, 0))]
    args += [modsel, g1, w, wuq, wukv, gq, gkv, rope_b, rope_c]
    return pl.pallas_call(
        functools.partial(_inproj_kernel, with_moe=moe is not None),
        out_shape=out_shape,
        grid=(b, nb),
        in_specs=in_specs,
        out_specs=out_specs,
        compiler_params=_params(2),
        name="inproj",
    )(*args)


def _lane_lo():
    return lax.broadcasted_iota(jnp.int32, (1, LANES), 1) < (LANES // 2)


def _split_heads(qp, lo):
    zero = jnp.zeros_like(qp)
    return jnp.where(lo, qp, zero), jnp.where(lo, zero, qp)


def _softmax_pv(score_parts, value_parts, extra_logit=None):
    m = score_parts[0].max(axis=-1, keepdims=True)
    for s in score_parts[1:]:
        m = jnp.maximum(m, s.max(axis=-1, keepdims=True))
    if extra_logit is not None:
        m = jnp.maximum(m, extra_logit)
    den = None
    acc = None
    for s, v in zip(score_parts, value_parts):
        e = jnp.exp2(s - m)
        d = e.sum(axis=-1, keepdims=True)
        den = d if den is None else den + d
        o = _dot(e.astype(BF16), v)
        acc = o if acc is None else acc + o
    if extra_logit is not None:
        den = den + jnp.exp2(extra_logit - m)
    return acc / den


def _na_kernel(q_ref, k_ref, v_ref, bias_ref, o_ref, *, ctx, rows):
    j = pl.program_id(1)
    lo = _lane_lo()

    @pl.when(j == 0)
    def _():
        for jp in range(NA_HEADS // 2):
            sl = slice(jp * LANES, (jp + 1) * LANES)
            kc, vc = k_ref[0, 0:ctx, sl], v_ref[0, 0:ctx, sl]
            outs = [_softmax_pv([_dot_nt(qm, kc)], [vc]) for qm in _split_heads(q_ref[0, :, sl], lo)]
            o_ref[0, :, sl] = jnp.where(lo, outs[0], outs[1]).astype(BF16)

    @pl.when(j > 0)
    def _():
        r = (j - 1) * NA_QROWS
        s0 = jnp.clip(r - NA_KH // 2, 0, rows - NA_SLAB)
        start = pl.multiple_of(ctx + s0 * GRID_W, GRID_W)
        for jp in range(NA_HEADS // 2):
            sl = slice(jp * LANES, (jp + 1) * LANES)
            kc, vc = k_ref[0, 0:ctx, sl], v_ref[0, 0:ctx, sl]
            kn = k_ref[0, pl.ds(start, NA_SLAB * GRID_W), sl]
            vn = v_ref[0, pl.ds(start, NA_SLAB * GRID_W), sl]
            outs = []
            for half, qm in enumerate(_split_heads(q_ref[0, :, sl], lo)):
                s_n = _dot_nt(qm, kn) + bias_ref[0, 2 * jp + half]
                outs.append(_softmax_pv([_dot_nt(qm, kc), s_n], [vc, vn]))
            o_ref[0, :, sl] = jnp.where(lo, outs[0], outs[1]).astype(BF16)


def _na_attention(li, aq, ak, av, bias, ctx):
    b, l, w = aq.shape
    nb = l // TM
    rows = (l - ctx) // GRID_W
    last = nb - 1

    def bias_map(i, j):
        return (li, jnp.where(j <= 1, 0, jnp.where(j == last, 2, 1)), 0, 0, 0)

    return pl.pallas_call(
        functools.partial(_na_kernel, ctx=ctx, rows=rows),
        out_shape=jax.ShapeDtypeStruct((b, l, w), BF16),
        grid=(b, nb),
        in_specs=[pl.BlockSpec((1, TM, w), lambda i, j: (i, j, 0)),
                  pl.BlockSpec((1, l, w), lambda i, j: (i, 0, 0)),
                  pl.BlockSpec((1, l, w), lambda i, j: (i, 0, 0)),
                  pl.BlockSpec((None, 1, NA_HEADS, TM, NA_SLAB * GRID_W), bias_map)],
        out_specs=pl.BlockSpec((1, TM, w), lambda i, j: (i, j, 0)),
        compiler_params=_params(2),
        name="na_attention",
    )(aq, ak, av, bias)


def _na_bias_tables(na_rpb, rows):
    depth = na_rpb.shape[0]
    qc = np.arange(GRID_W)[:, None]
    kc = np.arange(GRID_W)[None, :]
    c0 = np.clip(qc - NA_KW // 2, 0, GRID_W - NA_KW)
    col_ok = (kc >= c0) & (kc < c0 + NA_KW)
    col_idx = np.clip(kc - qc + NA_KW - 1, 0, 2 * NA_KW - 2)
    bc = jnp.take(na_rpb, jnp.asarray(col_idx.reshape(-1)), axis=3).reshape(
        depth, NA_HEADS, 2 * NA_KH - 1, GRID_W, GRID_W)
    i_idx = np.zeros((3, NA_QROWS, NA_SLAB), np.int32)
    ok = np.zeros((3, NA_QROWS, NA_SLAB, GRID_W, GRID_W), bool)
    for case, (r, s) in enumerate(((0, 0), (NA_KH // 2, 0), (rows - NA_QROWS, rows - NA_SLAB))):
        for a in range(NA_QROWS):
            qr = r + a
            r0 = min(max(qr - NA_KH // 2, 0), rows - NA_KH)
            for c in range(NA_SLAB):
                kr = s + c
                inside = r0 <= kr < r0 + NA_KH
                i_idx[case, a, c] = min(max(kr - qr + NA_KH - 1, 0), 2 * NA_KH - 2)
                ok[case, a, c] = col_ok & inside
    t = jnp.take(bc, jnp.asarray(i_idx.reshape(-1)), axis=2).reshape(
        depth, NA_HEADS, 3, NA_QROWS, NA_SLAB, GRID_W, GRID_W)
    t = jnp.where(jnp.asarray(ok)[None, None], t * LOG2E, NEG_INF)
    t = t.transpose(0, 2, 1, 3, 5, 4, 6)
    return t.reshape(depth, 3, NA_HEADS, NA_QROWS * GRID_W, NA_SLAB * GRID_W)


def _mla_kernel(q_ref, k_ref, v_ref, o_ref, *, ctx):
    j = pl.program_id(1)
    lo = _lane_lo()

    def run(nkeys):
        for jp in range(MLA_HEADS // 2):
            vsl = slice(jp * LANES, (jp + 1) * LANES)
            vp = v_ref[0, 0:nkeys, vsl]
            outs = []
            for half in range(2):
                hsl = slice((2 * jp + half) * LANES, (2 * jp + half + 1) * LANES)
                s = _dot_nt(q_ref[0, :, hsl], k_ref[0, 0:nkeys, hsl])
                outs.append(_softmax_pv([s], [vp]))
            o_ref[0, :, vsl] = jnp.where(lo, outs[0], outs[1]).astype(BF16)

    @pl.when(j == 0)
    def _():
        run(ctx)

    @pl.when(j > 0)
    def _():
        run(k_ref.shape[1])


def _mla_attention(bq, bk, bv, ctx):
    b, l, wq = bq.shape
    wv = bv.shape[2]
    return pl.pallas_call(
        functools.partial(_mla_kernel, ctx=ctx),
        out_shape=jax.ShapeDtypeStruct((b, l, wv), BF16),
        grid=(b, l // TM),
        in_specs=[pl.BlockSpec((1, TM, wq), lambda i, j: (i, j, 0)),
                  pl.BlockSpec((1, l, wq), lambda i, j: (i, 0, 0)),
                  pl.BlockSpec((1, l, wv), lambda i, j: (i, 0, 0))],
        out_specs=pl.BlockSpec((1, TM, wv), lambda i, j: (i, j, 0)),
        compiler_params=_params(2),
        name="mla_attention",
    )(bq, bk, bv)


def _swa_kernel(sink_ref, q_ref, k_ref, v_ref, o_ref, *, ctx, n_lat):
    j = pl.program_id(1)
    lo = _lane_lo()
    group = SWA_HEADS // SWA_KV_HEADS
    band = TM + 2 * SWA_WINDOW

    def stacked_q(kv):
        parts = []
        for jp in range(kv * group // 2, (kv + 1) * group // 2):
            parts.extend(_split_heads(q_ref[0, :, jp * LANES:(jp + 1) * LANES], lo))
        return jnp.concatenate(parts, axis=0)

    def finish(kv, score_parts, value_parts):
        outs = []
        for g in range(group):
            rs = slice(g * TM, (g + 1) * TM)
            sink = jnp.full((1, 1), sink_ref[kv * group + g] * LOG2E, F32)
            outs.append(_softmax_pv([s[rs] for s in score_parts], value_parts, extra_logit=sink))
        for p in range(group // 2):
            jp = kv * group // 2 + p
            o_ref[0, :, jp * LANES:(jp + 1) * LANES] = jnp.where(lo, outs[2 * p], outs[2 * p + 1]).astype(BF16)

    @pl.when(j == 0)
    def _():
        for kv in range(SWA_KV_HEADS):
            sl = slice(kv * LANES, (kv + 1) * LANES)
            kc, vc = k_ref[0, 0:ctx, sl], v_ref[0, 0:ctx, sl]
            finish(kv, [_dot_nt(stacked_q(kv), kc)], [vc])

    @pl.when(j > 0)
    def _():
        q0 = (j - 1) * TM
        s0 = jnp.clip(q0 - SWA_WINDOW, 0, n_lat - band)
        start = pl.multiple_of(ctx + s0, SWA_WINDOW)
        qpos = q0 + (lax.broadcasted_iota(jnp.int32, (group * TM, band), 0) & (TM - 1))
        kpos = s0 + lax.broadcasted_iota(jnp.int32, (group * TM, band), 1)
        keep = jnp.abs(qpos - kpos) <= SWA_WINDOW
        for kv in range(SWA_KV_HEADS):
            sl = slice(kv * LANES, (kv + 1) * LANES)
            kc, vc = k_ref[0, 0:ctx, sl], v_ref[0, 0:ctx, sl]
            kb, vb = k_ref[0, pl.ds(start, band), sl], v_ref[0, pl.ds(start, band), sl]
            qs = stacked_q(kv)
            s_band = jnp.where(keep, _dot_nt(qs, kb), NEG_INF)
            finish(kv, [_dot_nt(qs, kc), s_band], [vc, vb])


def _swa_attention(sink, cq, ck, cv, ctx):
    b, l, w = cq.shape
    wk = ck.shape[2]
    grid_spec = pltpu.PrefetchScalarGridSpec(
        num_scalar_prefetch=1,
        grid=(b, l // TM),
        in_specs=[pl.BlockSpec((1, TM, w), lambda i, j, s: (i, j, 0)),
                  pl.BlockSpec((1, l, wk), lambda i, j, s: (i, 0, 0)),
                  pl.BlockSpec((1, l, wk), lambda i, j, s: (i, 0, 0))],
        out_specs=pl.BlockSpec((1, TM, w), lambda i, j, s: (i, j, 0)),
    )
    return pl.pallas_call(
        functools.partial(_swa_kernel, ctx=ctx, n_lat=l - ctx),
        out_shape=jax.ShapeDtypeStruct((b, l, w), BF16),
        grid_spec=grid_spec,
        compiler_params=_params(2),
        name="swa_attention",
    )(sink, cq, ck, cv)


def _merge_kernel(oa_ref, ob_ref, oc_ref, gate_ref, x_ref, mod_ref, wbr_ref, wout_ref, g2_ref,
                  rw_ref, rb_ref, xo_ref, h2_ref, route_ref, cnt_ref, run_ref, *, n_exp):
    d = x_ref.shape[2]

    @pl.when(jnp.logical_and(pl.program_id(0) == 0, pl.program_id(1) == 0))
    def _():
        run_ref[...] = jnp.zeros_like(run_ref)

    mod = mod_ref[0, 0]
    mix = None
    for i, o_ref in enumerate((oa_ref, ob_ref, oc_ref)):
        t = gate_ref[0, :, i * d:(i + 1) * d].astype(F32) * _dot(o_ref[0], wbr_ref[i])
        mix = t if mix is None else mix + t
    y = _dot(mix.astype(BF16), wout_ref[...])
    x = x_ref[0] + mod[2:3] * y
    xo_ref[0] = x
    h2 = (_rms(x) * g2_ref[...]) * (1.0 + mod[4:5]) + mod[3:4]
    h2b = h2.astype(BF16)
    h2_ref[0] = _pack_rows(h2b)
    logits = _dot(h2b, rw_ref[...]) + rb_ref[...]
    lane = lax.broadcasted_iota(jnp.int32, logits.shape, 1).astype(F32)
    work = jnp.where(lane < n_exp, logits, -jnp.inf)
    ids, vals = [], []
    for _ in range(TOP_K):
        m = work.max(axis=-1, keepdims=True)
        idx = jnp.where(work == m, lane, float(LANES)).min(axis=-1, keepdims=True)
        ids.append(idx)
        vals.append(m)
        work = jnp.where(lane == idx, -jnp.inf, work)
    ex = [jnp.exp(v - vals[0]) for v in vals]
    den = ex[0] + ex[1] + ex[2] + ex[3]
    hits = jnp.zeros(logits.shape, F32)
    for idx in ids:
        hits = hits + jnp.where(lane == idx, 1.0, 0.0)
    r = lax.broadcasted_iota(jnp.int32, (TM, TM), 0)
    c = lax.broadcasted_iota(jnp.int32, (TM, TM), 1)
    tri = jnp.where(c < r, 1.0, 0.0).astype(BF16)
    before = _dot(tri, hits.astype(BF16)) + run_ref[0:1]
    route = jnp.zeros(logits.shape, F32)
    for k in range(TOP_K):
        rank = jnp.where(lane == ids[k], before, 0.0).sum(axis=-1, keepdims=True)
        route = jnp.where(lane == k, ids[k], route)
        route = jnp.where(lane == TOP_K + k, ex[k] / den, route)
        route = jnp.where(lane == 2 * TOP_K + k, rank, route)
    route_ref[0] = route
    run_ref[...] = run_ref[...] + hits.sum(axis=0, keepdims=True)
    cnt_ref[...] = run_ref[...]


def _merge(li, oa, ob, oc, gate, x_all, modsel, wbr, wout, g2, rw, rb, n_exp):
    b, l, d = x_all.shape
    row = lambda width: pl.BlockSpec((1, TM, width), lambda i, j: (i, j, 0))
    full = lambda a: _layer_spec(a, li)
    return pl.pallas_call(
        functools.partial(_merge_kernel, n_exp=n_exp),
        out_shape=[jax.ShapeDtypeStruct((b, l, d), F32),
                   jax.ShapeDtypeStruct((b, l, d // 2), jnp.int32),
                   jax.ShapeDtypeStruct((b, l, LANES), F32),
                   jax.ShapeDtypeStruct((8, LANES), F32)],
        grid=(b, l // TM),
        in_specs=[row(BRANCH_W), row(BRANCH_W), row(BRANCH_W), row(N_BRANCH * d), row(d), _mod_spec(li)(d),
                  full(wbr), full(wout), full(g2), full(rw), full(rb)],
        out_specs=[row(d), row(d // 2), row(LANES), pl.BlockSpec((8, LANES), lambda i, j: (0, 0))],
        scratch_shapes=[pltpu.VMEM((8, LANES), F32)],
        compiler_params=_params(2),
        name="merge",
    )(oa, ob, oc, gate, x_all, modsel, wbr, wout, g2, rw, rb)


def _moe_kernel(be_ref, nx_ref, nu_ref, x_ref, wgu_hbm, bgu_ref, wdn_hbm, bdn_ref, y_ref,
                gu_stage, dn_stage, wgu_s, wdn_s, sem, *, li):
    i = pl.program_id(0)
    ff = wdn_s.shape[0]

    def fetch(e):
        return (pltpu.make_async_copy(wgu_hbm.at[li, e], gu_stage, sem.at[0]),
                pltpu.make_async_copy(wdn_hbm.at[li, e], dn_stage, sem.at[1]))

    @pl.when(i < nu_ref[0])
    def _():
        e = be_ref[i]

        @pl.when(i == 0)
        def _():
            for copy in fetch(e):
                copy.start()

        @pl.when(jnp.logical_or(i == 0, e != be_ref[jnp.maximum(i - 1, 0)]))
        def _():
            for copy in fetch(e):
                copy.wait()
            wgu_s[...] = gu_stage[...].astype(BF16)
            wdn_s[...] = dn_stage[...].astype(BF16)
            nxt = nx_ref[i]

            @pl.when(nxt >= 0)
            def _():
                for copy in fetch(nxt):
                    copy.start()

        gu = _dot(_unpack_rows(x_ref[...]), wgu_s[...]) + bgu_ref[0]
        glu = jnp.minimum(gu[:, :ff], SWIGLU_LIMIT)
        lin = jnp.clip(gu[:, ff:], -SWIGLU_LIMIT, SWIGLU_LIMIT)
        act = glu * _sigmoid(SWIGLU_ALPHA * glu) * (lin + 1.0)
        y_ref[...] = _pack_rows((_dot(act.astype(BF16), wdn_s[...]) + bdn_ref[0]).astype(BF16))


def _moe_experts(li, blk_e, nxt_e, n_used, xs, w_gu, b_gu, w_dn, b_dn):
    n_rows, packed_w = xs.shape
    depth, n_exp, d, ff2 = w_gu.shape
    ff = ff2 // 2
    n_blocks = n_rows // MOE_TM

    def row_map(i, be, nx, nu):
        return (jnp.minimum(i, nu[0] - 1), 0)

    def b_map(i, be, nx, nu):
        return (li, be[jnp.minimum(i, nu[0] - 1)], 0, 0)

    grid_spec = pltpu.PrefetchScalarGridSpec(
        num_scalar_prefetch=3,
        grid=(n_blocks,),
        in_specs=[pl.BlockSpec((MOE_TM, packed_w), row_map),
                  pl.BlockSpec(memory_space=pl.ANY),
                  pl.BlockSpec((None, 1, 1, ff2), b_map),
                  pl.BlockSpec(memory_space=pl.ANY),
                  pl.BlockSpec((None, 1, 1, d), b_map)],
        out_specs=pl.BlockSpec((MOE_TM, packed_w), row_map),
        scratch_shapes=[pltpu.VMEM((d, ff2), F32), pltpu.VMEM((ff, d), F32),
                        pltpu.VMEM((d, ff2), BF16), pltpu.VMEM((ff, d), BF16),
                        pltpu.SemaphoreType.DMA((2,))],
    )
    return pl.pallas_call(
        functools.partial(_moe_kernel, li=li),
        out_shape=jax.ShapeDtypeStruct((n_rows, packed_w), jnp.int32),
        grid_spec=grid_spec,
        compiler_params=_params(1),
        name="moe_experts",
    )(blk_e, nxt_e, n_used, xs, w_gu, b_gu.reshape(depth, n_exp, 1, ff2), w_dn, b_dn.reshape(depth, n_exp, 1, d))


def _layout(route, cnt, n_exp):
    b, l, _ = route.shape
    t = b * l
    ids = route[..., 0:TOP_K].astype(jnp.int32)
    rank = route[..., 2 * TOP_K:3 * TOP_K].astype(jnp.int32)
    counts = cnt[0, :n_exp].astype(jnp.int32)
    padded = (counts + MOE_TM - 1) // MOE_TM * MOE_TM
    pad_end = jnp.cumsum(padded)
    pad_start = pad_end - padded
    onehot = ids[..., None] == jnp.arange(n_exp, dtype=jnp.int32)
    dest = jnp.sum(jnp.where(onehot, pad_start, 0), axis=-1) + rank
    dest = dest.reshape(t, TOP_K).T
    n_blocks = -(-t * TOP_K // MOE_TM) + n_exp
    blk_start = jnp.arange(n_blocks, dtype=jnp.int32) * MOE_TM
    blk_e = jnp.minimum(jnp.sum(blk_start[:, None] >= pad_end[None, :], axis=1), n_exp - 1).astype(jnp.int32)
    n_used = (pad_end[-1] // MOE_TM).astype(jnp.int32).reshape(1)
    ar = jnp.arange(n_exp, dtype=jnp.int32)
    later = jnp.logical_and(ar[None, :] > ar[:, None], (counts > 0)[None, :])
    nxt_of = jnp.min(jnp.where(later, ar[None, :], n_exp), axis=1)
    nxt_of = jnp.where(nxt_of == n_exp, -1, nxt_of)
    nxt_e = jnp.sum(jnp.where(blk_e[:, None] == ar[None, :], nxt_of[None, :], 0), axis=1).astype(jnp.int32)
    return dest, blk_e, nxt_e, n_used, n_blocks * MOE_TM


def _sc_mesh():
    return plsc.VectorSubcoreMesh(core_axis_name="c", subcore_axis_name="s",
                                  num_cores=SC_CORES, num_subcores=SC_SUBCORES)


def _sc_worker():
    return lax.axis_index("s") * SC_CORES + lax.axis_index("c")


def _sc_scatter_rows(x, idx, n_rows):
    t, d = x.shape
    n_idx = idx.shape[0]
    workers = SC_CORES * SC_SUBCORES
    n_chunks = n_idx // (workers * SC_SCATTER_ROWS)
    assert n_chunks * workers * SC_SCATTER_ROWS == n_idx and t % SC_SCATTER_ROWS == 0
    idx3 = idx.reshape(workers, n_chunks, SC_SCATTER_ROWS)

    @pl.kernel(out_type=jax.ShapeDtypeStruct((n_rows, d), x.dtype), mesh=_sc_mesh(),
               scratch_types=[pltpu.VMEM((n_chunks, SC_SCATTER_ROWS), jnp.int32),
                              pltpu.VMEM((SC_SCATTER_ROWS, d), x.dtype),
                              pltpu.SemaphoreType.DMA])
    def scatter_kernel(x_hbm, i_hbm, o_hbm, idx_v, rows_v, sem):
        wid = _sc_worker()
        pltpu.sync_copy(i_hbm.at[wid], idx_v)

        @pl.loop(0, n_chunks)
        def _(j):
            src = pl.multiple_of(((wid * n_chunks + j) * SC_SCATTER_ROWS) % t, SC_SCATTER_ROWS)
            pltpu.sync_copy(x_hbm.at[pl.ds(src, SC_SCATTER_ROWS)], rows_v)
            pltpu.async_copy(rows_v, o_hbm.at[idx_v.at[j]], sem).wait()

    return scatter_kernel(x, idx3)


def _sc_gather_rows(x, idx):
    d = x.shape[1]
    n_idx = idx.shape[0]
    workers = SC_CORES * SC_SUBCORES
    per_worker = n_idx // workers
    n_chunks = per_worker // SC_GATHER_ROWS
    assert n_chunks * workers * SC_GATHER_ROWS == n_idx

    assert n_chunks % 2 == 0

    @pl.kernel(out_type=jax.ShapeDtypeStruct((n_idx, d), x.dtype), mesh=_sc_mesh(),
               scratch_types=[pltpu.VMEM((per_worker,), jnp.int32),
                              pltpu.VMEM((SC_GATHER_ROWS, d), x.dtype),
                              pltpu.VMEM((SC_GATHER_ROWS, d), x.dtype),
                              pltpu.SemaphoreType.DMA, pltpu.SemaphoreType.DMA])
    def gather_kernel(x_hbm, i_hbm, o_hbm, idx_v, buf0, buf1, sem0, sem1):
        base = _sc_worker() * per_worker
        pltpu.sync_copy(i_hbm.at[pl.ds(base, per_worker)], idx_v)
        bufs, sems = (buf0, buf1), (sem0, sem1)

        def gather(chunk, b):
            off = pl.multiple_of(chunk * SC_GATHER_ROWS, SC_GATHER_ROWS)
            return pltpu.make_async_copy(x_hbm.at[idx_v.at[pl.ds(off, SC_GATHER_ROWS)]], bufs[b], sems[b])

        gather(0, 0).start()

        @pl.loop(0, n_chunks, step=2)
        def _(j):
            for b in range(2):
                chunk = j + b
                gather(chunk, b).wait()
                if b == 0:
                    gather(chunk + 1, 1).start()
                else:
                    @pl.when(chunk + 1 < n_chunks)
                    def _():
                        gather(chunk + 1, 0).start()
                off = pl.multiple_of(chunk * SC_GATHER_ROWS, SC_GATHER_ROWS)
                pltpu.sync_copy(bufs[b], o_hbm.at[pl.ds(base + off, SC_GATHER_ROWS)])

    return gather_kernel(x, idx)


def _combine(y4_ref, route_ref):
    route = route_ref[0]
    y = None
    for k in range(TOP_K):
        t = route[:, TOP_K + k:TOP_K + k + 1] * _unpack_rows(y4_ref[k, 0]).astype(F32)
        y = t if y is None else y + t
    return y


def _final_kernel(x_ref, y4_ref, route_ref, mod_ref, g_ref, o_ref):
    x = x_ref[0] + mod_ref[0, 0][5:6] * _combine(y4_ref, route_ref)
    o_ref[0] = _rms(x) * g_ref[...]


def _final(li, x_all, y4, route, modsel, skip, final_g):
    b, l, d = x_all.shape
    off = skip // TM
    lo = l - skip
    return pl.pallas_call(
        _final_kernel, out_shape=jax.ShapeDtypeStruct((b, lo, d), F32), grid=(b, lo // TM),
        in_specs=[pl.BlockSpec((1, TM, d), lambda i, j: (i, j + off, 0)),
                  pl.BlockSpec((TOP_K, 1, TM, y4.shape[3]), lambda i, j: (0, i, j + off, 0)),
                  pl.BlockSpec((1, TM, LANES), lambda i, j: (i, j + off, 0)),
                  _mod_spec(li, off)(d),
                  pl.BlockSpec((1, d), lambda i, j: (0, 0))],
        out_specs=pl.BlockSpec((1, TM, d), lambda i, j: (i, j, 0)),
        compiler_params=_params(2), name="final_norm",
    )(x_all, y4, route, modsel, final_g)


def _rope_tables(n, ctx, rot_dim, group_pattern):
    t = jnp.arange(n, dtype=jnp.int32)
    row = (t // GRID_W).astype(F32)
    col = (t % GRID_W).astype(F32)
    per_axis = rot_dim // 2
    inv = ROPE_BASE ** (-jnp.arange(0, per_axis, 2, dtype=F32) / per_axis)
    ang = jnp.concatenate([row[:, None] * inv[None], col[:, None] * inv[None]], axis=-1)
    cos, sin = jnp.cos(ang), jnp.sin(ang)
    half = rot_dim // 2
    c = jnp.ones((n, LANES), F32)
    sdn = jnp.zeros((n, LANES), F32)
    sup = jnp.zeros((n, LANES), F32)
    for off in group_pattern:
        c = c.at[:, off:off + half].set(cos).at[:, off + half:off + rot_dim].set(cos)
        sdn = sdn.at[:, off:off + half].set(-sin)
        sup = sup.at[:, off + half:off + rot_dim].set(sin)
    tab = jnp.stack([c, sdn, sup])
    ident = jnp.stack([jnp.ones((ctx, LANES), F32), jnp.zeros((ctx, LANES), F32), jnp.zeros((ctx, LANES), F32)])
    return jnp.concatenate([ident, tab], axis=1)


def _wprep_kernel(w_ref, o_ref):
    x = w_ref[0]
    rows = x.shape[0]
    o = 3 * 512 + MLA_Q_LORA + MLA_KV_LORA
    zeros = lambda n: jnp.zeros((rows, n), F32)
    kr = jnp.concatenate([zeros(MLA_NOPE), x[:, o:o + MLA_ROPE], zeros(LANES - MLA_NOPE - MLA_ROPE)], axis=1)
    o += MLA_ROPE
    cq = x[:, o:o + 512]
    k0, k1 = x[:, o + 512:o + 576], x[:, o + 576:o + 640]
    v0, v1 = x[:, o + 640:o + 704], x[:, o + 704:o + 768]
    g = x[:, o + 768:]
    o_ref[0] = jnp.concatenate([x[:, :3 * 512 + MLA_Q_LORA + MLA_KV_LORA], kr, cq, k0, k0, k1, k1, v0, v0, v1, v1, g],
                               axis=1).astype(BF16)


def _wprep(w_in):
    depth, d, cols = w_in.shape
    out_cols = cols - MLA_ROPE + LANES + 2 * 128
    tr = 128
    return pl.pallas_call(
        _wprep_kernel,
        out_shape=jax.ShapeDtypeStruct((depth, d, out_cols), BF16),
        grid=(depth, d // tr),
        in_specs=[pl.BlockSpec((1, tr, cols), lambda l, i: (l, i, 0))],
        out_specs=pl.BlockSpec((1, tr, out_cols), lambda l, i: (l, i, 0)),
        compiler_params=_params(2),
        name="wprep",
    )(w_in)


def _prep_weights(w_in, mla_w_uq, mla_w_ukv):
    depth = w_in.shape[0]
    w = _wprep(w_in)
    uq = mla_w_uq.reshape(depth, MLA_Q_LORA, MLA_HEADS, MLA_NOPE + MLA_ROPE)
    uq = jnp.pad(uq, ((0, 0), (0, 0), (0, 0), (0, LANES - MLA_NOPE - MLA_ROPE)))
    uq = uq.reshape(depth, MLA_Q_LORA, MLA_HEADS * LANES).astype(BF16)
    ukv = mla_w_ukv.reshape(depth, MLA_KV_LORA, MLA_HEADS, MLA_NOPE + MLA_V)
    uk = jnp.pad(ukv[..., :MLA_NOPE], ((0, 0), (0, 0), (0, 0), (0, LANES - MLA_NOPE)))
    uk = uk.reshape(depth, MLA_KV_LORA, MLA_HEADS * LANES)
    uv = ukv[..., MLA_NOPE:].reshape(depth, MLA_KV_LORA, MLA_HEADS * MLA_V)
    ukv = jnp.concatenate([uk, uv], axis=-1).astype(BF16)
    return w, uq, ukv


def kernel(x, c, ctx, c_ctx, norm1_g, norm2_g, w_mod, b_mod, w_in, na_rpb, mla_q_norm_g, mla_kv_norm_g, mla_w_uq, mla_w_ukv, swa_sink, w_branch, w_out, router_w, router_b, expert_w_gate_up, expert_b_gate_up, expert_w_down, expert_b_down, final_norm_g):
    b, n, d = x.shape
    lc = ctx.shape[1]
    l = lc + n
    depth = w_in.shape[0]
    n_exp = router_w.shape[2]
    assert lc == TM and n % TM == 0 and TM == NA_QROWS * GRID_W

    cvec = jnp.zeros((8, d), F32).at[:b].set(c).at[b].set(c_ctx)
    mod = _modulation(cvec, w_mod, b_mod)
    mod_lat = mod[:, :b].reshape(depth, b, 1, 6, d)
    mod_ctx = jnp.broadcast_to(mod[:, b].reshape(depth, 1, 1, 6, d), (depth, b, 1, 6, d))
    modsel = jnp.concatenate([mod_ctx, mod_lat], axis=2)

    w_all, uq_all, ukv_all = _prep_weights(w_in, mla_w_uq, mla_w_ukv)
    wbr_all = w_branch.astype(BF16)
    wout_all = w_out.astype(BF16)
    rw_all = jnp.pad(router_w, ((0, 0), (0, 0), (0, LANES - n_exp))).astype(BF16)
    rb_all = jnp.pad(router_b, ((0, 0), (0, LANES - n_exp))).reshape(depth, 1, LANES)
    rope_b = _rope_tables(n, lc, MLA_ROPE, (MLA_NOPE,))
    rope_c = _rope_tables(n, lc, SWA_HEAD_DIM, (0, SWA_HEAD_DIM))
    bias_all = _na_bias_tables(na_rpb, n // GRID_W)

    g1_all = norm1_g.reshape(depth, 1, d)
    g2_all = norm2_g.reshape(depth, 1, d)
    gq_all = mla_q_norm_g.reshape(depth, 1, -1)
    gkv_all = mla_kv_norm_g.reshape(depth, 1, -1)

    x_all = jnp.concatenate([ctx, x], axis=1)
    moe = None
    for li in range(depth):
        ms = modsel
        outs = _inproj(li, x_all, moe, ms, g1_all, w_all, uq_all, ukv_all, gq_all, gkv_all, rope_b, rope_c)
        if moe is not None:
            x_all, outs = outs[0], outs[1:]
        aq, ak, av, bq, bk, bv, cq, ck, cv, gate = outs
        oa = _na_attention(li, aq, ak, av, bias_all, lc)
        ob = _mla_attention(bq, bk, bv, lc)
        oc = _swa_attention(swa_sink[li], cq, ck, cv, lc)
        x_all, h2, route, cnt = _merge(li, oa, ob, oc, gate, x_all, ms, wbr_all, wout_all,
                                       g2_all, rw_all, rb_all, n_exp)
        dest, blk_e, nxt_e, n_used, n_rows = _layout(route, cnt, n_exp)
        dest = dest.reshape(TOP_K * b * l)
        xs = _sc_scatter_rows(h2.reshape(b * l, d // 2), dest, n_rows)
        ys = _moe_experts(li, blk_e, nxt_e, n_used, xs, expert_w_gate_up, expert_b_gate_up,
                          expert_w_down, expert_b_down)
        moe = (_sc_gather_rows(ys, dest).reshape(TOP_K, b, l, d // 2), route)
    return _final(depth - 1, x_all, moe[0], moe[1], modsel, lc, final_norm_g.reshape(1, d))
```

```python
import functools

import numpy as np
import jax
import jax.numpy as jnp
from jax import lax
from jax.experimental import pallas as pl
from jax.experimental.pallas import tpu as pltpu
from jax.experimental.pallas import tpu_sc as plsc

GRID_W = 64
EPS = 1e-6
ROPE_BASE = 10000.0
NEG_INF = -1e30
LOG2E = 1.4426950408889634
LANES = 128

NA_HEADS = 8
NA_HEAD_DIM = 64
NA_KH = 8
NA_KW = 16
NA_QROWS = 4
NA_SLAB = 12
MLA_HEADS = 8
MLA_NOPE = 64
MLA_ROPE = 32
MLA_V = 64
MLA_Q_LORA = 256
MLA_KV_LORA = 128
SWA_HEADS = 8
SWA_KV_HEADS = 2
SWA_HEAD_DIM = 64
SWA_WINDOW = 128
N_BRANCH = 3
BRANCH_W = 512
N_EXPERTS = 32
TOP_K = 4
SWIGLU_ALPHA = 1.702
SWIGLU_LIMIT = 7.0

TM = 256
MOE_TM = 512
SC_CORES = 2
SC_SUBCORES = 16
SC_SCATTER_ROWS = 128
SC_GATHER_ROWS = 64
VMEM_LIMIT = 56 * 1024 * 1024

BF16 = jnp.bfloat16
F32 = jnp.float32


def _dot(a, b):
    return jnp.dot(a, b, preferred_element_type=F32)


def _dot_nt(a, b):
    return lax.dot_general(a, b, (((1,), (1,)), ((), ())), preferred_element_type=F32)


def _params(n_axes, vmem=VMEM_LIMIT):
    return pltpu.CompilerParams(dimension_semantics=("arbitrary",) * n_axes, vmem_limit_bytes=vmem)


def _layer_spec(a, li):
    return pl.BlockSpec((None,) + a.shape[1:], lambda *_: (li,) + (0,) * (a.ndim - 1))


def _mod_spec(li, off=0):
    return lambda d: pl.BlockSpec((None, 1, 1, 6, d), lambda i, j: (li, i, jnp.minimum(j + off, 1), 0, 0))


def _rms(x):
    return x * lax.rsqrt(jnp.mean(x * x, axis=-1, keepdims=True) + EPS)


def _sigmoid(x):
    return 1.0 / (1.0 + jnp.exp(-x))


def _pack_rows(xb):
    half = xb.shape[1] // 2
    lo = pltpu.bitcast(xb[:, :half].astype(F32), jnp.int32)
    hi = pltpu.bitcast(xb[:, half:].astype(F32), jnp.int32)
    return (hi & jnp.int32(-65536)) | lax.shift_right_logical(lo, 16)


def _unpack_rows(w):
    lo = pltpu.bitcast(lax.shift_left(w, 16), F32).astype(BF16)
    hi = pltpu.bitcast(w & jnp.int32(-65536), F32).astype(BF16)
    return jnp.concatenate([lo, hi], axis=1)


def _mod_kernel(c_ref, w_ref, b_ref, o_ref):
    c = c_ref[...]
    s = (c * _sigmoid(c)).astype(BF16)
    o_ref[0] = _dot(s, w_ref[0].astype(BF16)) + b_ref[0]


def _modulation(cvec, w_mod, b_mod):
    depth, d, n6 = w_mod.shape
    tn = n6 // 4
    return pl.pallas_call(
        _mod_kernel,
        out_shape=jax.ShapeDtypeStruct((depth, 8, n6), F32),
        grid=(depth, n6 // tn),
        in_specs=[pl.BlockSpec((8, d), lambda l, j: (0, 0)),
                  pl.BlockSpec((1, d, tn), lambda l, j: (l, 0, j)),
                  pl.BlockSpec((1, 1, tn), lambda l, j: (l, 0, j))],
        out_specs=pl.BlockSpec((1, 8, tn), lambda l, j: (l, 0, j)),
        compiler_params=_params(2),
        name="modulation",
    )(cvec, w_mod, b_mod.reshape(depth, 1, n6))


def _rope_groups(x, tab_ref, shift):
    cos, sdn, sup = tab_ref[0], tab_ref[1], tab_ref[2]
    outs = []
    for g in range(x.shape[1] // LANES):
        xg = x[:, g * LANES:(g + 1) * LANES]
        outs.append(xg * cos + pltpu.roll(xg, LANES - shift, 1) * sdn + pltpu.roll(xg, shift, 1) * sup)
    return outs[0] if len(outs) == 1 else jnp.concatenate(outs, axis=1)


def _inproj_kernel(*refs, with_moe):
    if with_moe:
        x_ref, y4_ref, route_ref, modp_ref = refs[:4]
        refs = refs[4:]
        (mod_ref, g1_ref, w_ref, wuq_ref, wukv_ref, gq_ref, gkv_ref, rb_ref, rc_ref, xo_ref,
         aq_ref, ak_ref, av_ref, bq_ref, bk_ref, bv_ref, cq_ref, ck_ref, cv_ref, gate_ref) = refs
        x = x_ref[0] + modp_ref[0, 0][5:6] * _combine(y4_ref, route_ref)
        xo_ref[0] = x
    else:
        (x_ref, mod_ref, g1_ref, w_ref, wuq_ref, wukv_ref, gq_ref, gkv_ref, rb_ref, rc_ref,
         aq_ref, ak_ref, av_ref, bq_ref, bk_ref, bv_ref, cq_ref, ck_ref, cv_ref, gate_ref) = refs
        x = x_ref[0]
    mod = mod_ref[0, 0]
    h = (_rms(x) * g1_ref[...]) * (1.0 + mod[1:2]) + mod[0:1]
    hb = h.astype(BF16)
    acc = _dot(hb, w_ref[:, 0:1536])
    aq_ref[0] = (acc[:, 0:512] * (NA_HEAD_DIM ** -0.5 * LOG2E)).astype(BF16)
    ak_ref[0] = acc[:, 512:1024].astype(BF16)
    av_ref[0] = acc[:, 1024:1536].astype(BF16)
    acc = _dot(hb, w_ref[:, 1536:2048])
    qn = (_rms(acc[:, 0:256]) * gq_ref[...]).astype(BF16)
    kvn = (_rms(acc[:, 256:384]) * gkv_ref[...]).astype(BF16)
    kr = _rope_groups(acc[:, 384:512], rb_ref, MLA_ROPE // 2)
    q = _rope_groups(_dot(qn, wuq_ref[...]), rb_ref, MLA_ROPE // 2)
    bq_ref[0] = (q * ((MLA_NOPE + MLA_ROPE) ** -0.5 * LOG2E)).astype(BF16)
    kv = _dot(kvn, wukv_ref[...])
    bk_ref[0] = (kv[:, 0:1024] + jnp.concatenate([kr] * MLA_HEADS, axis=1)).astype(BF16)
    bv_ref[0] = kv[:, 1024:1536].astype(BF16)
    acc = _dot(hb, w_ref[:, 2048:3072])
    cq_ref[0] = (_rope_groups(acc[:, 0:512], rc_ref, SWA_HEAD_DIM // 2) * (SWA_HEAD_DIM ** -0.5 * LOG2E)).astype(BF16)
    ck_ref[0] = _rope_groups(acc[:, 512:768], rc_ref, SWA_HEAD_DIM // 2).astype(BF16)
    cv_ref[0] = acc[:, 768:1024].astype(BF16)
    gate_ref[0] = _sigmoid(_dot(hb, w_ref[:, 3072:])).astype(BF16)


def _inproj(li, x_all, moe, modsel, g1, w, wuq, wukv, gq, gkv, rope_b, rope_c):
    b, l, d = x_all.shape
    nb = l // TM
    row = lambda width: pl.BlockSpec((1, TM, width), lambda i, j: (i, j, 0))
    full = lambda a: _layer_spec(a, li)
    widths = (512, 512, 512, 1024, 1024, 512, 512, 256, 256, N_BRANCH * d)
    in_specs = [row(d)]
    out_specs = [row(wd) for wd in widths]
    out_shape = [jax.ShapeDtypeStruct((b, l, wd), BF16) for wd in widths]
    args = [x_all]
    if moe is not None:
        y4, route = moe
        in_specs += [pl.BlockSpec((TOP_K, 1, TM, y4.shape[3]), lambda i, j: (0, i, j, 0)), row(LANES),
                     _mod_spec(li - 1)(d)]
        out_specs = [row(d)] + out_specs
        out_shape = [jax.ShapeDtypeStruct((b, l, d), F32)] + out_shape
        args += [y4, route, modsel]
    in_specs += [_mod_spec(li)(d), full(g1), full(w), full(wuq), full(wukv), full(gq), full(gkv),
                 pl.BlockSpec((3, TM, LANES), lambda i, j: (0, j, 0)),
                 pl.BlockSpec((3, TM, LANES), lambda i, j: (0, j, 0))]
    args += [modsel, g1, w, wuq, wukv, gq, gkv, rope_b, rope_c]
    return pl.pallas_call(
        functools.partial(_inproj_kernel, with_moe=moe is not None),
        out_shape=out_shape,
        grid=(b, nb),
        in_specs=in_specs,
        out_specs=out_specs,
        compiler_params=_params(2),
        name="inproj",
    )(*args)


def _lane_lo():
    return lax.broadcasted_iota(jnp.int32, (1, LANES), 1) < (LANES // 2)


def _split_heads(qp, lo):
    zero = jnp.zeros_like(qp)
    return jnp.where(lo, qp, zero), jnp.where(lo, zero, qp)


def _softmax_pv(score_parts, value_parts, extra_logit=None):
    m = score_parts[0].max(axis=-1, keepdims=True)
    for s in score_parts[1:]:
        m = jnp.maximum(m, s.max(axis=-1, keepdims=True))
    if extra_logit is not None:
        m = jnp.maximum(m, extra_logit)
    den = None
    acc = None
    for s, v in zip(score_parts, value_parts):
        e = jnp.exp2(s - m)
        d = e.sum(axis=-1, keepdims=True)
        den = d if den is None else den + d
        o = _dot(e.astype(BF16), v)
        acc = o if acc is None else acc + o
    if extra_logit is not None:
        den = den + jnp.exp2(extra_logit - m)
    return acc / den


def _na_kernel(q_ref, k_ref, v_ref, bias_ref, o_ref, *, ctx, rows):
    j = pl.program_id(1)
    lo = _lane_lo()

    @pl.when(j == 0)
    def _():
        for jp in range(NA_HEADS // 2):
            sl = slice(jp * LANES, (jp + 1) * LANES)
            kc, vc = k_ref[0, 0:ctx, sl], v_ref[0, 0:ctx, sl]
            outs = [_softmax_pv([_dot_nt(qm, kc)], [vc]) for qm in _split_heads(q_ref[0, :, sl], lo)]
            o_ref[0, :, sl] = jnp.where(lo, outs[0], outs[1]).astype(BF16)

    @pl.when(j > 0)
    def _():
        r = (j - 1) * NA_QROWS
        s0 = jnp.clip(r - NA_KH // 2, 0, rows - NA_SLAB)
        start = pl.multiple_of(ctx + s0 * GRID_W, GRID_W)
        slab = pl.ds(start, NA_SLAB * GRID_W)

        def scores(h):
            sl = slice((h // 2) * LANES, (h // 2 + 1) * LANES)
            qm = _split_heads(q_ref[0, :, sl], lo)[h % 2]
            return [_dot_nt(qm, k_ref[0, 0:ctx, sl]), _dot_nt(qm, k_ref[0, slab, sl]) + bias_ref[0, h]]

        ahead = 1
        pending = [scores(h) for h in range(ahead)]
        outs = []
        for h in range(NA_HEADS):
            if h + ahead < NA_HEADS:
                pending.append(scores(h + ahead))
            sl = slice((h // 2) * LANES, (h // 2 + 1) * LANES)
            outs.append(_softmax_pv(pending.pop(0), [v_ref[0, 0:ctx, sl], v_ref[0, slab, sl]]))
            if h % 2 == 1:
                o_ref[0, :, sl] = jnp.where(lo, outs[h - 1], outs[h]).astype(BF16)


def _na_attention(li, aq, ak, av, bias, ctx):
    b, l, w = aq.shape
    nb = l // TM
    rows = (l - ctx) // GRID_W
    last = nb - 1

    def bias_map(i, j):
        return (li, jnp.where(j <= 1, 0, jnp.where(j == last, 2, 1)), 0, 0, 0)

    return pl.pallas_call(
        functools.partial(_na_kernel, ctx=ctx, rows=rows),
        out_shape=jax.ShapeDtypeStruct((b, l, w), BF16),
        grid=(b, nb),
        in_specs=[pl.BlockSpec((1, TM, w), lambda i, j: (i, j, 0)),
                  pl.BlockSpec((1, l, w), lambda i, j: (i, 0, 0)),
                  pl.BlockSpec((1, l, w), lambda i, j: (i, 0, 0)),
                  pl.BlockSpec((None, 1, NA_HEADS, TM, NA_SLAB * GRID_W), bias_map)],
        out_specs=pl.BlockSpec((1, TM, w), lambda i, j: (i, j, 0)),
        compiler_params=_params(2),
        name="na_attention",
    )(aq, ak, av, bias)


def _na_bias_tables(na_rpb, rows):
    depth = na_rpb.shape[0]
    qc = np.arange(GRID_W)[:, None]
    kc = np.arange(GRID_W)[None, :]
    c0 = np.clip(qc - NA_KW // 2, 0, GRID_W - NA_KW)
    col_ok = (kc >= c0) & (kc < c0 + NA_KW)
    col_idx = np.clip(kc - qc + NA_KW - 1, 0, 2 * NA_KW - 2)
    bc = jnp.take(na_rpb, jnp.asarray(col_idx.reshape(-1)), axis=3).reshape(
        depth, NA_HEADS, 2 * NA_KH - 1, GRID_W, GRID_W)
    i_idx = np.zeros((3, NA_QROWS, NA_SLAB), np.int32)
    ok = np.zeros((3, NA_QROWS, NA_SLAB, GRID_W, GRID_W), bool)
    for case, (r, s) in enumerate(((0, 0), (NA_KH // 2, 0), (rows - NA_QROWS, rows - NA_SLAB))):
        for a in range(NA_QROWS):
            qr = r + a
            r0 = min(max(qr - NA_KH // 2, 0), rows - NA_KH)
            for c in range(NA_SLAB):
                kr = s + c
                inside = r0 <= kr < r0 + NA_KH
                i_idx[case, a, c] = min(max(kr - qr + NA_KH - 1, 0), 2 * NA_KH - 2)
                ok[case, a, c] = col_ok & inside
    t = jnp.take(bc, jnp.asarray(i_idx.reshape(-1)), axis=2).reshape(
        depth, NA_HEADS, 3, NA_QROWS, NA_SLAB, GRID_W, GRID_W)
    t = jnp.where(jnp.asarray(ok)[None, None], t * LOG2E, NEG_INF)
    t = t.transpose(0, 2, 1, 3, 5, 4, 6)
    return t.reshape(depth, 3, NA_HEADS, NA_QROWS * GRID_W, NA_SLAB * GRID_W)


def _mla_kernel(q_ref, k_ref, v_ref, o_ref, *, ctx):
    j = pl.program_id(1)
    lo = _lane_lo()

    def run(nkeys):
        def scores(h):
            hsl = slice(h * LANES, (h + 1) * LANES)
            return _dot_nt(q_ref[0, :, hsl], k_ref[0, 0:nkeys, hsl])

        ahead = 1
        pending = [scores(h) for h in range(ahead)]
        outs = []
        for h in range(MLA_HEADS):
            if h + ahead < MLA_HEADS:
                pending.append(scores(h + ahead))
            vsl = slice((h // 2) * LANES, (h // 2 + 1) * LANES)
            outs.append(_softmax_pv([pending.pop(0)], [v_ref[0, 0:nkeys, vsl]]))
            if h % 2 == 1:
                o_ref[0, :, vsl] = jnp.where(lo, outs[h - 1], outs[h]).astype(BF16)

    @pl.when(j == 0)
    def _():
        run(ctx)

    @pl.when(j > 0)
    def _():
        run(k_ref.shape[1])


def _mla_attention(bq, bk, bv, ctx):
    b, l, wq = bq.shape
    wv = bv.shape[2]
    return pl.pallas_call(
        functools.partial(_mla_kernel, ctx=ctx),
        out_shape=jax.ShapeDtypeStruct((b, l, wv), BF16),
        grid=(b, l // TM),
        in_specs=[pl.BlockSpec((1, TM, wq), lambda i, j: (i, j, 0)),
                  pl.BlockSpec((1, l, wq), lambda i, j: (i, 0, 0)),
                  pl.BlockSpec((1, l, wv), lambda i, j: (i, 0, 0))],
        out_specs=pl.BlockSpec((1, TM, wv), lambda i, j: (i, j, 0)),
        compiler_params=_params(2),
        name="mla_attention",
    )(bq, bk, bv)


def _swa_kernel(sink_ref, q_ref, k_ref, v_ref, o_ref, *, ctx, n_lat):
    j = pl.program_id(1)
    lo = _lane_lo()
    group = SWA_HEADS // SWA_KV_HEADS
    band = TM + 2 * SWA_WINDOW

    def stacked_q(kv):
        parts = []
        for jp in range(kv * group // 2, (kv + 1) * group // 2):
            parts.extend(_split_heads(q_ref[0, :, jp * LANES:(jp + 1) * LANES], lo))
        return jnp.concatenate(parts, axis=0)

    def finish(kv, score_parts, value_parts, keep=None):
        outs = []
        for g in range(group):
            rs = slice(g * TM, (g + 1) * TM)
            sink = jnp.full((1, 1), sink_ref[kv * group + g] * LOG2E, F32)
            parts = [s[rs] for s in score_parts]
            if keep is not None:
                parts[-1] = jnp.where(keep, parts[-1], NEG_INF)
            outs.append(_softmax_pv(parts, value_parts, extra_logit=sink))
        for p in range(group // 2):
            jp = kv * group // 2 + p
            o_ref[0, :, jp * LANES:(jp + 1) * LANES] = jnp.where(lo, outs[2 * p], outs[2 * p + 1]).astype(BF16)

    @pl.when(j == 0)
    def _():
        for kv in range(SWA_KV_HEADS):
            sl = slice(kv * LANES, (kv + 1) * LANES)
            kc, vc = k_ref[0, 0:ctx, sl], v_ref[0, 0:ctx, sl]
            finish(kv, [_dot_nt(stacked_q(kv), kc)], [vc])

    @pl.when(j > 0)
    def _():
        q0 = (j - 1) * TM
        s0 = jnp.clip(q0 - SWA_WINDOW, 0, n_lat - band)
        rows = pl.ds(pl.multiple_of(ctx + s0, SWA_WINDOW), band)
        qpos = q0 + lax.broadcasted_iota(jnp.int32, (TM, band), 0)
        kpos = s0 + lax.broadcasted_iota(jnp.int32, (TM, band), 1)
        keep = jnp.abs(qpos - kpos) <= SWA_WINDOW

        def scores(kv):
            sl = slice(kv * LANES, (kv + 1) * LANES)
            qs = stacked_q(kv)
            return [_dot_nt(qs, k_ref[0, 0:ctx, sl]), _dot_nt(qs, k_ref[0, rows, sl])]

        s_all = [scores(kv) for kv in range(SWA_KV_HEADS)]
        for kv in range(SWA_KV_HEADS):
            sl = slice(kv * LANES, (kv + 1) * LANES)
            finish(kv, s_all[kv], [v_ref[0, 0:ctx, sl], v_ref[0, rows, sl]], keep)


def _swa_attention(sink, cq, ck, cv, ctx):
    b, l, w = cq.shape
    wk = ck.shape[2]
    grid_spec = pltpu.PrefetchScalarGridSpec(
        num_scalar_prefetch=1,
        grid=(b, l // TM),
        in_specs=[pl.BlockSpec((1, TM, w), lambda i, j, s: (i, j, 0)),
                  pl.BlockSpec((1, l, wk), lambda i, j, s: (i, 0, 0)),
                  pl.BlockSpec((1, l, wk), lambda i, j, s: (i, 0, 0))],
        out_specs=pl.BlockSpec((1, TM, w), lambda i, j, s: (i, j, 0)),
    )
    return pl.pallas_call(
        functools.partial(_swa_kernel, ctx=ctx, n_lat=l - ctx),
        out_shape=jax.ShapeDtypeStruct((b, l, w), BF16),
        grid_spec=grid_spec,
        compiler_params=_params(2),
        name="swa_attention",
    )(sink, cq, ck, cv)


def _merge_kernel(oa_ref, ob_ref, oc_ref, gate_ref, x_ref, mod_ref, wbr_ref, wout_ref, g2_ref,
                  rw_ref, rb_ref, xo_ref, h2_ref, route_ref, cnt_ref, run_ref, *, n_exp):
    d = x_ref.shape[2]

    @pl.when(jnp.logical_and(pl.program_id(0) == 0, pl.program_id(1) == 0))
    def _():
        run_ref[...] = jnp.zeros_like(run_ref)

    mod = mod_ref[0, 0]
    mix = None
    for i, o_ref in enumerate((oa_ref, ob_ref, oc_ref)):
        t = gate_ref[0, :, i * d:(i + 1) * d].astype(F32) * _dot(o_ref[0], wbr_ref[i])
        mix = t if mix is None else mix + t
    y = _dot(mix.astype(BF16), wout_ref[...])
    x = x_ref[0] + mod[2:3] * y
    xo_ref[0] = x
    h2 = (_rms(x) * g2_ref[...]) * (1.0 + mod[4:5]) + mod[3:4]
    h2b = h2.astype(BF16)
    h2_ref[0] = _pack_rows(h2b)
    logits = _dot(h2b, rw_ref[...]) + rb_ref[...]
    lane = lax.broadcasted_iota(jnp.int32, logits.shape, 1).astype(F32)
    work = jnp.where(lane < n_exp, logits, -jnp.inf)
    ids, vals = [], []
    for _ in range(TOP_K):
        m = work.max(axis=-1, keepdims=True)
        idx = jnp.where(work == m, lane, float(LANES)).min(axis=-1, keepdims=True)
        ids.append(idx)
        vals.append(m)
        work = jnp.where(lane == idx, -jnp.inf, work)
    ex = [jnp.exp(v - vals[0]) for v in vals]
    den = ex[0] + ex[1] + ex[2] + ex[3]
    hits = jnp.zeros(logits.shape, F32)
    for idx in ids:
        hits = hits + jnp.where(lane == idx, 1.0, 0.0)
    r = lax.broadcasted_iota(jnp.int32, (TM, TM), 0)
    c = lax.broadcasted_iota(jnp.int32, (TM, TM), 1)
    tri = jnp.where(c < r, 1.0, 0.0).astype(BF16)
    before = _dot(tri, hits.astype(BF16)) + run_ref[0:1]
    route = jnp.zeros(logits.shape, F32)
    for k in range(TOP_K):
        rank = jnp.where(lane == ids[k], before, 0.0).sum(axis=-1, keepdims=True)
        route = jnp.where(lane == k, ids[k], route)
        route = jnp.where(lane == TOP_K + k, ex[k] / den, route)
        route = jnp.where(lane == 2 * TOP_K + k, rank, route)
    route_ref[0] = route
    run_ref[...] = run_ref[...] + hits.sum(axis=0, keepdims=True)
    cnt_ref[...] = run_ref[...]


def _merge(li, oa, ob, oc, gate, x_all, modsel, wbr, wout, g2, rw, rb, n_exp):
    b, l, d = x_all.shape
    row = lambda width: pl.BlockSpec((1, TM, width), lambda i, j: (i, j, 0))
    full = lambda a: _layer_spec(a, li)
    return pl.pallas_call(
        functools.partial(_merge_kernel, n_exp=n_exp),
        out_shape=[jax.ShapeDtypeStruct((b, l, d), F32),
                   jax.ShapeDtypeStruct((b, l, d // 2), jnp.int32),
                   jax.ShapeDtypeStruct((b, l, LANES), F32),
                   jax.ShapeDtypeStruct((8, LANES), F32)],
        grid=(b, l // TM),
        in_specs=[row(BRANCH_W), row(BRANCH_W), row(BRANCH_W), row(N_BRANCH * d), row(d), _mod_spec(li)(d),
                  full(wbr), full(wout), full(g2), full(rw), full(rb)],
        out_specs=[row(d), row(d // 2), row(LANES), pl.BlockSpec((8, LANES), lambda i, j: (0, 0))],
        scratch_shapes=[pltpu.VMEM((8, LANES), F32)],
        compiler_params=_params(2),
        name="merge",
    )(oa, ob, oc, gate, x_all, modsel, wbr, wout, g2, rw, rb)


def _moe_kernel(be_ref, nx_ref, nu_ref, x_ref, wgu_hbm, bgu_ref, wdn_hbm, bdn_ref, y_ref,
                gu_stage, dn_stage, wgu_s, wdn_s, sem, *, li):
    i = pl.program_id(0)
    ff = wdn_s.shape[0]

    def fetch(e):
        return (pltpu.make_async_copy(wgu_hbm.at[li, e], gu_stage, sem.at[0]),
                pltpu.make_async_copy(wdn_hbm.at[li, e], dn_stage, sem.at[1]))

    @pl.when(i < nu_ref[0])
    def _():
        e = be_ref[i]

        @pl.when(i == 0)
        def _():
            for copy in fetch(e):
                copy.start()

        @pl.when(jnp.logical_or(i == 0, e != be_ref[jnp.maximum(i - 1, 0)]))
        def _():
            for copy in fetch(e):
                copy.wait()
            wgu_s[...] = gu_stage[...].astype(BF16)
            wdn_s[...] = dn_stage[...].astype(BF16)
            nxt = nx_ref[i]

            @pl.when(nxt >= 0)
            def _():
                for copy in fetch(nxt):
                    copy.start()

        gu = _dot(_unpack_rows(x_ref[...]), wgu_s[...]) + bgu_ref[0]
        glu = jnp.minimum(gu[:, :ff], SWIGLU_LIMIT)
        lin = jnp.clip(gu[:, ff:], -SWIGLU_LIMIT, SWIGLU_LIMIT)
        act = glu * _sigmoid(SWIGLU_ALPHA * glu) * (lin + 1.0)
        y_ref[...] = _pack_rows((_dot(act.astype(BF16), wdn_s[...]) + bdn_ref[0]).astype(BF16))


def _moe_experts(li, blk_e, nxt_e, n_used, xs, w_gu, b_gu, w_dn, b_dn):
    n_rows, packed_w = xs.shape
    depth, n_exp, d, ff2 = w_gu.shape
    ff = ff2 // 2
    n_blocks = n_rows // MOE_TM

    def row_map(i, be, nx, nu):
        return (jnp.minimum(i, nu[0] - 1), 0)

    def b_map(i, be, nx, nu):
        return (li, be[jnp.minimum(i, nu[0] - 1)], 0, 0)

    grid_spec = pltpu.PrefetchScalarGridSpec(
        num_scalar_prefetch=3,
        grid=(n_blocks,),
        in_specs=[pl.BlockSpec((MOE_TM, packed_w), row_map),
                  pl.BlockSpec(memory_space=pl.ANY),
                  pl.BlockSpec((None, 1, 1, ff2), b_map),
                  pl.BlockSpec(memory_space=pl.ANY),
                  pl.BlockSpec((None, 1, 1, d), b_map)],
        out_specs=pl.BlockSpec((MOE_TM, packed_w), row_map),
        scratch_shapes=[pltpu.VMEM((d, ff2), F32), pltpu.VMEM((ff, d), F32),
                        pltpu.VMEM((d, ff2), BF16), pltpu.VMEM((ff, d), BF16),
                        pltpu.SemaphoreType.DMA((2,))],
    )
    return pl.pallas_call(
        functools.partial(_moe_kernel, li=li),
        out_shape=jax.ShapeDtypeStruct((n_rows, packed_w), jnp.int32),
        grid_spec=grid_spec,
        compiler_params=_params(1),
        name="moe_experts",
    )(blk_e, nxt_e, n_used, xs, w_gu, b_gu.reshape(depth, n_exp, 1, ff2), w_dn, b_dn.reshape(depth, n_exp, 1, d))


def _layout(route, cnt, n_exp):
    b, l, _ = route.shape
    t = b * l
    ids = route[..., 0:TOP_K].astype(jnp.int32)
    rank = route[..., 2 * TOP_K:3 * TOP_K].astype(jnp.int32)
    counts = cnt[0, :n_exp].astype(jnp.int32)
    padded = (counts + MOE_TM - 1) // MOE_TM * MOE_TM
    pad_end = jnp.cumsum(padded)
    pad_start = pad_end - padded
    onehot = ids[..., None] == jnp.arange(n_exp, dtype=jnp.int32)
    dest = jnp.sum(jnp.where(onehot, pad_start, 0), axis=-1) + rank
    dest = dest.reshape(t, TOP_K).T
    n_blocks = -(-t * TOP_K // MOE_TM) + n_exp
    blk_start = jnp.arange(n_blocks, dtype=jnp.int32) * MOE_TM
    blk_e = jnp.minimum(jnp.sum(blk_start[:, None] >= pad_end[None, :], axis=1), n_exp - 1).astype(jnp.int32)
    n_used = (pad_end[-1] // MOE_TM).astype(jnp.int32).reshape(1)
    ar = jnp.arange(n_exp, dtype=jnp.int32)
    later = jnp.logical_and(ar[None, :] > ar[:, None], (counts > 0)[None, :])
    nxt_of = jnp.min(jnp.where(later, ar[None, :], n_exp), axis=1)
    nxt_of = jnp.where(nxt_of == n_exp, -1, nxt_of)
    nxt_e = jnp.sum(jnp.where(blk_e[:, None] == ar[None, :], nxt_of[None, :], 0), axis=1).astype(jnp.int32)
    return dest, blk_e, nxt_e, n_used, n_blocks * MOE_TM


def _sc_mesh():
    return plsc.VectorSubcoreMesh(core_axis_name="c", subcore_axis_name="s",
                                  num_cores=SC_CORES, num_subcores=SC_SUBCORES)


def _sc_worker():
    return lax.axis_index("s") * SC_CORES + lax.axis_index("c")


def _sc_scatter_rows(x, idx, n_rows):
    t, d = x.shape
    n_idx = idx.shape[0]
    workers = SC_CORES * SC_SUBCORES
    n_chunks = n_idx // (workers * SC_SCATTER_ROWS)
    assert n_chunks * workers * SC_SCATTER_ROWS == n_idx and t % SC_SCATTER_ROWS == 0
    idx3 = idx.reshape(workers, n_chunks, SC_SCATTER_ROWS)

    @pl.kernel(out_type=jax.ShapeDtypeStruct((n_rows, d), x.dtype), mesh=_sc_mesh(),
               scratch_types=[pltpu.VMEM((n_chunks, SC_SCATTER_ROWS), jnp.int32),
                              pltpu.VMEM((SC_SCATTER_ROWS, d), x.dtype),
                              pltpu.SemaphoreType.DMA])
    def scatter_kernel(x_hbm, i_hbm, o_hbm, idx_v, rows_v, sem):
        wid = _sc_worker()
        pltpu.sync_copy(i_hbm.at[wid], idx_v)

        @pl.loop(0, n_chunks)
        def _(j):
            src = pl.multiple_of(((wid * n_chunks + j) * SC_SCATTER_ROWS) % t, SC_SCATTER_ROWS)
            pltpu.sync_copy(x_hbm.at[pl.ds(src, SC_SCATTER_ROWS)], rows_v)
            pltpu.async_copy(rows_v, o_hbm.at[idx_v.at[j]], sem).wait()

    return scatter_kernel(x, idx3)


def _sc_gather_rows(x, idx):
    d = x.shape[1]
    n_idx = idx.shape[0]
    workers = SC_CORES * SC_SUBCORES
    per_worker = n_idx // workers
    n_chunks = per_worker // SC_GATHER_ROWS
    assert n_chunks * workers * SC_GATHER_ROWS == n_idx

    assert n_chunks % 2 == 0

    @pl.kernel(out_type=jax.ShapeDtypeStruct((n_idx, d), x.dtype), mesh=_sc_mesh(),
               scratch_types=[pltpu.VMEM((per_worker,), jnp.int32),
                              pltpu.VMEM((SC_GATHER_ROWS, d), x.dtype),
                              pltpu.VMEM((SC_GATHER_ROWS, d), x.dtype),
                              pltpu.SemaphoreType.DMA, pltpu.SemaphoreType.DMA])
    def gather_kernel(x_hbm, i_hbm, o_hbm, idx_v, buf0, buf1, sem0, sem1):
        base = _sc_worker() * per_worker
        pltpu.sync_copy(i_hbm.at[pl.ds(base, per_worker)], idx_v)
        bufs, sems = (buf0, buf1), (sem0, sem1)

        def gather(chunk, b):
            off = pl.multiple_of(chunk * SC_GATHER_ROWS, SC_GATHER_ROWS)
            return pltpu.make_async_copy(x_hbm.at[idx_v.at[pl.ds(off, SC_GATHER_ROWS)]], bufs[b], sems[b])

        gather(0, 0).start()

        @pl.loop(0, n_chunks, step=2)
        def _(j):
            for b in range(2):
                chunk = j + b
                gather(chunk, b).wait()
                if b == 0:
                    gather(chunk + 1, 1).start()
                else:
                    @pl.when(chunk + 1 < n_chunks)
                    def _():
                        gather(chunk + 1, 0).start()
                off = pl.multiple_of(chunk * SC_GATHER_ROWS, SC_GATHER_ROWS)
                pltpu.sync_copy(bufs[b], o_hbm.at[pl.ds(base + off, SC_GATHER_ROWS)])

    return gather_kernel(x, idx)


def _combine(y4_ref, route_ref):
    route = route_ref[0]
    y = None
    for k in range(TOP_K):
        t = route[:, TOP_K + k:TOP_K + k + 1] * _unpack_rows(y4_ref[k, 0]).astype(F32)
        y = t if y is None else y + t
    return y


def _final_kernel(x_ref, y4_ref, route_ref, mod_ref, g_ref, o_ref):
    x = x_ref[0] + mod_ref[0, 0][5:6] * _combine(y4_ref, route_ref)
    o_ref[0] = _rms(x) * g_ref[...]


def _final(li, x_all, y4, route, modsel, skip, final_g):
    b, l, d = x_all.shape
    off = skip // TM
    lo = l - skip
    return pl.pallas_call(
        _final_kernel, out_shape=jax.ShapeDtypeStruct((b, lo, d), F32), grid=(b, lo // TM),
        in_specs=[pl.BlockSpec((1, TM, d), lambda i, j: (i, j + off, 0)),
                  pl.BlockSpec((TOP_K, 1, TM, y4.shape[3]), lambda i, j: (0, i, j + off, 0)),
                  pl.BlockSpec((1, TM, LANES), lambda i, j: (i, j + off, 0)),
                  _mod_spec(li, off)(d),
                  pl.BlockSpec((1, d), lambda i, j: (0, 0))],
        out_specs=pl.BlockSpec((1, TM, d), lambda i, j: (i, j, 0)),
        compiler_params=_params(2), name="final_norm",
    )(x_all, y4, route, modsel, final_g)


def _rope_tables(n, ctx, rot_dim, group_pattern):
    t = jnp.arange(n, dtype=jnp.int32)
    row = (t // GRID_W).astype(F32)
    col = (t % GRID_W).astype(F32)
    per_axis = rot_dim // 2
    inv = ROPE_BASE ** (-jnp.arange(0, per_axis, 2, dtype=F32) / per_axis)
    ang = jnp.concatenate([row[:, None] * inv[None], col[:, None] * inv[None]], axis=-1)
    cos, sin = jnp.cos(ang), jnp.sin(ang)
    half = rot_dim // 2
    c = jnp.ones((n, LANES), F32)
    sdn = jnp.zeros((n, LANES), F32)
    sup = jnp.zeros((n, LANES), F32)
    for off in group_pattern:
        c = c.at[:, off:off + half].set(cos).at[:, off + half:off + rot_dim].set(cos)
        sdn = sdn.at[:, off:off + half].set(-sin)
        sup = sup.at[:, off + half:off + rot_dim].set(sin)
    tab = jnp.stack([c, sdn, sup])
    ident = jnp.stack([jnp.ones((ctx, LANES), F32), jnp.zeros((ctx, LANES), F32), jnp.zeros((ctx, LANES), F32)])
    return jnp.concatenate([ident, tab], axis=1)


def _wprep_kernel(w_ref, o_ref):
    x = w_ref[0]
    rows = x.shape[0]
    o = 3 * 512 + MLA_Q_LORA + MLA_KV_LORA
    zeros = lambda n: jnp.zeros((rows, n), F32)
    kr = jnp.concatenate([zeros(MLA_NOPE), x[:, o:o + MLA_ROPE], zeros(LANES - MLA_NOPE - MLA_ROPE)], axis=1)
    o += MLA_ROPE
    cq = x[:, o:o + 512]
    k0, k1 = x[:, o + 512:o + 576], x[:, o + 576:o + 640]
    v0, v1 = x[:, o + 640:o + 704], x[:, o + 704:o + 768]
    g = x[:, o + 768:]
    o_ref[0] = jnp.concatenate([x[:, :3 * 512 + MLA_Q_LORA + MLA_KV_LORA], kr, cq, k0, k0, k1, k1, v0, v0, v1, v1, g],
                               axis=1).astype(BF16)


def _wprep(w_in):
    depth, d, cols = w_in.shape
    out_cols = cols - MLA_ROPE + LANES + 2 * 128
    tr = 128
    return pl.pallas_call(
        _wprep_kernel,
        out_shape=jax.ShapeDtypeStruct((depth, d, out_cols), BF16),
        grid=(depth, d // tr),
        in_specs=[pl.BlockSpec((1, tr, cols), lambda l, i: (l, i, 0))],
        out_specs=pl.BlockSpec((1, tr, out_cols), lambda l, i: (l, i, 0)),
        compiler_params=_params(2),
        name="wprep",
    )(w_in)


def _prep_weights(w_in, mla_w_uq, mla_w_ukv):
    depth = w_in.shape[0]
    w = _wprep(w_in)
    uq = mla_w_uq.reshape(depth, MLA_Q_LORA, MLA_HEADS, MLA_NOPE + MLA_ROPE)
    uq = jnp.pad(uq, ((0, 0), (0, 0), (0, 0), (0, LANES - MLA_NOPE - MLA_ROPE)))
    uq = uq.reshape(depth, MLA_Q_LORA, MLA_HEADS * LANES).astype(BF16)
    ukv = mla_w_ukv.reshape(depth, MLA_KV_LORA, MLA_HEADS, MLA_NOPE + MLA_V)
    uk = jnp.pad(ukv[..., :MLA_NOPE], ((0, 0), (0, 0), (0, 0), (0, LANES - MLA_NOPE)))
    uk = uk.reshape(depth, MLA_KV_LORA, MLA_HEADS * LANES)
    uv = ukv[..., MLA_NOPE:].reshape(depth, MLA_KV_LORA, MLA_HEADS * MLA_V)
    ukv = jnp.concatenate([uk, uv], axis=-1).astype(BF16)
    return w, uq, ukv


def kernel(x, c, ctx, c_ctx, norm1_g, norm2_g, w_mod, b_mod, w_in, na_rpb, mla_q_norm_g, mla_kv_norm_g, mla_w_uq, mla_w_ukv, swa_sink, w_branch, w_out, router_w, router_b, expert_w_gate_up, expert_b_gate_up, expert_w_down, expert_b_down, final_norm_g):
    b, n, d = x.shape
    lc = ctx.shape[1]
    l = lc + n
    depth = w_in.shape[0]
    n_exp = router_w.shape[2]
    assert lc == TM and n % TM == 0 and TM == NA_QROWS * GRID_W

    cvec = jnp.zeros((8, d), F32).at[:b].set(c).at[b].set(c_ctx)
    mod = _modulation(cvec, w_mod, b_mod)
    mod_lat = mod[:, :b].reshape(depth, b, 1, 6, d)
    mod_ctx = jnp.broadcast_to(mod[:, b].reshape(depth, 1, 1, 6, d), (depth, b, 1, 6, d))
    modsel = jnp.concatenate([mod_ctx, mod_lat], axis=2)

    w_all, uq_all, ukv_all = _prep_weights(w_in, mla_w_uq, mla_w_ukv)
    wbr_all = w_branch.astype(BF16)
    wout_all = w_out.astype(BF16)
    rw_all = jnp.pad(router_w, ((0, 0), (0, 0), (0, LANES - n_exp))).astype(BF16)
    rb_all = jnp.pad(router_b, ((0, 0), (0, LANES - n_exp))).reshape(depth, 1, LANES)
    rope_b = _rope_tables(n, lc, MLA_ROPE, (MLA_NOPE,))
    rope_c = _rope_tables(n, lc, SWA_HEAD_DIM, (0, SWA_HEAD_DIM))
    bias_all = _na_bias_tables(na_rpb, n // GRID_W)

    g1_all = norm1_g.reshape(depth, 1, d)
    g2_all = norm2_g.reshape(depth, 1, d)
    gq_all = mla_q_norm_g.reshape(depth, 1, -1)
    gkv_all = mla_kv_norm_g.reshape(depth, 1, -1)

    x_all = jnp.concatenate([ctx, x], axis=1)
    moe = None
    for li in range(depth):
        ms = modsel
        outs = _inproj(li, x_all, moe, ms, g1_all, w_all, uq_all, ukv_all, gq_all, gkv_all, rope_b, rope_c)
        if moe is not None:
            x_all, outs = outs[0], outs[1:]
        aq, ak, av, bq, bk, bv, cq, ck, cv, gate = outs
        oa = _na_attention(li, aq, ak, av, bias_all, lc)
        ob = _mla_attention(bq, bk, bv, lc)
        oc = _swa_attention(swa_sink[li], cq, ck, cv, lc)
        x_all, h2, route, cnt = _merge(li, oa, ob, oc, gate, x_all, ms, wbr_all, wout_all,
                                       g2_all, rw_all, rb_all, n_exp)
        dest, blk_e, nxt_e, n_used, n_rows = _layout(route, cnt, n_exp)
        dest = dest.reshape(TOP_K * b * l)
        xs = _sc_scatter_rows(h2.reshape(b * l, d // 2), dest, n_rows)
        ys = _moe_experts(li, blk_e, nxt_e, n_used, xs, expert_w_gate_up, expert_b_gate_up,
                          expert_w_down, expert_b_down)
        moe = (_sc_gather_rows(ys, dest).reshape(TOP_K, b, l, d // 2), route)
    return _final(depth - 1, x_all, moe[0], moe[1], modsel, lc, final_norm_g.reshape(1, d))
```

```python
import functools

import numpy as np
import jax
import jax.numpy as jnp
from jax import lax
from jax.experimental import pallas as pl
from jax.experimental.pallas import tpu as pltpu
from jax.experimental.pallas import tpu_sc as plsc

GRID_W = 64
EPS = 1e-6
ROPE_BASE = 10000.0
NEG_INF = -1e30
LOG2E = 1.4426950408889634
LANES = 128

NA_HEADS = 8
NA_HEAD_DIM = 64
NA_KH = 8
NA_KW = 16
NA_QROWS = 4
NA_SLAB = 12
MLA_HEADS = 8
MLA_NOPE = 64
MLA_ROPE = 32
MLA_V = 64
MLA_KEY_PARTS = 4
MLA_Q_LORA = 256
MLA_KV_LORA = 128
SWA_HEADS = 8
SWA_KV_HEADS = 2
SWA_HEAD_DIM = 64
SWA_WINDOW = 128
N_BRANCH = 3
BRANCH_W = 512
N_EXPERTS = 32
TOP_K = 4
SWIGLU_ALPHA = 1.702
SWIGLU_LIMIT = 7.0

TM = 256
MOE_TM = 512
SC_CORES = 2
SC_SUBCORES = 16
SC_SCATTER_ROWS = 128
SC_GATHER_ROWS = 64
VMEM_LIMIT = 56 * 1024 * 1024

BF16 = jnp.bfloat16
F32 = jnp.float32


def _dot(a, b):
    return jnp.dot(a, b, preferred_element_type=F32)


def _dot_nt(a, b):
    return lax.dot_general(a, b, (((1,), (1,)), ((), ())), preferred_element_type=F32)


def _params(n_axes, vmem=VMEM_LIMIT):
    return pltpu.CompilerParams(dimension_semantics=("arbitrary",) * n_axes, vmem_limit_bytes=vmem)


def _layer_spec(a, li):
    return pl.BlockSpec((None,) + a.shape[1:], lambda *_: (li,) + (0,) * (a.ndim - 1))


def _mod_spec(li, off=0):
    return lambda d: pl.BlockSpec((None, 1, 1, 6, d), lambda i, j: (li, i, jnp.minimum(j + off, 1), 0, 0))


def _rms(x):
    return x * lax.rsqrt(jnp.mean(x * x, axis=-1, keepdims=True) + EPS)


def _sigmoid(x):
    return 1.0 / (1.0 + jnp.exp(-x))


def _pack_rows(xb):
    half = xb.shape[1] // 2
    lo = pltpu.bitcast(xb[:, :half].astype(F32), jnp.int32)
    hi = pltpu.bitcast(xb[:, half:].astype(F32), jnp.int32)
    return (hi & jnp.int32(-65536)) | lax.shift_right_logical(lo, 16)


def _unpack_rows(w):
    lo = pltpu.bitcast(lax.shift_left(w, 16), F32).astype(BF16)
    hi = pltpu.bitcast(w & jnp.int32(-65536), F32).astype(BF16)
    return jnp.concatenate([lo, hi], axis=1)


def _mod_kernel(c_ref, w_ref, b_ref, o_ref):
    c = c_ref[...]
    s = (c * _sigmoid(c)).astype(BF16)
    o_ref[0] = _dot(s, w_ref[0].astype(BF16)) + b_ref[0]


def _modulation(cvec, w_mod, b_mod):
    depth, d, n6 = w_mod.shape
    tn = n6 // 4
    return pl.pallas_call(
        _mod_kernel,
        out_shape=jax.ShapeDtypeStruct((depth, 8, n6), F32),
        grid=(depth, n6 // tn),
        in_specs=[pl.BlockSpec((8, d), lambda l, j: (0, 0)),
                  pl.BlockSpec((1, d, tn), lambda l, j: (l, 0, j)),
                  pl.BlockSpec((1, 1, tn), lambda l, j: (l, 0, j))],
        out_specs=pl.BlockSpec((1, 8, tn), lambda l, j: (l, 0, j)),
        compiler_params=_params(2),
        name="modulation",
    )(cvec, w_mod, b_mod.reshape(depth, 1, n6))


def _rope_groups(x, tab_ref, shift):
    cos, sdn, sup = tab_ref[0], tab_ref[1], tab_ref[2]
    outs = []
    for g in range(x.shape[1] // LANES):
        xg = x[:, g * LANES:(g + 1) * LANES]
        outs.append(xg * cos + pltpu.roll(xg, LANES - shift, 1) * sdn + pltpu.roll(xg, shift, 1) * sup)
    return outs[0] if len(outs) == 1 else jnp.concatenate(outs, axis=1)


def _inproj_kernel(*refs, with_moe):
    if with_moe:
        x_ref, y4_ref, route_ref, modp_ref = refs[:4]
        refs = refs[4:]
        (mod_ref, g1_ref, w_ref, wuq_ref, wukv_ref, gq_ref, gkv_ref, rb_ref, rc_ref, xo_ref,
         aq_ref, ak_ref, av_ref, bq_ref, bk_ref, bv_ref, cq_ref, ck_ref, cv_ref, gate_ref) = refs
        x = x_ref[0] + modp_ref[0, 0][5:6] * _combine(y4_ref, route_ref)
        xo_ref[0] = x
    else:
        (x_ref, mod_ref, g1_ref, w_ref, wuq_ref, wukv_ref, gq_ref, gkv_ref, rb_ref, rc_ref,
         aq_ref, ak_ref, av_ref, bq_ref, bk_ref, bv_ref, cq_ref, ck_ref, cv_ref, gate_ref) = refs
        x = x_ref[0]
    mod = mod_ref[0, 0]
    h = (_rms(x) * g1_ref[...]) * (1.0 + mod[1:2]) + mod[0:1]
    hb = h.astype(BF16)
    acc = _dot(hb, w_ref[:, 0:1536])
    aq_ref[0] = (acc[:, 0:512] * (NA_HEAD_DIM ** -0.5 * LOG2E)).astype(BF16)
    ak_ref[0] = acc[:, 512:1024].astype(BF16)
    av_ref[0] = acc[:, 1024:1536].astype(BF16)
    acc = _dot(hb, w_ref[:, 1536:2048])
    qn = (_rms(acc[:, 0:256]) * gq_ref[...]).astype(BF16)
    kvn = (_rms(acc[:, 256:384]) * gkv_ref[...]).astype(BF16)
    kr = _rope_groups(acc[:, 384:512], rb_ref, MLA_ROPE // 2)
    q = _rope_groups(_dot(qn, wuq_ref[...]), rb_ref, MLA_ROPE // 2)
    bq_ref[0] = (q * ((MLA_NOPE + MLA_ROPE) ** -0.5 * LOG2E)).astype(BF16)
    kv = _dot(kvn, wukv_ref[...])
    bk_ref[0] = (kv[:, 0:1024] + jnp.concatenate([kr] * MLA_HEADS, axis=1)).astype(BF16)
    bv_ref[0] = kv[:, 1024:1536].T.astype(BF16)
    acc = _dot(hb, w_ref[:, 2048:3072])
    cq_ref[0] = (_rope_groups(acc[:, 0:512], rc_ref, SWA_HEAD_DIM // 2) * (SWA_HEAD_DIM ** -0.5 * LOG2E)).astype(BF16)
    ck_ref[0] = _rope_groups(acc[:, 512:768], rc_ref, SWA_HEAD_DIM // 2).astype(BF16)
    cv_ref[0] = acc[:, 768:1024].astype(BF16)
    gate_ref[0] = _sigmoid(_dot(hb, w_ref[:, 3072:])).astype(BF16)


def _inproj(li, x_all, moe, modsel, g1, w, wuq, wukv, gq, gkv, rope_b, rope_c):
    b, l, d = x_all.shape
    nb = l // TM
    row = lambda width: pl.BlockSpec((1, TM, width), lambda i, j: (i, j, 0))
    full = lambda a: _layer_spec(a, li)
    widths = (512, 512, 512, 1024, 1024, 512, 512, 256, 256, N_BRANCH * d)
    in_specs = [row(d)]
    out_specs = [row(wd) for wd in widths]
    out_shape = [jax.ShapeDtypeStruct((b, l, wd), BF16) for wd in widths]
    bv_pos = 5
    out_specs[bv_pos] = pl.BlockSpec((1, widths[bv_pos], TM), lambda i, j: (i, 0, j))
    out_shape[bv_pos] = jax.ShapeDtypeStruct((b, widths[bv_pos], l), BF16)
    args = [x_all]
    if moe is not None:
        y4, route = moe
        in_specs += [pl.BlockSpec((TOP_K, 1, TM, y4.shape[3]), lambda i, j: (0, i, j, 0)), row(LANES),
                     _mod_spec(li - 1)(d)]
        out_specs = [row(d)] + out_specs
        out_shape = [jax.ShapeDtypeStruct((b, l, d), F32)] + out_shape
        args += [y4, route, modsel]
    in_specs += [_mod_spec(li)(d), full(g1), full(w), full(wuq), full(wukv), full(gq), full(gkv),
                 pl.BlockSpec((3, TM, LANES), lambda i, j: (0, j, 0)),
                 pl.BlockSpec((3, TM, LANES), lambda i, j: (0, j, 0))]
    args += [modsel, g1, w, wuq, wukv, gq, gkv, rope_b, rope_c]
    return pl.pallas_call(
        functools.partial(_inproj_kernel, with_moe=moe is not None),
        out_shape=out_shape,
        grid=(b, nb),
        in_specs=in_specs,
        out_specs=out_specs,
        compiler_params=_params(2),
        name="inproj",
    )(*args)


def _lane_lo():
    return lax.broadcasted_iota(jnp.int32, (1, LANES), 1) < (LANES // 2)


def _split_heads(qp, lo):
    zero = jnp.zeros_like(qp)
    return jnp.where(lo, qp, zero), jnp.where(lo, zero, qp)


def _softmax_pv(score_parts, value_parts, extra_logit=None):
    m = score_parts[0].max(axis=-1, keepdims=True)
    for s in score_parts[1:]:
        m = jnp.maximum(m, s.max(axis=-1, keepdims=True))
    if extra_logit is not None:
        m = jnp.maximum(m, extra_logit)
    den = None
    acc = None
    for s, v in zip(score_parts, value_parts):
        e = jnp.exp2(s - m)
        d = e.sum(axis=-1, keepdims=True)
        den = d if den is None else den + d
        o = _dot(e.astype(BF16), v)
        acc = o if acc is None else acc + o
    if extra_logit is not None:
        den = den + jnp.exp2(extra_logit - m)
    return acc / den


def _na_kernel(q_ref, k_ref, v_ref, bias_ref, o_ref, *, ctx, rows):
    j = pl.program_id(1)
    lo = _lane_lo()

    @pl.when(j == 0)
    def _():
        for jp in range(NA_HEADS // 2):
            sl = slice(jp * LANES, (jp + 1) * LANES)
            kc, vc = k_ref[0, 0:ctx, sl], v_ref[0, 0:ctx, sl]
            outs = [_softmax_pv([_dot_nt(qm, kc)], [vc]) for qm in _split_heads(q_ref[0, :, sl], lo)]
            o_ref[0, :, sl] = jnp.where(lo, outs[0], outs[1]).astype(BF16)

    @pl.when(j > 0)
    def _():
        r = (j - 1) * NA_QROWS
        s0 = jnp.clip(r - NA_KH // 2, 0, rows - NA_SLAB)
        start = pl.multiple_of(ctx + s0 * GRID_W, GRID_W)
        slab = pl.ds(start, NA_SLAB * GRID_W)

        def scores(h):
            sl = slice((h // 2) * LANES, (h // 2 + 1) * LANES)
            qm = _split_heads(q_ref[0, :, sl], lo)[h % 2]
            return [_dot_nt(qm, k_ref[0, 0:ctx, sl]), _dot_nt(qm, k_ref[0, slab, sl]) + bias_ref[0, h]]

        ahead = 1
        pending = [scores(h) for h in range(ahead)]
        outs = []
        for h in range(NA_HEADS):
            if h + ahead < NA_HEADS:
                pending.append(scores(h + ahead))
            sl = slice((h // 2) * LANES, (h // 2 + 1) * LANES)
            outs.append(_softmax_pv(pending.pop(0), [v_ref[0, 0:ctx, sl], v_ref[0, slab, sl]]))
            if h % 2 == 1:
                o_ref[0, :, sl] = jnp.where(lo, outs[h - 1], outs[h]).astype(BF16)


def _na_attention(li, aq, ak, av, bias, ctx):
    b, l, w = aq.shape
    nb = l // TM
    rows = (l - ctx) // GRID_W
    last = nb - 1

    def bias_map(i, j):
        return (li, jnp.where(j <= 1, 0, jnp.where(j == last, 2, 1)), 0, 0, 0)

    return pl.pallas_call(
        functools.partial(_na_kernel, ctx=ctx, rows=rows),
        out_shape=jax.ShapeDtypeStruct((b, l, w), BF16),
        grid=(b, nb),
        in_specs=[pl.BlockSpec((1, TM, w), lambda i, j: (i, j, 0)),
                  pl.BlockSpec((1, l, w), lambda i, j: (i, 0, 0)),
                  pl.BlockSpec((1, l, w), lambda i, j: (i, 0, 0)),
                  pl.BlockSpec((None, 1, NA_HEADS, TM, NA_SLAB * GRID_W), bias_map)],
        out_specs=pl.BlockSpec((1, TM, w), lambda i, j: (i, j, 0)),
        compiler_params=_params(2),
        name="na_attention",
    )(aq, ak, av, bias)


def _na_bias_tables(na_rpb, rows):
    depth = na_rpb.shape[0]
    qc = np.arange(GRID_W)[:, None]
    kc = np.arange(GRID_W)[None, :]
    c0 = np.clip(qc - NA_KW // 2, 0, GRID_W - NA_KW)
    col_ok = (kc >= c0) & (kc < c0 + NA_KW)
    col_idx = np.clip(kc - qc + NA_KW - 1, 0, 2 * NA_KW - 2)
    bc = jnp.take(na_rpb, jnp.asarray(col_idx.reshape(-1)), axis=3).reshape(
        depth, NA_HEADS, 2 * NA_KH - 1, GRID_W, GRID_W)
    i_idx = np.zeros((3, NA_QROWS, NA_SLAB), np.int32)
    ok = np.zeros((3, NA_QROWS, NA_SLAB, GRID_W, GRID_W), bool)
    for case, (r, s) in enumerate(((0, 0), (NA_KH // 2, 0), (rows - NA_QROWS, rows - NA_SLAB))):
        for a in range(NA_QROWS):
            qr = r + a
            r0 = min(max(qr - NA_KH // 2, 0), rows - NA_KH)
            for c in range(NA_SLAB):
                kr = s + c
                inside = r0 <= kr < r0 + NA_KH
                i_idx[case, a, c] = min(max(kr - qr + NA_KH - 1, 0), 2 * NA_KH - 2)
                ok[case, a, c] = col_ok & inside
    t = jnp.take(bc, jnp.asarray(i_idx.reshape(-1)), axis=2).reshape(
        depth, NA_HEADS, 3, NA_QROWS, NA_SLAB, GRID_W, GRID_W)
    t = jnp.where(jnp.asarray(ok)[None, None], t * LOG2E, NEG_INF)
    t = t.transpose(0, 2, 1, 3, 5, 4, 6)
    return t.reshape(depth, 3, NA_HEADS, NA_QROWS * GRID_W, NA_SLAB * GRID_W)


def _mla_kernel(q_ref, k_ref, vt_ref, o_ref, *, ctx):
    j = pl.program_id(1)

    def run(nkeys):
        tiles = nkeys // 256
        nparts = min(MLA_KEY_PARTS, tiles)
        cuts = [(tiles * p // nparts) * 256 for p in range(nparts + 1)]
        parts = list(zip(cuts[:-1], cuts[1:]))

        def scores(h):
            hsl = slice(h * LANES, (h + 1) * LANES)
            return [_dot_nt(k_ref[0, a:b, hsl], q_ref[0, :, hsl]) for a, b in parts]

        ahead = 3
        pending = [scores(h) for h in range(ahead)]
        outs = []
        for h in range(MLA_HEADS):
            if h + ahead < MLA_HEADS:
                pending.append(scores(h + ahead))
            s_parts = pending.pop(0)
            m = s_parts[0].max(axis=0, keepdims=True)
            for s in s_parts[1:]:
                m = jnp.maximum(m, s.max(axis=0, keepdims=True))
            den = None
            o = None
            for (a, b), s in zip(parts, s_parts):
                e = jnp.exp2(s - m)
                d = e.sum(axis=0, keepdims=True)
                den = d if den is None else den + d
                t = _dot(vt_ref[0, h * MLA_V:(h + 1) * MLA_V, a:b], e.astype(BF16))
                o = t if o is None else o + t
            outs.append(o / den)
        o_ref[0] = jnp.concatenate(outs, axis=0).T.astype(BF16)

    @pl.when(j == 0)
    def _():
        run(ctx)

    @pl.when(j > 0)
    def _():
        run(k_ref.shape[1])


def _mla_attention(bq, bk, bvt, ctx):
    b, l, wq = bq.shape
    wv = bvt.shape[1]
    return pl.pallas_call(
        functools.partial(_mla_kernel, ctx=ctx),
        out_shape=jax.ShapeDtypeStruct((b, l, wv), BF16),
        grid=(b, l // TM),
        in_specs=[pl.BlockSpec((1, TM, wq), lambda i, j: (i, j, 0)),
                  pl.BlockSpec((1, l, wq), lambda i, j: (i, 0, 0)),
                  pl.BlockSpec((1, wv, l), lambda i, j: (i, 0, 0))],
        out_specs=pl.BlockSpec((1, TM, wv), lambda i, j: (i, j, 0)),
        compiler_params=_params(2),
        name="mla_attention",
    )(bq, bk, bvt)


def _swa_kernel(sink_ref, q_ref, k_ref, v_ref, o_ref, *, ctx, n_lat):
    j = pl.program_id(1)
    lo = _lane_lo()
    group = SWA_HEADS // SWA_KV_HEADS
    band = TM + 2 * SWA_WINDOW

    def stacked_q(kv):
        parts = []
        for jp in range(kv * group // 2, (kv + 1) * group // 2):
            parts.extend(_split_heads(q_ref[0, :, jp * LANES:(jp + 1) * LANES], lo))
        return jnp.concatenate(parts, axis=0)

    def finish(kv, score_parts, value_parts, keep=None):
        outs = []
        for g in range(group):
            rs = slice(g * TM, (g + 1) * TM)
            sink = jnp.full((1, 1), sink_ref[kv * group + g] * LOG2E, F32)
            parts = [s[rs] for s in score_parts]
            if keep is not None:
                parts[-1] = jnp.where(keep, parts[-1], NEG_INF)
            outs.append(_softmax_pv(parts, value_parts, extra_logit=sink))
        for p in range(group // 2):
            jp = kv * group // 2 + p
            o_ref[0, :, jp * LANES:(jp + 1) * LANES] = jnp.where(lo, outs[2 * p], outs[2 * p + 1]).astype(BF16)

    @pl.when(j == 0)
    def _():
        for kv in range(SWA_KV_HEADS):
            sl = slice(kv * LANES, (kv + 1) * LANES)
            kc, vc = k_ref[0, 0:ctx, sl], v_ref[0, 0:ctx, sl]
            finish(kv, [_dot_nt(stacked_q(kv), kc)], [vc])

    @pl.when(j > 0)
    def _():
        q0 = (j - 1) * TM
        s0 = jnp.clip(q0 - SWA_WINDOW, 0, n_lat - band)
        rows = pl.ds(pl.multiple_of(ctx + s0, SWA_WINDOW), band)
        qpos = q0 + lax.broadcasted_iota(jnp.int32, (TM, band), 0)
        kpos = s0 + lax.broadcasted_iota(jnp.int32, (TM, band), 1)
        keep = jnp.abs(qpos - kpos) <= SWA_WINDOW

        def scores(kv):
            sl = slice(kv * LANES, (kv + 1) * LANES)
            qs = stacked_q(kv)
            return [_dot_nt(qs, k_ref[0, 0:ctx, sl]), _dot_nt(qs, k_ref[0, rows, sl])]

        s_all = [scores(kv) for kv in range(SWA_KV_HEADS)]
        for kv in range(SWA_KV_HEADS):
            sl = slice(kv * LANES, (kv + 1) * LANES)
            finish(kv, s_all[kv], [v_ref[0, 0:ctx, sl], v_ref[0, rows, sl]], keep)


def _swa_attention(sink, cq, ck, cv, ctx):
    b, l, w = cq.shape
    wk = ck.shape[2]
    grid_spec = pltpu.PrefetchScalarGridSpec(
        num_scalar_prefetch=1,
        grid=(b, l // TM),
        in_specs=[pl.BlockSpec((1, TM, w), lambda i, j, s: (i, j, 0)),
                  pl.BlockSpec((1, l, wk), lambda i, j, s: (i, 0, 0)),
                  pl.BlockSpec((1, l, wk), lambda i, j, s: (i, 0, 0))],
        out_specs=pl.BlockSpec((1, TM, w), lambda i, j, s: (i, j, 0)),
    )
    return pl.pallas_call(
        functools.partial(_swa_kernel, ctx=ctx, n_lat=l - ctx),
        out_shape=jax.ShapeDtypeStruct((b, l, w), BF16),
        grid_spec=grid_spec,
        compiler_params=_params(2),
        name="swa_attention",
    )(sink, cq, ck, cv)


def _merge_kernel(oa_ref, ob_ref, oc_ref, gate_ref, x_ref, mod_ref, wbr_ref, wout_ref, g2_ref,
                  rw_ref, rb_ref, xo_ref, h2_ref, route_ref, cnt_ref, run_ref, *, n_exp):
    d = x_ref.shape[2]

    @pl.when(jnp.logical_and(pl.program_id(0) == 0, pl.program_id(1) == 0))
    def _():
        run_ref[...] = jnp.zeros_like(run_ref)

    mod = mod_ref[0, 0]
    mix = None
    for i, o_ref in enumerate((oa_ref, ob_ref, oc_ref)):
        t = gate_ref[0, :, i * d:(i + 1) * d].astype(F32) * _dot(o_ref[0], wbr_ref[i])
        mix = t if mix is None else mix + t
    y = _dot(mix.astype(BF16), wout_ref[...])
    x = x_ref[0] + mod[2:3] * y
    xo_ref[0] = x
    h2 = (_rms(x) * g2_ref[...]) * (1.0 + mod[4:5]) + mod[3:4]
    h2b = h2.astype(BF16)
    h2_ref[0] = _pack_rows(h2b)
    logits = _dot(h2b, rw_ref[...]) + rb_ref[...]
    lane = lax.broadcasted_iota(jnp.int32, logits.shape, 1).astype(F32)
    work = jnp.where(lane < n_exp, logits, -jnp.inf)
    ids, vals = [], []
    for _ in range(TOP_K):
        m = work.max(axis=-1, keepdims=True)
        idx = jnp.where(work == m, lane, float(LANES)).min(axis=-1, keepdims=True)
        ids.append(idx)
        vals.append(m)
        work = jnp.where(lane == idx, -jnp.inf, work)
    ex = [jnp.exp(v - vals[0]) for v in vals]
    den = ex[0] + ex[1] + ex[2] + ex[3]
    hits = jnp.zeros(logits.shape, F32)
    for idx in ids:
        hits = hits + jnp.where(lane == idx, 1.0, 0.0)
    r = lax.broadcasted_iota(jnp.int32, (TM, TM), 0)
    c = lax.broadcasted_iota(jnp.int32, (TM, TM), 1)
    tri = jnp.where(c < r, 1.0, 0.0).astype(BF16)
    before = _dot(tri, hits.astype(BF16)) + run_ref[0:1]
    route = jnp.zeros(logits.shape, F32)
    for k in range(TOP_K):
        rank = jnp.where(lane == ids[k], before, 0.0).sum(axis=-1, keepdims=True)
        route = jnp.where(lane == k, ids[k], route)
        route = jnp.where(lane == TOP_K + k, ex[k] / den, route)
        route = jnp.where(lane == 2 * TOP_K + k, rank, route)
    route_ref[0] = route
    run_ref[...] = run_ref[...] + hits.sum(axis=0, keepdims=True)
    cnt_ref[...] = run_ref[...]


def _merge(li, oa, ob, oc, gate, x_all, modsel, wbr, wout, g2, rw, rb, n_exp):
    b, l, d = x_all.shape
    row = lambda width: pl.BlockSpec((1, TM, width), lambda i, j: (i, j, 0))
    full = lambda a: _layer_spec(a, li)
    return pl.pallas_call(
        functools.partial(_merge_kernel, n_exp=n_exp),
        out_shape=[jax.ShapeDtypeStruct((b, l, d), F32),
                   jax.ShapeDtypeStruct((b, l, d // 2), jnp.int32),
                   jax.ShapeDtypeStruct((b, l, LANES), F32),
                   jax.ShapeDtypeStruct((8, LANES), F32)],
        grid=(b, l // TM),
        in_specs=[row(BRANCH_W), row(BRANCH_W), row(BRANCH_W), row(N_BRANCH * d), row(d), _mod_spec(li)(d),
                  full(wbr), full(wout), full(g2), full(rw), full(rb)],
        out_specs=[row(d), row(d // 2), row(LANES), pl.BlockSpec((8, LANES), lambda i, j: (0, 0))],
        scratch_shapes=[pltpu.VMEM((8, LANES), F32)],
        compiler_params=_params(2),
        name="merge",
    )(oa, ob, oc, gate, x_all, modsel, wbr, wout, g2, rw, rb)


def _moe_kernel(be_ref, nx_ref, nu_ref, x_ref, wgu_hbm, bgu_ref, wdn_hbm, bdn_ref, y_ref,
                gu_stage, dn_stage, wgu_s, wdn_s, sem, *, li):
    i = pl.program_id(0)
    ff = wdn_s.shape[0]

    def fetch(e):
        return (pltpu.make_async_copy(wgu_hbm.at[li, e], gu_stage, sem.at[0]),
                pltpu.make_async_copy(wdn_hbm.at[li, e], dn_stage, sem.at[1]))

    @pl.when(i < nu_ref[0])
    def _():
        e = be_ref[i]

        @pl.when(i == 0)
        def _():
            for copy in fetch(e):
                copy.start()

        @pl.when(jnp.logical_or(i == 0, e != be_ref[jnp.maximum(i - 1, 0)]))
        def _():
            for copy in fetch(e):
                copy.wait()
            wgu_s[...] = gu_stage[...].astype(BF16)
            wdn_s[...] = dn_stage[...].astype(BF16)
            nxt = nx_ref[i]

            @pl.when(nxt >= 0)
            def _():
                for copy in fetch(nxt):
                    copy.start()

        gu = _dot(_unpack_rows(x_ref[...]), wgu_s[...]) + bgu_ref[0]
        glu = jnp.minimum(gu[:, :ff], SWIGLU_LIMIT)
        lin = jnp.clip(gu[:, ff:], -SWIGLU_LIMIT, SWIGLU_LIMIT)
        act = glu * _sigmoid(SWIGLU_ALPHA * glu) * (lin + 1.0)
        y_ref[...] = _pack_rows((_dot(act.astype(BF16), wdn_s[...]) + bdn_ref[0]).astype(BF16))


def _moe_experts(li, blk_e, nxt_e, n_used, xs, w_gu, b_gu, w_dn, b_dn):
    n_rows, packed_w = xs.shape
    depth, n_exp, d, ff2 = w_gu.shape
    ff = ff2 // 2
    n_blocks = n_rows // MOE_TM

    def row_map(i, be, nx, nu):
        return (jnp.minimum(i, nu[0] - 1), 0)

    def b_map(i, be, nx, nu):
        return (li, be[jnp.minimum(i, nu[0] - 1)], 0, 0)

    grid_spec = pltpu.PrefetchScalarGridSpec(
        num_scalar_prefetch=3,
        grid=(n_blocks,),
        in_specs=[pl.BlockSpec((MOE_TM, packed_w), row_map),
                  pl.BlockSpec(memory_space=pl.ANY),
                  pl.BlockSpec((None, 1, 1, ff2), b_map),
                  pl.BlockSpec(memory_space=pl.ANY),
                  pl.BlockSpec((None, 1, 1, d), b_map)],
        out_specs=pl.BlockSpec((MOE_TM, packed_w), row_map),
        scratch_shapes=[pltpu.VMEM((d, ff2), F32), pltpu.VMEM((ff, d), F32),
                        pltpu.VMEM((d, ff2), BF16), pltpu.VMEM((ff, d), BF16),
                        pltpu.SemaphoreType.DMA((2,))],
    )
    return pl.pallas_call(
        functools.partial(_moe_kernel, li=li),
        out_shape=jax.ShapeDtypeStruct((n_rows, packed_w), jnp.int32),
        grid_spec=grid_spec,
        compiler_params=_params(1),
        name="moe_experts",
    )(blk_e, nxt_e, n_used, xs, w_gu, b_gu.reshape(depth, n_exp, 1, ff2), w_dn, b_dn.reshape(depth, n_exp, 1, d))


def _layout(route, cnt, n_exp):
    b, l, _ = route.shape
    t = b * l
    ids = route[..., 0:TOP_K].astype(jnp.int32)
    rank = route[..., 2 * TOP_K:3 * TOP_K].astype(jnp.int32)
    counts = cnt[0, :n_exp].astype(jnp.int32)
    padded = (counts + MOE_TM - 1) // MOE_TM * MOE_TM
    pad_end = jnp.cumsum(padded)
    pad_start = pad_end - padded
    onehot = ids[..., None] == jnp.arange(n_exp, dtype=jnp.int32)
    dest = jnp.sum(jnp.where(onehot, pad_start, 0), axis=-1) + rank
    dest = dest.reshape(t, TOP_K).T
    n_blocks = -(-t * TOP_K // MOE_TM) + n_exp
    blk_start = jnp.arange(n_blocks, dtype=jnp.int32) * MOE_TM
    blk_e = jnp.minimum(jnp.sum(blk_start[:, None] >= pad_end[None, :], axis=1), n_exp - 1).astype(jnp.int32)
    n_used = (pad_end[-1] // MOE_TM).astype(jnp.int32).reshape(1)
    ar = jnp.arange(n_exp, dtype=jnp.int32)
    later = jnp.logical_and(ar[None, :] > ar[:, None], (counts > 0)[None, :])
    nxt_of = jnp.min(jnp.where(later, ar[None, :], n_exp), axis=1)
    nxt_of = jnp.where(nxt_of == n_exp, -1, nxt_of)
    nxt_e = jnp.sum(jnp.where(blk_e[:, None] == ar[None, :], nxt_of[None, :], 0), axis=1).astype(jnp.int32)
    return dest, blk_e, nxt_e, n_used, n_blocks * MOE_TM


def _sc_mesh():
    return plsc.VectorSubcoreMesh(core_axis_name="c", subcore_axis_name="s",
                                  num_cores=SC_CORES, num_subcores=SC_SUBCORES)


def _sc_worker():
    return lax.axis_index("s") * SC_CORES + lax.axis_index("c")


def _sc_scatter_rows(x, idx, n_rows):
    t, d = x.shape
    n_idx = idx.shape[0]
    workers = SC_CORES * SC_SUBCORES
    n_chunks = n_idx // (workers * SC_SCATTER_ROWS)
    assert n_chunks * workers * SC_SCATTER_ROWS == n_idx and t % SC_SCATTER_ROWS == 0
    idx3 = idx.reshape(workers, n_chunks, SC_SCATTER_ROWS)

    @pl.kernel(out_type=jax.ShapeDtypeStruct((n_rows, d), x.dtype), mesh=_sc_mesh(),
               scratch_types=[pltpu.VMEM((n_chunks, SC_SCATTER_ROWS), jnp.int32),
                              pltpu.VMEM((SC_SCATTER_ROWS, d), x.dtype),
                              pltpu.SemaphoreType.DMA])
    def scatter_kernel(x_hbm, i_hbm, o_hbm, idx_v, rows_v, sem):
        wid = _sc_worker()
        pltpu.sync_copy(i_hbm.at[wid], idx_v)

        @pl.loop(0, n_chunks)
        def _(j):
            src = pl.multiple_of(((wid * n_chunks + j) * SC_SCATTER_ROWS) % t, SC_SCATTER_ROWS)
            pltpu.sync_copy(x_hbm.at[pl.ds(src, SC_SCATTER_ROWS)], rows_v)
            pltpu.async_copy(rows_v, o_hbm.at[idx_v.at[j]], sem).wait()

    return scatter_kernel(x, idx3)


def _sc_gather_rows(x, idx):
    d = x.shape[1]
    n_idx = idx.shape[0]
    workers = SC_CORES * SC_SUBCORES
    per_worker = n_idx // workers
    n_chunks = per_worker // SC_GATHER_ROWS
    assert n_chunks * workers * SC_GATHER_ROWS == n_idx

    assert n_chunks % 2 == 0

    @pl.kernel(out_type=jax.ShapeDtypeStruct((n_idx, d), x.dtype), mesh=_sc_mesh(),
               scratch_types=[pltpu.VMEM((per_worker,), jnp.int32),
                              pltpu.VMEM((SC_GATHER_ROWS, d), x.dtype),
                              pltpu.VMEM((SC_GATHER_ROWS, d), x.dtype),
                              pltpu.SemaphoreType.DMA, pltpu.SemaphoreType.DMA])
    def gather_kernel(x_hbm, i_hbm, o_hbm, idx_v, buf0, buf1, sem0, sem1):
        base = _sc_worker() * per_worker
        pltpu.sync_copy(i_hbm.at[pl.ds(base, per_worker)], idx_v)
        bufs, sems = (buf0, buf1), (sem0, sem1)

        def gather(chunk, b):
            off = pl.multiple_of(chunk * SC_GATHER_ROWS, SC_GATHER_ROWS)
            return pltpu.make_async_copy(x_hbm.at[idx_v.at[pl.ds(off, SC_GATHER_ROWS)]], bufs[b], sems[b])

        gather(0, 0).start()

        @pl.loop(0, n_chunks, step=2)
        def _(j):
            for b in range(2):
                chunk = j + b
                gather(chunk, b).wait()
                if b == 0:
                    gather(chunk + 1, 1).start()
                else:
                    @pl.when(chunk + 1 < n_chunks)
                    def _():
                        gather(chunk + 1, 0).start()
                off = pl.multiple_of(chunk * SC_GATHER_ROWS, SC_GATHER_ROWS)
                pltpu.sync_copy(bufs[b], o_hbm.at[pl.ds(base + off, SC_GATHER_ROWS)])

    return gather_kernel(x, idx)


def _combine(y4_ref, route_ref):
    route = route_ref[0]
    y = None
    for k in range(TOP_K):
        t = route[:, TOP_K + k:TOP_K + k + 1] * _unpack_rows(y4_ref[k, 0]).astype(F32)
        y = t if y is None else y + t
    return y


def _final_kernel(x_ref, y4_ref, route_ref, mod_ref, g_ref, o_ref):
    x = x_ref[0] + mod_ref[0, 0][5:6] * _combine(y4_ref, route_ref)
    o_ref[0] = _rms(x) * g_ref[...]


def _final(li, x_all, y4, route, modsel, skip, final_g):
    b, l, d = x_all.shape
    off = skip // TM
    lo = l - skip
    return pl.pallas_call(
        _final_kernel, out_shape=jax.ShapeDtypeStruct((b, lo, d), F32), grid=(b, lo // TM),
        in_specs=[pl.BlockSpec((1, TM, d), lambda i, j: (i, j + off, 0)),
                  pl.BlockSpec((TOP_K, 1, TM, y4.shape[3]), lambda i, j: (0, i, j + off, 0)),
                  pl.BlockSpec((1, TM, LANES), lambda i, j: (i, j + off, 0)),
                  _mod_spec(li, off)(d),
                  pl.BlockSpec((1, d), lambda i, j: (0, 0))],
        out_specs=pl.BlockSpec((1, TM, d), lambda i, j: (i, j, 0)),
        compiler_params=_params(2), name="final_norm",
    )(x_all, y4, route, modsel, final_g)


def _rope_tables(n, ctx, rot_dim, group_pattern):
    t = jnp.arange(n, dtype=jnp.int32)
    row = (t // GRID_W).astype(F32)
    col = (t % GRID_W).astype(F32)
    per_axis = rot_dim // 2
    inv = ROPE_BASE ** (-jnp.arange(0, per_axis, 2, dtype=F32) / per_axis)
    ang = jnp.concatenate([row[:, None] * inv[None], col[:, None] * inv[None]], axis=-1)
    cos, sin = jnp.cos(ang), jnp.sin(ang)
    half = rot_dim // 2
    c = jnp.ones((n, LANES), F32)
    sdn = jnp.zeros((n, LANES), F32)
    sup = jnp.zeros((n, LANES), F32)
    for off in group_pattern:
        c = c.at[:, off:off + half].set(cos).at[:, off + half:off + rot_dim].set(cos)
        sdn = sdn.at[:, off:off + half].set(-sin)
        sup = sup.at[:, off + half:off + rot_dim].set(sin)
    tab = jnp.stack([c, sdn, sup])
    ident = jnp.stack([jnp.ones((ctx, LANES), F32), jnp.zeros((ctx, LANES), F32), jnp.zeros((ctx, LANES), F32)])
    return jnp.concatenate([ident, tab], axis=1)


def _wprep_kernel(w_ref, o_ref):
    x = w_ref[0]
    rows = x.shape[0]
    o = 3 * 512 + MLA_Q_LORA + MLA_KV_LORA
    zeros = lambda n: jnp.zeros((rows, n), F32)
    kr = jnp.concatenate([zeros(MLA_NOPE), x[:, o:o + MLA_ROPE], zeros(LANES - MLA_NOPE - MLA_ROPE)], axis=1)
    o += MLA_ROPE
    cq = x[:, o:o + 512]
    k0, k1 = x[:, o + 512:o + 576], x[:, o + 576:o + 640]
    v0, v1 = x[:, o + 640:o + 704], x[:, o + 704:o + 768]
    g = x[:, o + 768:]
    o_ref[0] = jnp.concatenate([x[:, :3 * 512 + MLA_Q_LORA + MLA_KV_LORA], kr, cq, k0, k0, k1, k1, v0, v0, v1, v1, g],
                               axis=1).astype(BF16)


def _wprep(w_in):
    depth, d, cols = w_in.shape
    out_cols = cols - MLA_ROPE + LANES + 2 * 128
    tr = 128
    return pl.pallas_call(
        _wprep_kernel,
        out_shape=jax.ShapeDtypeStruct((depth, d, out_cols), BF16),
        grid=(depth, d // tr),
        in_specs=[pl.BlockSpec((1, tr, cols), lambda l, i: (l, i, 0))],
        out_specs=pl.BlockSpec((1, tr, out_cols), lambda l, i: (l, i, 0)),
        compiler_params=_params(2),
        name="wprep",
    )(w_in)


def _prep_weights(w_in, mla_w_uq, mla_w_ukv):
    depth = w_in.shape[0]
    w = _wprep(w_in)
    uq = mla_w_uq.reshape(depth, MLA_Q_LORA, MLA_HEADS, MLA_NOPE + MLA_ROPE)
    uq = jnp.pad(uq, ((0, 0), (0, 0), (0, 0), (0, LANES - MLA_NOPE - MLA_ROPE)))
    uq = uq.reshape(depth, MLA_Q_LORA, MLA_HEADS * LANES).astype(BF16)
    ukv = mla_w_ukv.reshape(depth, MLA_KV_LORA, MLA_HEADS, MLA_NOPE + MLA_V)
    uk = jnp.pad(ukv[..., :MLA_NOPE], ((0, 0), (0, 0), (0, 0), (0, LANES - MLA_NOPE)))
    uk = uk.reshape(depth, MLA_KV_LORA, MLA_HEADS * LANES)
    uv = ukv[..., MLA_NOPE:].reshape(depth, MLA_KV_LORA, MLA_HEADS * MLA_V)
    ukv = jnp.concatenate([uk, uv], axis=-1).astype(BF16)
    return w, uq, ukv


def kernel(x, c, ctx, c_ctx, norm1_g, norm2_g, w_mod, b_mod, w_in, na_rpb, mla_q_norm_g, mla_kv_norm_g, mla_w_uq, mla_w_ukv, swa_sink, w_branch, w_out, router_w, router_b, expert_w_gate_up, expert_b_gate_up, expert_w_down, expert_b_down, final_norm_g):
    b, n, d = x.shape
    lc = ctx.shape[1]
    l = lc + n
    depth = w_in.shape[0]
    n_exp = router_w.shape[2]
    assert lc == TM and n % TM == 0 and TM == NA_QROWS * GRID_W

    cvec = jnp.zeros((8, d), F32).at[:b].set(c).at[b].set(c_ctx)
    mod = _modulation(cvec, w_mod, b_mod)
    mod_lat = mod[:, :b].reshape(depth, b, 1, 6, d)
    mod_ctx = jnp.broadcast_to(mod[:, b].reshape(depth, 1, 1, 6, d), (depth, b, 1, 6, d))
    modsel = jnp.concatenate([mod_ctx, mod_lat], axis=2)

    w_all, uq_all, ukv_all = _prep_weights(w_in, mla_w_uq, mla_w_ukv)
    wbr_all = w_branch.astype(BF16)
    wout_all = w_out.astype(BF16)
    rw_all = jnp.pad(router_w, ((0, 0), (0, 0), (0, LANES - n_exp))).astype(BF16)
    rb_all = jnp.pad(router_b, ((0, 0), (0, LANES - n_exp))).reshape(depth, 1, LANES)
    rope_b = _rope_tables(n, lc, MLA_ROPE, (MLA_NOPE,))
    rope_c = _rope_tables(n, lc, SWA_HEAD_DIM, (0, SWA_HEAD_DIM))
    bias_all = _na_bias_tables(na_rpb, n // GRID_W)

    g1_all = norm1_g.reshape(depth, 1, d)
    g2_all = norm2_g.reshape(depth, 1, d)
    gq_all = mla_q_norm_g.reshape(depth, 1, -1)
    gkv_all = mla_kv_norm_g.reshape(depth, 1, -1)

    x_all = jnp.concatenate([ctx, x], axis=1)
    moe = None
    for li in range(depth):
        ms = modsel
        outs = _inproj(li, x_all, moe, ms, g1_all, w_all, uq_all, ukv_all, gq_all, gkv_all, rope_b, rope_c)
        if moe is not None:
            x_all, outs = outs[0], outs[1:]
        aq, ak, av, bq, bk, bv, cq, ck, cv, gate = outs
        oa = _na_attention(li, aq, ak, av, bias_all, lc)
        ob = _mla_attention(bq, bk, bv, lc)
        oc = _swa_attention(swa_sink[li], cq, ck, cv, lc)
        x_all, h2, route, cnt = _merge(li, oa, ob, oc, gate, x_all, ms, wbr_all, wout_all,
                                       g2_all, rw_all, rb_all, n_exp)
        dest, blk_e, nxt_e, n_used, n_rows = _layout(route, cnt, n_exp)
        dest = dest.reshape(TOP_K * b * l)
        xs = _sc_scatter_rows(h2.reshape(b * l, d // 2), dest, n_rows)
        ys = _moe_experts(li, blk_e, nxt_e, n_used, xs, expert_w_gate_up, expert_b_gate_up,
                          expert_w_down, expert_b_down)
        moe = (_sc_gather_rows(ys, dest).reshape(TOP_K, b, l, d // 2), route)
    return _final(depth - 1, x_all, moe[0], moe[1], modsel, lc, final_norm_g.reshape(1, d))
```

```python
import functools

import numpy as np
import jax
import jax.numpy as jnp
from jax import lax
from jax.experimental import pallas as pl
from jax.experimental.pallas import tpu as pltpu
from jax.experimental.pallas import tpu_sc as plsc

GRID_W = 64
EPS = 1e-6
ROPE_BASE = 10000.0
NEG_INF = -1e30
LOG2E = 1.4426950408889634
LANES = 128

NA_HEADS = 8
NA_HEAD_DIM = 64
NA_KH = 8
NA_KW = 16
NA_QROWS = 4
NA_SLAB = 12
MLA_HEADS = 8
MLA_NOPE = 64
MLA_ROPE = 32
MLA_V = 64
MLA_KEY_PARTS = 4
MLA_Q_LORA = 256
MLA_KV_LORA = 128
SWA_HEADS = 8
SWA_KV_HEADS = 2
SWA_HEAD_DIM = 64
SWA_WINDOW = 128
N_BRANCH = 3
BRANCH_W = 512
N_EXPERTS = 32
TOP_K = 4
SWIGLU_ALPHA = 1.702
SWIGLU_LIMIT = 7.0

TM = 256
MOE_TM = 512
SC_CORES = 2
SC_SUBCORES = 16
SC_SCATTER_ROWS = 128
SC_GATHER_ROWS = 64
VMEM_LIMIT = 56 * 1024 * 1024

BF16 = jnp.bfloat16
F32 = jnp.float32


def _dot(a, b):
    return jnp.dot(a, b, preferred_element_type=F32)


def _dot_nt(a, b):
    return lax.dot_general(a, b, (((1,), (1,)), ((), ())), preferred_element_type=F32)


def _params(n_axes, vmem=VMEM_LIMIT):
    return pltpu.CompilerParams(dimension_semantics=("arbitrary",) * n_axes, vmem_limit_bytes=vmem)


def _layer_spec(a, li):
    return pl.BlockSpec((None,) + a.shape[1:], lambda *_: (li,) + (0,) * (a.ndim - 1))


def _mod_spec(li, off=0):
    return lambda d: pl.BlockSpec((None, 1, 1, 6, d), lambda i, j: (li, i, jnp.minimum(j + off, 1), 0, 0))


def _rms(x):
    return x * lax.rsqrt(jnp.mean(x * x, axis=-1, keepdims=True) + EPS)


def _sigmoid(x):
    return 1.0 / (1.0 + jnp.exp(-x))


def _pack_rows(xb):
    half = xb.shape[1] // 2
    lo = pltpu.bitcast(xb[:, :half].astype(F32), jnp.int32)
    hi = pltpu.bitcast(xb[:, half:].astype(F32), jnp.int32)
    return (hi & jnp.int32(-65536)) | lax.shift_right_logical(lo, 16)


def _unpack_rows(w):
    lo = pltpu.bitcast(lax.shift_left(w, 16), F32).astype(BF16)
    hi = pltpu.bitcast(w & jnp.int32(-65536), F32).astype(BF16)
    return jnp.concatenate([lo, hi], axis=1)


def _mod_kernel(c_ref, w_ref, b_ref, o_ref):
    c = c_ref[...]
    s = (c * _sigmoid(c)).astype(BF16)
    o_ref[0] = _dot(s, w_ref[0].astype(BF16)) + b_ref[0]


def _modulation(cvec, w_mod, b_mod):
    depth, d, n6 = w_mod.shape
    tn = n6 // 4
    return pl.pallas_call(
        _mod_kernel,
        out_shape=jax.ShapeDtypeStruct((depth, 8, n6), F32),
        grid=(depth, n6 // tn),
        in_specs=[pl.BlockSpec((8, d), lambda l, j: (0, 0)),
                  pl.BlockSpec((1, d, tn), lambda l, j: (l, 0, j)),
                  pl.BlockSpec((1, 1, tn), lambda l, j: (l, 0, j))],
        out_specs=pl.BlockSpec((1, 8, tn), lambda l, j: (l, 0, j)),
        compiler_params=_params(2),
        name="modulation",
    )(cvec, w_mod, b_mod.reshape(depth, 1, n6))


def _rope_groups(x, tab_ref, shift):
    cos, sdn, sup = tab_ref[0], tab_ref[1], tab_ref[2]
    outs = []
    for g in range(x.shape[1] // LANES):
        xg = x[:, g * LANES:(g + 1) * LANES]
        outs.append(xg * cos + pltpu.roll(xg, LANES - shift, 1) * sdn + pltpu.roll(xg, shift, 1) * sup)
    return outs[0] if len(outs) == 1 else jnp.concatenate(outs, axis=1)


def _inproj_kernel(*refs, with_moe):
    if with_moe:
        x_ref, y4_ref, route_ref, modp_ref = refs[:4]
        refs = refs[4:]
        (mod_ref, g1_ref, w_ref, wuq_ref, wukv_ref, gq_ref, gkv_ref, rb_ref, rc_ref, xo_ref,
         aq_ref, ak_ref, av_ref, bq_ref, bk_ref, bv_ref, cq_ref, ck_ref, cv_ref, gate_ref) = refs
        x = x_ref[0] + modp_ref[0, 0][5:6] * _combine(y4_ref, route_ref)
        xo_ref[0] = x
    else:
        (x_ref, mod_ref, g1_ref, w_ref, wuq_ref, wukv_ref, gq_ref, gkv_ref, rb_ref, rc_ref,
         aq_ref, ak_ref, av_ref, bq_ref, bk_ref, bv_ref, cq_ref, ck_ref, cv_ref, gate_ref) = refs
        x = x_ref[0]
    mod = mod_ref[0, 0]
    h = (_rms(x) * g1_ref[...]) * (1.0 + mod[1:2]) + mod[0:1]
    hb = h.astype(BF16)
    acc = _dot(hb, w_ref[:, 0:1536])
    aq_ref[0] = (acc[:, 0:512] * (NA_HEAD_DIM ** -0.5 * LOG2E)).astype(BF16)
    ak_ref[0] = acc[:, 512:1024].astype(BF16)
    av_ref[0] = acc[:, 1024:1536].T.astype(BF16)
    acc = _dot(hb, w_ref[:, 1536:2048])
    qn = (_rms(acc[:, 0:256]) * gq_ref[...]).astype(BF16)
    kvn = (_rms(acc[:, 256:384]) * gkv_ref[...]).astype(BF16)
    kr = _rope_groups(acc[:, 384:512], rb_ref, MLA_ROPE // 2)
    q = _rope_groups(_dot(qn, wuq_ref[...]), rb_ref, MLA_ROPE // 2)
    bq_ref[0] = (q * ((MLA_NOPE + MLA_ROPE) ** -0.5 * LOG2E)).astype(BF16)
    kv = _dot(kvn, wukv_ref[...])
    bk_ref[0] = (kv[:, 0:1024] + jnp.concatenate([kr] * MLA_HEADS, axis=1)).astype(BF16)
    bv_ref[0] = kv[:, 1024:1536].T.astype(BF16)
    acc = _dot(hb, w_ref[:, 2048:3072])
    cq_ref[0] = (_rope_groups(acc[:, 0:512], rc_ref, SWA_HEAD_DIM // 2) * (SWA_HEAD_DIM ** -0.5 * LOG2E)).astype(BF16)
    ck_ref[0] = _rope_groups(acc[:, 512:768], rc_ref, SWA_HEAD_DIM // 2).astype(BF16)
    cv = jnp.concatenate([acc[:, 768:832], acc[:, 896:960]], axis=1)
    cv_ref[0] = cv.T.astype(BF16)
    gate_ref[0] = _sigmoid(_dot(hb, w_ref[:, 3072:])).astype(BF16)


def _inproj(li, x_all, moe, modsel, g1, w, wuq, wukv, gq, gkv, rope_b, rope_c):
    b, l, d = x_all.shape
    nb = l // TM
    row = lambda width: pl.BlockSpec((1, TM, width), lambda i, j: (i, j, 0))
    full = lambda a: _layer_spec(a, li)
    widths = (512, 512, 512, 1024, 1024, 512, 512, 256, SWA_KV_HEADS * SWA_HEAD_DIM, N_BRANCH * d)
    in_specs = [row(d)]
    out_specs = [row(wd) for wd in widths]
    out_shape = [jax.ShapeDtypeStruct((b, l, wd), BF16) for wd in widths]
    for pos in (2, 5, 8):
        out_specs[pos] = pl.BlockSpec((1, widths[pos], TM), lambda i, j: (i, 0, j))
        out_shape[pos] = jax.ShapeDtypeStruct((b, widths[pos], l), BF16)
    args = [x_all]
    if moe is not None:
        y4, route = moe
        in_specs += [pl.BlockSpec((TOP_K, 1, TM, y4.shape[3]), lambda i, j: (0, i, j, 0)), row(LANES),
                     _mod_spec(li - 1)(d)]
        out_specs = [row(d)] + out_specs
        out_shape = [jax.ShapeDtypeStruct((b, l, d), F32)] + out_shape
        args += [y4, route, modsel]
    in_specs += [_mod_spec(li)(d), full(g1), full(w), full(wuq), full(wukv), full(gq), full(gkv),
                 pl.BlockSpec((3, TM, LANES), lambda i, j: (0, j, 0)),
                 pl.BlockSpec((3, TM, LANES), lambda i, j: (0, j, 0))]
    args += [modsel, g1, w, wuq, wukv, gq, gkv, rope_b, rope_c]
    return pl.pallas_call(
        functools.partial(_inproj_kernel, with_moe=moe is not None),
        out_shape=out_shape,
        grid=(b, nb),
        in_specs=in_specs,
        out_specs=out_specs,
        compiler_params=_params(2),
        name="inproj",
    )(*args)


def _lane_lo():
    return lax.broadcasted_iota(jnp.int32, (1, LANES), 1) < (LANES // 2)


def _split_heads(qp, lo):
    zero = jnp.zeros_like(qp)
    return jnp.where(lo, qp, zero), jnp.where(lo, zero, qp)


def _softmax_pv_t(score_parts, vt_parts, extra_logit=None):
    m = score_parts[0].max(axis=0, keepdims=True)
    for s in score_parts[1:]:
        m = jnp.maximum(m, s.max(axis=0, keepdims=True))
    if extra_logit is not None:
        m = jnp.maximum(m, extra_logit)
    den = None
    acc = None
    for s, vt in zip(score_parts, vt_parts):
        e = jnp.exp2(s - m)
        d = e.sum(axis=0, keepdims=True)
        den = d if den is None else den + d
        o = _dot(vt, e.astype(BF16))
        acc = o if acc is None else acc + o
    if extra_logit is not None:
        den = den + jnp.exp2(extra_logit - m)
    return acc / den


def _heads_ahead(n_heads, ahead, scores, finish):
    pending = [scores(h) for h in range(ahead)]
    outs = []
    for h in range(n_heads):
        if h + ahead < n_heads:
            pending.append(scores(h + ahead))
        outs.append(finish(h, pending.pop(0)))
    return outs


def _na_kernel(q_ref, k_ref, vt_ref, bias_ref, o_ref, *, ctx, rows):
    j = pl.program_id(1)
    lo = _lane_lo()

    def head_q(h):
        sl = slice((h // 2) * LANES, (h // 2 + 1) * LANES)
        return _split_heads(q_ref[0, :, sl], lo)[h % 2], sl

    def vt(h, cols):
        return vt_ref[0, h * NA_HEAD_DIM:(h + 1) * NA_HEAD_DIM, cols]

    @pl.when(j == 0)
    def _():
        def scores(h):
            qm, sl = head_q(h)
            return [_dot_nt(k_ref[0, 0:ctx, sl], qm)]

        outs = _heads_ahead(NA_HEADS, 2, scores, lambda h, s: _softmax_pv_t(s, [vt(h, slice(0, ctx))]))
        o_ref[0] = jnp.concatenate(outs, axis=0).T.astype(BF16)

    @pl.when(j > 0)
    def _():
        r = (j - 1) * NA_QROWS
        s0 = jnp.clip(r - NA_KH // 2, 0, rows - NA_SLAB)
        slab = pl.ds(pl.multiple_of(ctx + s0 * GRID_W, NA_QROWS * GRID_W), NA_SLAB * GRID_W)

        def scores(h):
            qm, sl = head_q(h)
            return [_dot_nt(k_ref[0, 0:ctx, sl], qm), _dot_nt(k_ref[0, slab, sl], qm) + bias_ref[0, h]]

        outs = _heads_ahead(NA_HEADS, 2, scores,
                            lambda h, s: _softmax_pv_t(s, [vt(h, slice(0, ctx)), vt(h, slab)]))
        o_ref[0] = jnp.concatenate(outs, axis=0).T.astype(BF16)


def _na_attention(li, aq, ak, avt, bias, ctx):
    b, l, w = aq.shape
    assert ((l - ctx) // GRID_W) % NA_QROWS == 0 and ctx % (NA_QROWS * GRID_W) == 0
    nb = l // TM
    rows = (l - ctx) // GRID_W
    last = nb - 1

    def bias_map(i, j):
        return (li, jnp.where(j <= 1, 0, jnp.where(j == last, 2, 1)), 0, 0, 0)

    return pl.pallas_call(
        functools.partial(_na_kernel, ctx=ctx, rows=rows),
        out_shape=jax.ShapeDtypeStruct((b, l, w), BF16),
        grid=(b, nb),
        in_specs=[pl.BlockSpec((1, TM, w), lambda i, j: (i, j, 0)),
                  pl.BlockSpec((1, l, w), lambda i, j: (i, 0, 0)),
                  pl.BlockSpec((1, w, l), lambda i, j: (i, 0, 0)),
                  pl.BlockSpec((None, 1, NA_HEADS, NA_SLAB * GRID_W, TM), bias_map)],
        out_specs=pl.BlockSpec((1, TM, w), lambda i, j: (i, j, 0)),
        compiler_params=_params(2),
        name="na_attention",
    )(aq, ak, avt, bias)


def _na_bias_tables(na_rpb, rows):
    depth = na_rpb.shape[0]
    qc = np.arange(GRID_W)[:, None]
    kc = np.arange(GRID_W)[None, :]
    c0 = np.clip(qc - NA_KW // 2, 0, GRID_W - NA_KW)
    col_ok = (kc >= c0) & (kc < c0 + NA_KW)
    col_idx = np.clip(kc - qc + NA_KW - 1, 0, 2 * NA_KW - 2)
    bc = jnp.take(na_rpb, jnp.asarray(col_idx.reshape(-1)), axis=3).reshape(
        depth, NA_HEADS, 2 * NA_KH - 1, GRID_W, GRID_W)
    i_idx = np.zeros((3, NA_QROWS, NA_SLAB), np.int32)
    ok = np.zeros((3, NA_QROWS, NA_SLAB, GRID_W, GRID_W), bool)
    for case, (r, s) in enumerate(((0, 0), (NA_KH // 2, 0), (rows - NA_QROWS, rows - NA_SLAB))):
        for a in range(NA_QROWS):
            qr = r + a
            r0 = min(max(qr - NA_KH // 2, 0), rows - NA_KH)
            for c in range(NA_SLAB):
                kr = s + c
                inside = r0 <= kr < r0 + NA_KH
                i_idx[case, a, c] = min(max(kr - qr + NA_KH - 1, 0), 2 * NA_KH - 2)
                ok[case, a, c] = col_ok & inside
    t = jnp.take(bc, jnp.asarray(i_idx.reshape(-1)), axis=2).reshape(
        depth, NA_HEADS, 3, NA_QROWS, NA_SLAB, GRID_W, GRID_W)
    t = jnp.where(jnp.asarray(ok)[None, None], t * LOG2E, NEG_INF)
    t = t.transpose(0, 2, 1, 4, 6, 3, 5)
    return t.reshape(depth, 3, NA_HEADS, NA_SLAB * GRID_W, NA_QROWS * GRID_W)


def _mla_kernel(q_ref, k_ref, vt_ref, o_ref, *, ctx):
    j = pl.program_id(1)

    def run(nkeys):
        tiles = nkeys // 256
        nparts = min(MLA_KEY_PARTS, tiles)
        cuts = [(tiles * p // nparts) * 256 for p in range(nparts + 1)]
        parts = list(zip(cuts[:-1], cuts[1:]))

        def scores(h):
            hsl = slice(h * LANES, (h + 1) * LANES)
            return [_dot_nt(k_ref[0, a:b, hsl], q_ref[0, :, hsl]) for a, b in parts]

        def finish(h, s_parts):
            return _softmax_pv_t(s_parts, [vt_ref[0, h * MLA_V:(h + 1) * MLA_V, a:b] for a, b in parts])

        outs = _heads_ahead(MLA_HEADS, 3, scores, finish)
        o_ref[0] = jnp.concatenate(outs, axis=0).T.astype(BF16)

    @pl.when(j == 0)
    def _():
        run(ctx)

    @pl.when(j > 0)
    def _():
        run(k_ref.shape[1])


def _mla_attention(bq, bk, bvt, ctx):
    b, l, wq = bq.shape
    wv = bvt.shape[1]
    return pl.pallas_call(
        functools.partial(_mla_kernel, ctx=ctx),
        out_shape=jax.ShapeDtypeStruct((b, l, wv), BF16),
        grid=(b, l // TM),
        in_specs=[pl.BlockSpec((1, TM, wq), lambda i, j: (i, j, 0)),
                  pl.BlockSpec((1, l, wq), lambda i, j: (i, 0, 0)),
                  pl.BlockSpec((1, wv, l), lambda i, j: (i, 0, 0))],
        out_specs=pl.BlockSpec((1, TM, wv), lambda i, j: (i, j, 0)),
        compiler_params=_params(2),
        name="mla_attention",
    )(bq, bk, bvt)


def _swa_kernel(sink_ref, q_ref, k_ref, vt_ref, o_ref, *, ctx, n_lat):
    j = pl.program_id(1)
    lo = _lane_lo()
    group = SWA_HEADS // SWA_KV_HEADS
    band = TM + 2 * SWA_WINDOW

    def head_q(h):
        sl = slice((h // 2) * LANES, (h // 2 + 1) * LANES)
        return _split_heads(q_ref[0, :, sl], lo)[h % 2]

    def kv_lanes(h):
        return slice((h // group) * LANES, (h // group + 1) * LANES)

    def vt(h, cols):
        kv = h // group
        return vt_ref[0, kv * SWA_HEAD_DIM:(kv + 1) * SWA_HEAD_DIM, cols]

    def sink(h):
        return jnp.full((1, 1), sink_ref[h] * LOG2E, F32)

    @pl.when(j == 0)
    def _():
        outs = _heads_ahead(
            SWA_HEADS, 2, lambda h: [_dot_nt(k_ref[0, 0:ctx, kv_lanes(h)], head_q(h))],
            lambda h, s: _softmax_pv_t(s, [vt(h, slice(0, ctx))], extra_logit=sink(h)))
        o_ref[0] = jnp.concatenate(outs, axis=0).T.astype(BF16)

    @pl.when(j > 0)
    def _():
        q0 = (j - 1) * TM
        s0 = jnp.clip(q0 - SWA_WINDOW, 0, n_lat - band)
        rows = pl.ds(pl.multiple_of(ctx + s0, SWA_WINDOW), band)
        kpos = s0 + lax.broadcasted_iota(jnp.int32, (band, TM), 0)
        qpos = q0 + lax.broadcasted_iota(jnp.int32, (band, TM), 1)
        keep = jnp.abs(qpos - kpos) <= SWA_WINDOW

        def scores(h):
            qm = head_q(h)
            return [_dot_nt(k_ref[0, 0:ctx, kv_lanes(h)], qm),
                    jnp.where(keep, _dot_nt(k_ref[0, rows, kv_lanes(h)], qm), NEG_INF)]

        outs = _heads_ahead(
            SWA_HEADS, 2, scores,
            lambda h, s: _softmax_pv_t(s, [vt(h, slice(0, ctx)), vt(h, rows)], extra_logit=sink(h)))
        o_ref[0] = jnp.concatenate(outs, axis=0).T.astype(BF16)


def _swa_attention(sink, cq, ck, cvt, ctx):
    b, l, w = cq.shape
    wk = ck.shape[2]
    wv = cvt.shape[1]
    grid_spec = pltpu.PrefetchScalarGridSpec(
        num_scalar_prefetch=1,
        grid=(b, l // TM),
        in_specs=[pl.BlockSpec((1, TM, w), lambda i, j, s: (i, j, 0)),
                  pl.BlockSpec((1, l, wk), lambda i, j, s: (i, 0, 0)),
                  pl.BlockSpec((1, wv, l), lambda i, j, s: (i, 0, 0))],
        out_specs=pl.BlockSpec((1, TM, w), lambda i, j, s: (i, j, 0)),
    )
    return pl.pallas_call(
        functools.partial(_swa_kernel, ctx=ctx, n_lat=l - ctx),
        out_shape=jax.ShapeDtypeStruct((b, l, w), BF16),
        grid_spec=grid_spec,
        compiler_params=_params(2),
        name="swa_attention",
    )(sink, cq, ck, cvt)


def _merge_kernel(oa_ref, ob_ref, oc_ref, gate_ref, x_ref, mod_ref, wbr_ref, wout_ref, g2_ref,
                  rw_ref, rb_ref, xo_ref, h2_ref, route_ref, cnt_ref, run_ref, *, n_exp):
    d = x_ref.shape[2]

    @pl.when(jnp.logical_and(pl.program_id(0) == 0, pl.program_id(1) == 0))
    def _():
        run_ref[...] = jnp.zeros_like(run_ref)

    mod = mod_ref[0, 0]
    mix = None
    for i, o_ref in enumerate((oa_ref, ob_ref, oc_ref)):
        t = gate_ref[0, :, i * d:(i + 1) * d].astype(F32) * _dot(o_ref[0], wbr_ref[i])
        mix = t if mix is None else mix + t
    y = _dot(mix.astype(BF16), wout_ref[...])
    x = x_ref[0] + mod[2:3] * y
    xo_ref[0] = x
    h2 = (_rms(x) * g2_ref[...]) * (1.0 + mod[4:5]) + mod[3:4]
    h2b = h2.astype(BF16)
    h2_ref[0] = _pack_rows(h2b)
    logits = _dot(h2b, rw_ref[...]) + rb_ref[...]
    lane = lax.broadcasted_iota(jnp.int32, logits.shape, 1).astype(F32)
    work = jnp.where(lane < n_exp, logits, -jnp.inf)
    ids, vals = [], []
    for _ in range(TOP_K):
        m = work.max(axis=-1, keepdims=True)
        idx = jnp.where(work == m, lane, float(LANES)).min(axis=-1, keepdims=True)
        ids.append(idx)
        vals.append(m)
        work = jnp.where(lane == idx, -jnp.inf, work)
    ex = [jnp.exp(v - vals[0]) for v in vals]
    den = ex[0] + ex[1] + ex[2] + ex[3]
    hits = jnp.zeros(logits.shape, F32)
    for idx in ids:
        hits = hits + jnp.where(lane == idx, 1.0, 0.0)
    r = lax.broadcasted_iota(jnp.int32, (TM, TM), 0)
    c = lax.broadcasted_iota(jnp.int32, (TM, TM), 1)
    tri = jnp.where(c < r, 1.0, 0.0).astype(BF16)
    before = _dot(tri, hits.astype(BF16)) + run_ref[0:1]
    route = jnp.zeros(logits.shape, F32)
    for k in range(TOP_K):
        rank = jnp.where(lane == ids[k], before, 0.0).sum(axis=-1, keepdims=True)
        route = jnp.where(lane == k, ids[k], route)
        route = jnp.where(lane == TOP_K + k, ex[k] / den, route)
        route = jnp.where(lane == 2 * TOP_K + k, rank, route)
    route_ref[0] = route
    run_ref[...] = run_ref[...] + hits.sum(axis=0, keepdims=True)
    cnt_ref[...] = run_ref[...]


def _merge(li, oa, ob, oc, gate, x_all, modsel, wbr, wout, g2, rw, rb, n_exp):
    b, l, d = x_all.shape
    row = lambda width: pl.BlockSpec((1, TM, width), lambda i, j: (i, j, 0))
    full = lambda a: _layer_spec(a, li)
    return pl.pallas_call(
        functools.partial(_merge_kernel, n_exp=n_exp),
        out_shape=[jax.ShapeDtypeStruct((b, l, d), F32),
                   jax.ShapeDtypeStruct((b, l, d // 2), jnp.int32),
                   jax.ShapeDtypeStruct((b, l, LANES), F32),
                   jax.ShapeDtypeStruct((8, LANES), F32)],
        grid=(b, l // TM),
        in_specs=[row(BRANCH_W), row(BRANCH_W), row(BRANCH_W), row(N_BRANCH * d), row(d), _mod_spec(li)(d),
                  full(wbr), full(wout), full(g2), full(rw), full(rb)],
        out_specs=[row(d), row(d // 2), row(LANES), pl.BlockSpec((8, LANES), lambda i, j: (0, 0))],
        scratch_shapes=[pltpu.VMEM((8, LANES), F32)],
        compiler_params=_params(2),
        name="merge",
    )(oa, ob, oc, gate, x_all, modsel, wbr, wout, g2, rw, rb)


def _moe_kernel(be_ref, nx_ref, nu_ref, x_ref, wgu_hbm, bgu_ref, wdn_hbm, bdn_ref, y_ref,
                gu_stage, dn_stage, wgu_s, wdn_s, sem, *, li):
    i = pl.program_id(0)
    ff = wdn_s.shape[0]

    def fetch(e):
        return (pltpu.make_async_copy(wgu_hbm.at[li, e], gu_stage, sem.at[0]),
                pltpu.make_async_copy(wdn_hbm.at[li, e], dn_stage, sem.at[1]))

    @pl.when(i < nu_ref[0])
    def _():
        e = be_ref[i]

        @pl.when(i == 0)
        def _():
            for copy in fetch(e):
                copy.start()

        @pl.when(jnp.logical_or(i == 0, e != be_ref[jnp.maximum(i - 1, 0)]))
        def _():
            for copy in fetch(e):
                copy.wait()
            wgu_s[...] = gu_stage[...].astype(BF16)
            wdn_s[...] = dn_stage[...].astype(BF16)
            nxt = nx_ref[i]

            @pl.when(nxt >= 0)
            def _():
                for copy in fetch(nxt):
                    copy.start()

        gu = _dot(_unpack_rows(x_ref[...]), wgu_s[...]) + bgu_ref[0]
        glu = jnp.minimum(gu[:, :ff], SWIGLU_LIMIT)
        lin = jnp.clip(gu[:, ff:], -SWIGLU_LIMIT, SWIGLU_LIMIT)
        act = glu * _sigmoid(SWIGLU_ALPHA * glu) * (lin + 1.0)
        y_ref[...] = _pack_rows((_dot(act.astype(BF16), wdn_s[...]) + bdn_ref[0]).astype(BF16))


def _moe_experts(li, blk_e, nxt_e, n_used, xs, w_gu, b_gu, w_dn, b_dn):
    n_rows, packed_w = xs.shape
    depth, n_exp, d, ff2 = w_gu.shape
    ff = ff2 // 2
    n_blocks = n_rows // MOE_TM

    def row_map(i, be, nx, nu):
        return (jnp.minimum(i, nu[0] - 1), 0)

    def b_map(i, be, nx, nu):
        return (li, be[jnp.minimum(i, nu[0] - 1)], 0, 0)

    grid_spec = pltpu.PrefetchScalarGridSpec(
        num_scalar_prefetch=3,
        grid=(n_blocks,),
        in_specs=[pl.BlockSpec((MOE_TM, packed_w), row_map),
                  pl.BlockSpec(memory_space=pl.ANY),
                  pl.BlockSpec((None, 1, 1, ff2), b_map),
                  pl.BlockSpec(memory_space=pl.ANY),
                  pl.BlockSpec((None, 1, 1, d), b_map)],
        out_specs=pl.BlockSpec((MOE_TM, packed_w), row_map),
        scratch_shapes=[pltpu.VMEM((d, ff2), F32), pltpu.VMEM((ff, d), F32),
                        pltpu.VMEM((d, ff2), BF16), pltpu.VMEM((ff, d), BF16),
                        pltpu.SemaphoreType.DMA((2,))],
    )
    return pl.pallas_call(
        functools.partial(_moe_kernel, li=li),
        out_shape=jax.ShapeDtypeStruct((n_rows, packed_w), jnp.int32),
        grid_spec=grid_spec,
        compiler_params=_params(1),
        name="moe_experts",
    )(blk_e, nxt_e, n_used, xs, w_gu, b_gu.reshape(depth, n_exp, 1, ff2), w_dn, b_dn.reshape(depth, n_exp, 1, d))


def _layout(route, cnt, n_exp):
    b, l, _ = route.shape
    t = b * l
    ids = route[..., 0:TOP_K].astype(jnp.int32)
    rank = route[..., 2 * TOP_K:3 * TOP_K].astype(jnp.int32)
    counts = cnt[0, :n_exp].astype(jnp.int32)
    padded = (counts + MOE_TM - 1) // MOE_TM * MOE_TM
    pad_end = jnp.cumsum(padded)
    pad_start = pad_end - padded
    onehot = ids[..., None] == jnp.arange(n_exp, dtype=jnp.int32)
    dest = jnp.sum(jnp.where(onehot, pad_start, 0), axis=-1) + rank
    dest = dest.reshape(t, TOP_K).T
    n_blocks = -(-t * TOP_K // MOE_TM) + n_exp
    blk_start = jnp.arange(n_blocks, dtype=jnp.int32) * MOE_TM
    blk_e = jnp.minimum(jnp.sum(blk_start[:, None] >= pad_end[None, :], axis=1), n_exp - 1).astype(jnp.int32)
    n_used = (pad_end[-1] // MOE_TM).astype(jnp.int32).reshape(1)
    ar = jnp.arange(n_exp, dtype=jnp.int32)
    later = jnp.logical_and(ar[None, :] > ar[:, None], (counts > 0)[None, :])
    nxt_of = jnp.min(jnp.where(later, ar[None, :], n_exp), axis=1)
    nxt_of = jnp.where(nxt_of == n_exp, -1, nxt_of)
    nxt_e = jnp.sum(jnp.where(blk_e[:, None] == ar[None, :], nxt_of[None, :], 0), axis=1).astype(jnp.int32)
    return dest, blk_e, nxt_e, n_used, n_blocks * MOE_TM


def _sc_mesh():
    return plsc.VectorSubcoreMesh(core_axis_name="c", subcore_axis_name="s",
                                  num_cores=SC_CORES, num_subcores=SC_SUBCORES)


def _sc_worker():
    return lax.axis_index("s") * SC_CORES + lax.axis_index("c")


def _sc_scatter_rows(x, idx, n_rows):
    t, d = x.shape
    n_idx = idx.shape[0]
    workers = SC_CORES * SC_SUBCORES
    n_chunks = n_idx // (workers * SC_SCATTER_ROWS)
    assert n_chunks * workers * SC_SCATTER_ROWS == n_idx and t % SC_SCATTER_ROWS == 0
    idx3 = idx.reshape(workers, n_chunks, SC_SCATTER_ROWS)

    @pl.kernel(out_type=jax.ShapeDtypeStruct((n_rows, d), x.dtype), mesh=_sc_mesh(),
               scratch_types=[pltpu.VMEM((n_chunks, SC_SCATTER_ROWS), jnp.int32),
                              pltpu.VMEM((SC_SCATTER_ROWS, d), x.dtype),
                              pltpu.SemaphoreType.DMA])
    def scatter_kernel(x_hbm, i_hbm, o_hbm, idx_v, rows_v, sem):
        wid = _sc_worker()
        pltpu.sync_copy(i_hbm.at[wid], idx_v)

        @pl.loop(0, n_chunks)
        def _(j):
            src = pl.multiple_of(((wid * n_chunks + j) * SC_SCATTER_ROWS) % t, SC_SCATTER_ROWS)
            pltpu.sync_copy(x_hbm.at[pl.ds(src, SC_SCATTER_ROWS)], rows_v)
            pltpu.async_copy(rows_v, o_hbm.at[idx_v.at[j]], sem).wait()

    return scatter_kernel(x, idx3)


def _sc_gather_rows(x, idx):
    d = x.shape[1]
    n_idx = idx.shape[0]
    workers = SC_CORES * SC_SUBCORES
    per_worker = n_idx // workers
    n_chunks = per_worker // SC_GATHER_ROWS
    assert n_chunks * workers * SC_GATHER_ROWS == n_idx

    assert n_chunks % 2 == 0

    @pl.kernel(out_type=jax.ShapeDtypeStruct((n_idx, d), x.dtype), mesh=_sc_mesh(),
               scratch_types=[pltpu.VMEM((per_worker,), jnp.int32),
                              pltpu.VMEM((SC_GATHER_ROWS, d), x.dtype),
                              pltpu.VMEM((SC_GATHER_ROWS, d), x.dtype),
                              pltpu.SemaphoreType.DMA, pltpu.SemaphoreType.DMA])
    def gather_kernel(x_hbm, i_hbm, o_hbm, idx_v, buf0, buf1, sem0, sem1):
        base = _sc_worker() * per_worker
        pltpu.sync_copy(i_hbm.at[pl.ds(base, per_worker)], idx_v)
        bufs, sems = (buf0, buf1), (sem0, sem1)

        def gather(chunk, b):
            off = pl.multiple_of(chunk * SC_GATHER_ROWS, SC_GATHER_ROWS)
            return pltpu.make_async_copy(x_hbm.at[idx_v.at[pl.ds(off, SC_GATHER_ROWS)]], bufs[b], sems[b])

        gather(0, 0).start()

        @pl.loop(0, n_chunks, step=2)
        def _(j):
            for b in range(2):
                chunk = j + b
                gather(chunk, b).wait()
                if b == 0:
                    gather(chunk + 1, 1).start()
                else:
                    @pl.when(chunk + 1 < n_chunks)
                    def _():
                        gather(chunk + 1, 0).start()
                off = pl.multiple_of(chunk * SC_GATHER_ROWS, SC_GATHER_ROWS)
                pltpu.sync_copy(bufs[b], o_hbm.at[pl.ds(base + off, SC_GATHER_ROWS)])

    return gather_kernel(x, idx)


def _combine(y4_ref, route_ref):
    route = route_ref[0]
    y = None
    for k in range(TOP_K):
        t = route[:, TOP_K + k:TOP_K + k + 1] * _unpack_rows(y4_ref[k, 0]).astype(F32)
        y = t if y is None else y + t
    return y


def _final_kernel(x_ref, y4_ref, route_ref, mod_ref, g_ref, o_ref):
    x = x_ref[0] + mod_ref[0, 0][5:6] * _combine(y4_ref, route_ref)
    o_ref[0] = _rms(x) * g_ref[...]


def _final(li, x_all, y4, route, modsel, skip, final_g):
    b, l, d = x_all.shape
    off = skip // TM
    lo = l - skip
    return pl.pallas_call(
        _final_kernel, out_shape=jax.ShapeDtypeStruct((b, lo, d), F32), grid=(b, lo // TM),
        in_specs=[pl.BlockSpec((1, TM, d), lambda i, j: (i, j + off, 0)),
                  pl.BlockSpec((TOP_K, 1, TM, y4.shape[3]), lambda i, j: (0, i, j + off, 0)),
                  pl.BlockSpec((1, TM, LANES), lambda i, j: (i, j + off, 0)),
                  _mod_spec(li, off)(d),
                  pl.BlockSpec((1, d), lambda i, j: (0, 0))],
        out_specs=pl.BlockSpec((1, TM, d), lambda i, j: (i, j, 0)),
        compiler_params=_params(2), name="final_norm",
    )(x_all, y4, route, modsel, final_g)


def _rope_tables(n, ctx, rot_dim, group_pattern):
    t = jnp.arange(n, dtype=jnp.int32)
    row = (t // GRID_W).astype(F32)
    col = (t % GRID_W).astype(F32)
    per_axis = rot_dim // 2
    inv = ROPE_BASE ** (-jnp.arange(0, per_axis, 2, dtype=F32) / per_axis)
    ang = jnp.concatenate([row[:, None] * inv[None], col[:, None] * inv[None]], axis=-1)
    cos, sin = jnp.cos(ang), jnp.sin(ang)
    half = rot_dim // 2
    c = jnp.ones((n, LANES), F32)
    sdn = jnp.zeros((n, LANES), F32)
    sup = jnp.zeros((n, LANES), F32)
    for off in group_pattern:
        c = c.at[:, off:off + half].set(cos).at[:, off + half:off + rot_dim].set(cos)
        sdn = sdn.at[:, off:off + half].set(-sin)
        sup = sup.at[:, off + half:off + rot_dim].set(sin)
    tab = jnp.stack([c, sdn, sup])
    ident = jnp.stack([jnp.ones((ctx, LANES), F32), jnp.zeros((ctx, LANES), F32), jnp.zeros((ctx, LANES), F32)])
    return jnp.concatenate([ident, tab], axis=1)


def _wprep_kernel(w_ref, o_ref):
    x = w_ref[0]
    rows = x.shape[0]
    o = 3 * 512 + MLA_Q_LORA + MLA_KV_LORA
    zeros = lambda n: jnp.zeros((rows, n), F32)
    kr = jnp.concatenate([zeros(MLA_NOPE), x[:, o:o + MLA_ROPE], zeros(LANES - MLA_NOPE - MLA_ROPE)], axis=1)
    o += MLA_ROPE
    cq = x[:, o:o + 512]
    k0, k1 = x[:, o + 512:o + 576], x[:, o + 576:o + 640]
    v0, v1 = x[:, o + 640:o + 704], x[:, o + 704:o + 768]
    g = x[:, o + 768:]
    o_ref[0] = jnp.concatenate([x[:, :3 * 512 + MLA_Q_LORA + MLA_KV_LORA], kr, cq, k0, k0, k1, k1, v0, v0, v1, v1, g],
                               axis=1).astype(BF16)


def _wprep(w_in):
    depth, d, cols = w_in.shape
    out_cols = cols - MLA_ROPE + LANES + 2 * 128
    tr = 128
    return pl.pallas_call(
        _wprep_kernel,
        out_shape=jax.ShapeDtypeStruct((depth, d, out_cols), BF16),
        grid=(depth, d // tr),
        in_specs=[pl.BlockSpec((1, tr, cols), lambda l, i: (l, i, 0))],
        out_specs=pl.BlockSpec((1, tr, out_cols), lambda l, i: (l, i, 0)),
        compiler_params=_params(2),
        name="wprep",
    )(w_in)


def _prep_weights(w_in, mla_w_uq, mla_w_ukv):
    depth = w_in.shape[0]
    w = _wprep(w_in)
    uq = mla_w_uq.reshape(depth, MLA_Q_LORA, MLA_HEADS, MLA_NOPE + MLA_ROPE)
    uq = jnp.pad(uq, ((0, 0), (0, 0), (0, 0), (0, LANES - MLA_NOPE - MLA_ROPE)))
    uq = uq.reshape(depth, MLA_Q_LORA, MLA_HEADS * LANES).astype(BF16)
    ukv = mla_w_ukv.reshape(depth, MLA_KV_LORA, MLA_HEADS, MLA_NOPE + MLA_V)
    uk = jnp.pad(ukv[..., :MLA_NOPE], ((0, 0), (0, 0), (0, 0), (0, LANES - MLA_NOPE)))
    uk = uk.reshape(depth, MLA_KV_LORA, MLA_HEADS * LANES)
    uv = ukv[..., MLA_NOPE:].reshape(depth, MLA_KV_LORA, MLA_HEADS * MLA_V)
    ukv = jnp.concatenate([uk, uv], axis=-1).astype(BF16)
    return w, uq, ukv


def kernel(x, c, ctx, c_ctx, norm1_g, norm2_g, w_mod, b_mod, w_in, na_rpb, mla_q_norm_g, mla_kv_norm_g, mla_w_uq, mla_w_ukv, swa_sink, w_branch, w_out, router_w, router_b, expert_w_gate_up, expert_b_gate_up, expert_w_down, expert_b_down, final_norm_g):
    b, n, d = x.shape
    lc = ctx.shape[1]
    l = lc + n
    depth = w_in.shape[0]
    n_exp = router_w.shape[2]
    assert lc == TM and n % TM == 0 and TM == NA_QROWS * GRID_W

    cvec = jnp.zeros((8, d), F32).at[:b].set(c).at[b].set(c_ctx)
    mod = _modulation(cvec, w_mod, b_mod)
    mod_lat = mod[:, :b].reshape(depth, b, 1, 6, d)
    mod_ctx = jnp.broadcast_to(mod[:, b].reshape(depth, 1, 1, 6, d), (depth, b, 1, 6, d))
    modsel = jnp.concatenate([mod_ctx, mod_lat], axis=2)

    w_all, uq_all, ukv_all = _prep_weights(w_in, mla_w_uq, mla_w_ukv)
    wbr_all = w_branch.astype(BF16)
    wout_all = w_out.astype(BF16)
    rw_all = jnp.pad(router_w, ((0, 0), (0, 0), (0, LANES - n_exp))).astype(BF16)
    rb_all = jnp.pad(router_b, ((0, 0), (0, LANES - n_exp))).reshape(depth, 1, LANES)
    rope_b = _rope_tables(n, lc, MLA_ROPE, (MLA_NOPE,))
    rope_c = _rope_tables(n, lc, SWA_HEAD_DIM, (0, SWA_HEAD_DIM))
    bias_all = _na_bias_tables(na_rpb, n // GRID_W)

    g1_all = norm1_g.reshape(depth, 1, d)
    g2_all = norm2_g.reshape(depth, 1, d)
    gq_all = mla_q_norm_g.reshape(depth, 1, -1)
    gkv_all = mla_kv_norm_g.reshape(depth, 1, -1)

    x_all = jnp.concatenate([ctx, x], axis=1)
    moe = None
    for li in range(depth):
        ms = modsel
        outs = _inproj(li, x_all, moe, ms, g1_all, w_all, uq_all, ukv_all, gq_all, gkv_all, rope_b, rope_c)
        if moe is not None:
            x_all, outs = outs[0], outs[1:]
        aq, ak, av, bq, bk, bv, cq, ck, cv, gate = outs
        oa = _na_attention(li, aq, ak, av, bias_all, lc)
        ob = _mla_attention(bq, bk, bv, lc)
        oc = _swa_attention(swa_sink[li], cq, ck, cv, lc)
        x_all, h2, route, cnt = _merge(li, oa, ob, oc, gate, x_all, ms, wbr_all, wout_all,
                                       g2_all, rw_all, rb_all, n_exp)
        dest, blk_e, nxt_e, n_used, n_rows = _layout(route, cnt, n_exp)
        dest = dest.reshape(TOP_K * b * l)
        xs = _sc_scatter_rows(h2.reshape(b * l, d // 2), dest, n_rows)
        ys = _moe_experts(li, blk_e, nxt_e, n_used, xs, expert_w_gate_up, expert_b_gate_up,
                          expert_w_down, expert_b_down)
        moe = (_sc_gather_rows(ys, dest).reshape(TOP_K, b, l, d // 2), route)
    return _final(depth - 1, x_all, moe[0], moe[1], modsel, lc, final_norm_g.reshape(1, d))
```

```python
import functools

import numpy as np
import jax
import jax.numpy as jnp
from jax import lax
from jax.experimental import pallas as pl
from jax.experimental.pallas import tpu as pltpu
from jax.experimental.pallas import tpu_sc as plsc

GRID_W = 64
EPS = 1e-6
ROPE_BASE = 10000.0
NEG_INF = -1e30
LOG2E = 1.4426950408889634
LANES = 128

NA_HEADS = 8
NA_HEAD_DIM = 64
NA_KH = 8
NA_KW = 16
NA_QROWS = 4
NA_SLAB = 12
MLA_HEADS = 8
MLA_NOPE = 64
MLA_ROPE = 32
MLA_V = 64
MLA_KEY_PARTS = 4
MLA_Q_LORA = 256
MLA_KV_LORA = 128
SWA_HEADS = 8
SWA_KV_HEADS = 2
SWA_HEAD_DIM = 64
SWA_WINDOW = 128
N_BRANCH = 3
BRANCH_W = 512
N_EXPERTS = 32
TOP_K = 4
SWIGLU_ALPHA = 1.702
SWIGLU_LIMIT = 7.0

TM = 256
MOE_TM = 512
SC_CORES = 2
SC_SUBCORES = 16
SC_SCATTER_ROWS = 64
SC_GATHER_ROWS = 64
VMEM_LIMIT = 56 * 1024 * 1024

BF16 = jnp.bfloat16
F32 = jnp.float32


def _dot(a, b):
    return jnp.dot(a, b, preferred_element_type=F32)


def _dot_nt(a, b):
    return lax.dot_general(a, b, (((1,), (1,)), ((), ())), preferred_element_type=F32)


def _params(n_axes, vmem=VMEM_LIMIT):
    return pltpu.CompilerParams(dimension_semantics=("arbitrary",) * n_axes, vmem_limit_bytes=vmem)


def _layer_spec(a, li):
    return pl.BlockSpec((None,) + a.shape[1:], lambda *_: (li,) + (0,) * (a.ndim - 1))


def _mod_spec(li, off=0):
    return lambda d: pl.BlockSpec((None, 1, 1, 6, d), lambda i, j: (li, i, jnp.minimum(j + off, 1), 0, 0))


def _rms(x):
    return x * lax.rsqrt(jnp.mean(x * x, axis=-1, keepdims=True) + EPS)


def _sigmoid(x):
    return 1.0 / (1.0 + jnp.exp(-x))


def _pack_rows(xb):
    half = xb.shape[1] // 2
    lo = pltpu.bitcast(xb[:, :half].astype(F32), jnp.int32)
    hi = pltpu.bitcast(xb[:, half:].astype(F32), jnp.int32)
    return (hi & jnp.int32(-65536)) | lax.shift_right_logical(lo, 16)


def _unpack_rows(w):
    lo = pltpu.bitcast(lax.shift_left(w, 16), F32).astype(BF16)
    hi = pltpu.bitcast(w & jnp.int32(-65536), F32).astype(BF16)
    return jnp.concatenate([lo, hi], axis=1)


def _mod_kernel(c_ref, w_ref, b_ref, o_ref):
    c = c_ref[...]
    s = (c * _sigmoid(c)).astype(BF16)
    o_ref[0] = _dot(s, w_ref[0].astype(BF16)) + b_ref[0]


def _modulation(cvec, w_mod, b_mod):
    depth, d, n6 = w_mod.shape
    tn = n6 // 4
    return pl.pallas_call(
        _mod_kernel,
        out_shape=jax.ShapeDtypeStruct((depth, 8, n6), F32),
        grid=(depth, n6 // tn),
        in_specs=[pl.BlockSpec((8, d), lambda l, j: (0, 0)),
                  pl.BlockSpec((1, d, tn), lambda l, j: (l, 0, j)),
                  pl.BlockSpec((1, 1, tn), lambda l, j: (l, 0, j))],
        out_specs=pl.BlockSpec((1, 8, tn), lambda l, j: (l, 0, j)),
        compiler_params=_params(2),
        name="modulation",
    )(cvec, w_mod, b_mod.reshape(depth, 1, n6))


def _rope_groups(x, tab_ref, shift):
    cos, sdn, sup = tab_ref[0], tab_ref[1], tab_ref[2]
    outs = []
    for g in range(x.shape[1] // LANES):
        xg = x[:, g * LANES:(g + 1) * LANES]
        outs.append(xg * cos + pltpu.roll(xg, LANES - shift, 1) * sdn + pltpu.roll(xg, shift, 1) * sup)
    return outs[0] if len(outs) == 1 else jnp.concatenate(outs, axis=1)


def _inproj_kernel(*refs, with_moe):
    if with_moe:
        x_ref, y4_ref, route_ref, modp_ref = refs[:4]
        refs = refs[4:]
        (mod_ref, g1_ref, w_ref, wuq_ref, wukv_ref, gq_ref, gkv_ref, rb_ref, rc_ref, xo_ref,
         aq_ref, ak_ref, av_ref, bq_ref, bk_ref, bv_ref, cq_ref, ck_ref, cv_ref, gate_ref) = refs
        x = x_ref[0] + modp_ref[0, 0][5:6] * _combine(y4_ref, route_ref)
        xo_ref[0] = x
    else:
        (x_ref, mod_ref, g1_ref, w_ref, wuq_ref, wukv_ref, gq_ref, gkv_ref, rb_ref, rc_ref,
         aq_ref, ak_ref, av_ref, bq_ref, bk_ref, bv_ref, cq_ref, ck_ref, cv_ref, gate_ref) = refs
        x = x_ref[0]
    mod = mod_ref[0, 0]
    h = (_rms(x) * g1_ref[...]) * (1.0 + mod[1:2]) + mod[0:1]
    hb = h.astype(BF16)
    acc_b = _dot(hb, w_ref[:, 1536:2048])
    acc_a = _dot(hb, w_ref[:, 0:1536])
    qn = (_rms(acc_b[:, 0:256]) * gq_ref[...]).astype(BF16)
    kvn = (_rms(acc_b[:, 256:384]) * gkv_ref[...]).astype(BF16)
    q = _dot(qn, wuq_ref[...])
    kv = _dot(kvn, wukv_ref[...])
    acc_c = _dot(hb, w_ref[:, 2048:3072])
    aq_ref[0] = (acc_a[:, 0:512] * (NA_HEAD_DIM ** -0.5 * LOG2E)).astype(BF16)
    ak_ref[0] = acc_a[:, 512:1024].astype(BF16)
    av_ref[0] = acc_a[:, 1024:1536].T.astype(BF16)
    acc_g = _dot(hb, w_ref[:, 3072:])
    kr = _rope_groups(acc_b[:, 384:512], rb_ref, MLA_ROPE // 2)
    q = _rope_groups(q, rb_ref, MLA_ROPE // 2)
    bq_ref[0] = (q * ((MLA_NOPE + MLA_ROPE) ** -0.5 * LOG2E)).astype(BF16)
    bk_ref[0] = (kv[:, 0:1024] + jnp.concatenate([kr] * MLA_HEADS, axis=1)).astype(BF16)
    bv_ref[0] = kv[:, 1024:1536].T.astype(BF16)
    cq_ref[0] = (_rope_groups(acc_c[:, 0:512], rc_ref, SWA_HEAD_DIM // 2) * (SWA_HEAD_DIM ** -0.5 * LOG2E)).astype(BF16)
    ck_ref[0] = _rope_groups(acc_c[:, 512:768], rc_ref, SWA_HEAD_DIM // 2).astype(BF16)
    cv = jnp.concatenate([acc_c[:, 768:832], acc_c[:, 896:960]], axis=1)
    cv_ref[0] = cv.T.astype(BF16)
    gate_ref[0] = _sigmoid(acc_g).astype(BF16)


def _inproj(li, x_all, moe, modsel, g1, w, wuq, wukv, gq, gkv, rope_b, rope_c):
    b, l, d = x_all.shape
    nb = l // TM
    row = lambda width: pl.BlockSpec((1, TM, width), lambda i, j: (i, j, 0))
    full = lambda a: _layer_spec(a, li)
    widths = (512, 512, 512, 1024, 1024, 512, 512, 256, SWA_KV_HEADS * SWA_HEAD_DIM, N_BRANCH * d)
    in_specs = [row(d)]
    out_specs = [row(wd) for wd in widths]
    out_shape = [jax.ShapeDtypeStruct((b, l, wd), BF16) for wd in widths]
    for pos in (2, 5, 8):
        out_specs[pos] = pl.BlockSpec((1, widths[pos], TM), lambda i, j: (i, 0, j))
        out_shape[pos] = jax.ShapeDtypeStruct((b, widths[pos], l), BF16)
    args = [x_all]
    if moe is not None:
        y4, route = moe
        in_specs += [pl.BlockSpec((TOP_K, 1, TM, y4.shape[3]), lambda i, j: (0, i, j, 0)), row(LANES),
                     _mod_spec(li - 1)(d)]
        out_specs = [row(d)] + out_specs
        out_shape = [jax.ShapeDtypeStruct((b, l, d), F32)] + out_shape
        args += [y4, route, modsel]
    in_specs += [_mod_spec(li)(d), full(g1), full(w), full(wuq), full(wukv), full(gq), full(gkv),
                 pl.BlockSpec((3, TM, LANES), lambda i, j: (0, j, 0)),
                 pl.BlockSpec((3, TM, LANES), lambda i, j: (0, j, 0))]
    args += [modsel, g1, w, wuq, wukv, gq, gkv, rope_b, rope_c]
    return pl.pallas_call(
        functools.partial(_inproj_kernel, with_moe=moe is not None),
        out_shape=out_shape,
        grid=(b, nb),
        in_specs=in_specs,
        out_specs=out_specs,
        compiler_params=_params(2),
        name="inproj",
    )(*args)


def _lane_lo():
    return lax.broadcasted_iota(jnp.int32, (1, LANES), 1) < (LANES // 2)


def _split_heads(qp, lo):
    zero = jnp.zeros_like(qp)
    return jnp.where(lo, qp, zero), jnp.where(lo, zero, qp)


def _softmax_pv_t(score_parts, vt_parts, extra_logit=None):
    m = score_parts[0].max(axis=0, keepdims=True)
    for s in score_parts[1:]:
        m = jnp.maximum(m, s.max(axis=0, keepdims=True))
    if extra_logit is not None:
        m = jnp.maximum(m, extra_logit)
    den = None
    acc = None
    for s, vt in zip(score_parts, vt_parts):
        e = jnp.exp2(s - m)
        d = e.sum(axis=0, keepdims=True)
        den = d if den is None else den + d
        o = _dot(vt, e.astype(BF16))
        acc = o if acc is None else acc + o
    if extra_logit is not None:
        den = den + jnp.exp2(extra_logit - m)
    return acc / den


def _heads_ahead(n_heads, ahead, scores, finish):
    pending = [scores(h) for h in range(ahead)]
    outs = []
    for h in range(n_heads):
        if h + ahead < n_heads:
            pending.append(scores(h + ahead))
        outs.append(finish(h, pending.pop(0)))
    return outs


def _na_kernel(q_ref, k_ref, vt_ref, bias_ref, o_ref, *, ctx, rows):
    j = pl.program_id(1)
    lo = _lane_lo()

    def head_q(h):
        sl = slice((h // 2) * LANES, (h // 2 + 1) * LANES)
        return _split_heads(q_ref[0, :, sl], lo)[h % 2], sl

    def vt(h, cols):
        return vt_ref[0, h * NA_HEAD_DIM:(h + 1) * NA_HEAD_DIM, cols]

    @pl.when(j == 0)
    def _():
        def scores(h):
            qm, sl = head_q(h)
            return [_dot_nt(k_ref[0, 0:ctx, sl], qm)]

        outs = _heads_ahead(NA_HEADS, 2, scores, lambda h, s: _softmax_pv_t(s, [vt(h, slice(0, ctx))]))
        o_ref[0] = jnp.concatenate(outs, axis=0).T.astype(BF16)

    @pl.when(j > 0)
    def _():
        r = (j - 1) * NA_QROWS
        s0 = jnp.clip(r - NA_KH // 2, 0, rows - NA_SLAB)
        slab = pl.ds(pl.multiple_of(ctx + s0 * GRID_W, NA_QROWS * GRID_W), NA_SLAB * GRID_W)

        def scores(h):
            qm, sl = head_q(h)
            return [_dot_nt(k_ref[0, 0:ctx, sl], qm), _dot_nt(k_ref[0, slab, sl], qm) + bias_ref[0, h]]

        outs = _heads_ahead(NA_HEADS, 2, scores,
                            lambda h, s: _softmax_pv_t(s, [vt(h, slice(0, ctx)), vt(h, slab)]))
        o_ref[0] = jnp.concatenate(outs, axis=0).T.astype(BF16)


def _na_attention(li, aq, ak, avt, bias, ctx):
    b, l, w = aq.shape
    assert ((l - ctx) // GRID_W) % NA_QROWS == 0 and ctx % (NA_QROWS * GRID_W) == 0
    nb = l // TM
    rows = (l - ctx) // GRID_W
    last = nb - 1

    def bias_map(i, j):
        return (li, jnp.where(j <= 1, 0, jnp.where(j == last, 2, 1)), 0, 0, 0)

    return pl.pallas_call(
        functools.partial(_na_kernel, ctx=ctx, rows=rows),
        out_shape=jax.ShapeDtypeStruct((b, l, w), BF16),
        grid=(b, nb),
        in_specs=[pl.BlockSpec((1, TM, w), lambda i, j: (i, j, 0)),
                  pl.BlockSpec((1, l, w), lambda i, j: (i, 0, 0)),
                  pl.BlockSpec((1, w, l), lambda i, j: (i, 0, 0)),
                  pl.BlockSpec((None, 1, NA_HEADS, NA_SLAB * GRID_W, TM), bias_map)],
        out_specs=pl.BlockSpec((1, TM, w), lambda i, j: (i, j, 0)),
        compiler_params=_params(2),
        name="na_attention",
    )(aq, ak, avt, bias)


def _na_bias_tables(na_rpb, rows):
    depth = na_rpb.shape[0]
    qc = np.arange(GRID_W)[:, None]
    kc = np.arange(GRID_W)[None, :]
    c0 = np.clip(qc - NA_KW // 2, 0, GRID_W - NA_KW)
    col_ok = (kc >= c0) & (kc < c0 + NA_KW)
    col_idx = np.clip(kc - qc + NA_KW - 1, 0, 2 * NA_KW - 2)
    n_off = 2 * NA_KH - 1
    bc = jnp.take(na_rpb, jnp.asarray(col_idx.reshape(-1)), axis=3).reshape(depth, NA_HEADS, n_off, GRID_W, GRID_W)
    bc = jnp.where(jnp.asarray(col_ok), bc * LOG2E, NEG_INF)
    bc = jnp.concatenate([bc, jnp.full((depth, NA_HEADS, 1, GRID_W, GRID_W), NEG_INF, F32)], axis=2)
    bc = bc.swapaxes(-1, -2)
    i_idx = np.zeros((3, NA_SLAB, NA_QROWS), np.int32)
    for case, (r, s) in enumerate(((0, 0), (NA_KH // 2, 0), (rows - NA_QROWS, rows - NA_SLAB))):
        for a in range(NA_QROWS):
            qr = r + a
            r0 = min(max(qr - NA_KH // 2, 0), rows - NA_KH)
            for c in range(NA_SLAB):
                kr = s + c
                i_idx[case, c, a] = kr - qr + NA_KH - 1 if r0 <= kr < r0 + NA_KH else n_off
    t = jnp.take(bc, jnp.asarray(i_idx.reshape(-1)), axis=2).reshape(
        depth, NA_HEADS, 3, NA_SLAB, NA_QROWS, GRID_W, GRID_W)
    t = t.transpose(0, 2, 1, 3, 5, 4, 6)
    return t.reshape(depth, 3, NA_HEADS, NA_SLAB * GRID_W, NA_QROWS * GRID_W)


def _mla_kernel(q_ref, k_ref, vt_ref, o_ref, *, ctx):
    j = pl.program_id(1)

    def run(nkeys):
        tiles = nkeys // 256
        nparts = min(MLA_KEY_PARTS, tiles)
        cuts = [(tiles * p // nparts) * 256 for p in range(nparts + 1)]
        parts = list(zip(cuts[:-1], cuts[1:]))

        def scores(h):
            hsl = slice(h * LANES, (h + 1) * LANES)
            return [_dot_nt(k_ref[0, a:b, hsl], q_ref[0, :, hsl]) for a, b in parts]

        def finish(h, s_parts):
            return _softmax_pv_t(s_parts, [vt_ref[0, h * MLA_V:(h + 1) * MLA_V, a:b] for a, b in parts])

        outs = _heads_ahead(MLA_HEADS, 3, scores, finish)
        o_ref[0] = jnp.concatenate(outs, axis=0).T.astype(BF16)

    @pl.when(j == 0)
    def _():
        run(ctx)

    @pl.when(j > 0)
    def _():
        run(k_ref.shape[1])


def _mla_attention(bq, bk, bvt, ctx):
    b, l, wq = bq.shape
    wv = bvt.shape[1]
    return pl.pallas_call(
        functools.partial(_mla_kernel, ctx=ctx),
        out_shape=jax.ShapeDtypeStruct((b, l, wv), BF16),
        grid=(b, l // TM),
        in_specs=[pl.BlockSpec((1, TM, wq), lambda i, j: (i, j, 0)),
                  pl.BlockSpec((1, l, wq), lambda i, j: (i, 0, 0)),
                  pl.BlockSpec((1, wv, l), lambda i, j: (i, 0, 0))],
        out_specs=pl.BlockSpec((1, TM, wv), lambda i, j: (i, j, 0)),
        compiler_params=_params(2),
        name="mla_attention",
    )(bq, bk, bvt)


def _swa_kernel(sink_ref, q_ref, k_ref, vt_ref, o_ref, *, ctx, n_lat):
    j = pl.program_id(1)
    lo = _lane_lo()
    group = SWA_HEADS // SWA_KV_HEADS
    band = TM + 2 * SWA_WINDOW

    def head_q(h):
        sl = slice((h // 2) * LANES, (h // 2 + 1) * LANES)
        return _split_heads(q_ref[0, :, sl], lo)[h % 2]

    def kv_lanes(h):
        return slice((h // group) * LANES, (h // group + 1) * LANES)

    def vt(h, cols):
        kv = h // group
        return vt_ref[0, kv * SWA_HEAD_DIM:(kv + 1) * SWA_HEAD_DIM, cols]

    def sink(h):
        return jnp.full((1, 1), sink_ref[h] * LOG2E, F32)

    @pl.when(j == 0)
    def _():
        outs = _heads_ahead(
            SWA_HEADS, 2, lambda h: [_dot_nt(k_ref[0, 0:ctx, kv_lanes(h)], head_q(h))],
            lambda h, s: _softmax_pv_t(s, [vt(h, slice(0, ctx))], extra_logit=sink(h)))
        o_ref[0] = jnp.concatenate(outs, axis=0).T.astype(BF16)

    @pl.when(j > 0)
    def _():
        q0 = (j - 1) * TM
        s0 = jnp.clip(q0 - SWA_WINDOW, 0, n_lat - band)
        rows = pl.ds(pl.multiple_of(ctx + s0, SWA_WINDOW), band)
        kpos = s0 + lax.broadcasted_iota(jnp.int32, (band, TM), 0)
        qpos = q0 + lax.broadcasted_iota(jnp.int32, (band, TM), 1)
        keep = jnp.abs(qpos - kpos) <= SWA_WINDOW

        def scores(h):
            qm = head_q(h)
            return [_dot_nt(k_ref[0, 0:ctx, kv_lanes(h)], qm),
                    jnp.where(keep, _dot_nt(k_ref[0, rows, kv_lanes(h)], qm), NEG_INF)]

        outs = _heads_ahead(
            SWA_HEADS, 2, scores,
            lambda h, s: _softmax_pv_t(s, [vt(h, slice(0, ctx)), vt(h, rows)], extra_logit=sink(h)))
        o_ref[0] = jnp.concatenate(outs, axis=0).T.astype(BF16)


def _swa_attention(sink, cq, ck, cvt, ctx):
    b, l, w = cq.shape
    wk = ck.shape[2]
    wv = cvt.shape[1]
    grid_spec = pltpu.PrefetchScalarGridSpec(
        num_scalar_prefetch=1,
        grid=(b, l // TM),
        in_specs=[pl.BlockSpec((1, TM, w), lambda i, j, s: (i, j, 0)),
                  pl.BlockSpec((1, l, wk), lambda i, j, s: (i, 0, 0)),
                  pl.BlockSpec((1, wv, l), lambda i, j, s: (i, 0, 0))],
        out_specs=pl.BlockSpec((1, TM, w), lambda i, j, s: (i, j, 0)),
    )
    return pl.pallas_call(
        functools.partial(_swa_kernel, ctx=ctx, n_lat=l - ctx),
        out_shape=jax.ShapeDtypeStruct((b, l, w), BF16),
        grid_spec=grid_spec,
        compiler_params=_params(2),
        name="swa_attention",
    )(sink, cq, ck, cvt)


def _merge_kernel(oa_ref, ob_ref, oc_ref, gate_ref, x_ref, mod_ref, wbr_ref, wout_ref, g2_ref,
                  rw_ref, rb_ref, xo_ref, h2_ref, route_ref, cnt_ref, run_ref, *, n_exp):
    d = x_ref.shape[2]

    @pl.when(jnp.logical_and(pl.program_id(0) == 0, pl.program_id(1) == 0))
    def _():
        run_ref[...] = jnp.zeros_like(run_ref)

    mod = mod_ref[0, 0]
    mix = None
    for i, o_ref in enumerate((oa_ref, ob_ref, oc_ref)):
        t = gate_ref[0, :, i * d:(i + 1) * d].astype(F32) * _dot(o_ref[0], wbr_ref[i])
        mix = t if mix is None else mix + t
    y = _dot(mix.astype(BF16), wout_ref[...])
    x = x_ref[0] + mod[2:3] * y
    xo_ref[0] = x
    h2 = (_rms(x) * g2_ref[...]) * (1.0 + mod[4:5]) + mod[3:4]
    h2b = h2.astype(BF16)
    h2_ref[0] = _pack_rows(h2b)
    logits = _dot(h2b, rw_ref[...]) + rb_ref[...]
    lane = lax.broadcasted_iota(jnp.int32, logits.shape, 1).astype(F32)
    work = jnp.where(lane < n_exp, logits, -jnp.inf)
    ids, vals = [], []
    for _ in range(TOP_K):
        m = work.max(axis=-1, keepdims=True)
        idx = jnp.where(work == m, lane, float(LANES)).min(axis=-1, keepdims=True)
        ids.append(idx)
        vals.append(m)
        work = jnp.where(lane == idx, -jnp.inf, work)
    ex = [jnp.exp(v - vals[0]) for v in vals]
    den = ex[0] + ex[1] + ex[2] + ex[3]
    hits = jnp.zeros(logits.shape, F32)
    for idx in ids:
        hits = hits + jnp.where(lane == idx, 1.0, 0.0)
    r = lax.broadcasted_iota(jnp.int32, (TM, TM), 0)
    c = lax.broadcasted_iota(jnp.int32, (TM, TM), 1)
    tri = jnp.where(c < r, 1.0, 0.0).astype(BF16)
    before = _dot(tri, hits.astype(BF16)) + run_ref[0:1]
    route = jnp.zeros(logits.shape, F32)
    for k in range(TOP_K):
        rank = jnp.where(lane == ids[k], before, 0.0).sum(axis=-1, keepdims=True)
        route = jnp.where(lane == k, ids[k], route)
        route = jnp.where(lane == TOP_K + k, ex[k] / den, route)
        route = jnp.where(lane == 2 * TOP_K + k, rank, route)
    route_ref[0] = route
    run_ref[...] = run_ref[...] + hits.sum(axis=0, keepdims=True)
    cnt_ref[...] = run_ref[...]


def _merge(li, oa, ob, oc, gate, x_all, modsel, wbr, wout, g2, rw, rb, n_exp):
    b, l, d = x_all.shape
    row = lambda width: pl.BlockSpec((1, TM, width), lambda i, j: (i, j, 0))
    full = lambda a: _layer_spec(a, li)
    return pl.pallas_call(
        functools.partial(_merge_kernel, n_exp=n_exp),
        out_shape=[jax.ShapeDtypeStruct((b, l, d), F32),
                   jax.ShapeDtypeStruct((b, l, d // 2), jnp.int32),
                   jax.ShapeDtypeStruct((b, l, LANES), F32),
                   jax.ShapeDtypeStruct((8, LANES), F32)],
        grid=(b, l // TM),
        in_specs=[row(BRANCH_W), row(BRANCH_W), row(BRANCH_W), row(N_BRANCH * d), row(d), _mod_spec(li)(d),
                  full(wbr), full(wout), full(g2), full(rw), full(rb)],
        out_specs=[row(d), row(d // 2), row(LANES), pl.BlockSpec((8, LANES), lambda i, j: (0, 0))],
        scratch_shapes=[pltpu.VMEM((8, LANES), F32)],
        compiler_params=_params(2),
        name="merge",
    )(oa, ob, oc, gate, x_all, modsel, wbr, wout, g2, rw, rb)


def _moe_kernel(be_ref, nx_ref, nu_ref, x_ref, wgu_hbm, bgu_ref, wdn_hbm, bdn_ref, y_ref,
                gu_stage, dn_stage, wgu_s, wdn_s, sem, *, li):
    i = pl.program_id(0)
    ff = wdn_s.shape[0]

    def fetch(e):
        return (pltpu.make_async_copy(wgu_hbm.at[li, e], gu_stage, sem.at[0]),
                pltpu.make_async_copy(wdn_hbm.at[li, e], dn_stage, sem.at[1]))

    @pl.when(i < nu_ref[0])
    def _():
        e = be_ref[i]

        @pl.when(i == 0)
        def _():
            for copy in fetch(e):
                copy.start()

        @pl.when(jnp.logical_or(i == 0, e != be_ref[jnp.maximum(i - 1, 0)]))
        def _():
            for copy in fetch(e):
                copy.wait()
            wgu_s[...] = gu_stage[...].astype(BF16)
            wdn_s[...] = dn_stage[...].astype(BF16)
            nxt = nx_ref[i]

            @pl.when(nxt >= 0)
            def _():
                for copy in fetch(nxt):
                    copy.start()

        gu = _dot(_unpack_rows(x_ref[...]), wgu_s[...]) + bgu_ref[0]
        glu = jnp.minimum(gu[:, :ff], SWIGLU_LIMIT)
        lin = jnp.clip(gu[:, ff:], -SWIGLU_LIMIT, SWIGLU_LIMIT)
        act = glu * _sigmoid(SWIGLU_ALPHA * glu) * (lin + 1.0)
        y_ref[...] = _pack_rows((_dot(act.astype(BF16), wdn_s[...]) + bdn_ref[0]).astype(BF16))


def _moe_experts(li, blk_e, nxt_e, n_used, xs, w_gu, b_gu, w_dn, b_dn):
    n_rows, packed_w = xs.shape
    depth, n_exp, d, ff2 = w_gu.shape
    ff = ff2 // 2
    n_blocks = n_rows // MOE_TM

    def row_map(i, be, nx, nu):
        return (jnp.minimum(i, nu[0] - 1), 0)

    def b_map(i, be, nx, nu):
        return (li, be[jnp.minimum(i, nu[0] - 1)], 0, 0)

    grid_spec = pltpu.PrefetchScalarGridSpec(
        num_scalar_prefetch=3,
        grid=(n_blocks,),
        in_specs=[pl.BlockSpec((MOE_TM, packed_w), row_map),
                  pl.BlockSpec(memory_space=pl.ANY),
                  pl.BlockSpec((None, 1, 1, ff2), b_map),
                  pl.BlockSpec(memory_space=pl.ANY),
                  pl.BlockSpec((None, 1, 1, d), b_map)],
        out_specs=pl.BlockSpec((MOE_TM, packed_w), row_map),
        scratch_shapes=[pltpu.VMEM((d, ff2), F32), pltpu.VMEM((ff, d), F32),
                        pltpu.VMEM((d, ff2), BF16), pltpu.VMEM((ff, d), BF16),
                        pltpu.SemaphoreType.DMA((2,))],
    )
    return pl.pallas_call(
        functools.partial(_moe_kernel, li=li),
        out_shape=jax.ShapeDtypeStruct((n_rows, packed_w), jnp.int32),
        grid_spec=grid_spec,
        compiler_params=_params(1),
        name="moe_experts",
    )(blk_e, nxt_e, n_used, xs, w_gu, b_gu.reshape(depth, n_exp, 1, ff2), w_dn, b_dn.reshape(depth, n_exp, 1, d))


def _layout(route, cnt, n_exp):
    b, l, _ = route.shape
    t = b * l
    ids = route[..., 0:TOP_K].astype(jnp.int32)
    rank = route[..., 2 * TOP_K:3 * TOP_K].astype(jnp.int32)
    counts = cnt[0, :n_exp].astype(jnp.int32)
    padded = (counts + MOE_TM - 1) // MOE_TM * MOE_TM
    pad_end = jnp.cumsum(padded)
    pad_start = pad_end - padded
    onehot = ids[..., None] == jnp.arange(n_exp, dtype=jnp.int32)
    dest = jnp.sum(jnp.where(onehot, pad_start, 0), axis=-1) + rank
    dest = dest.reshape(t, TOP_K).T
    n_blocks = -(-t * TOP_K // MOE_TM) + n_exp
    blk_start = jnp.arange(n_blocks, dtype=jnp.int32) * MOE_TM
    blk_e = jnp.minimum(jnp.sum(blk_start[:, None] >= pad_end[None, :], axis=1), n_exp - 1).astype(jnp.int32)
    n_used = (pad_end[-1] // MOE_TM).astype(jnp.int32).reshape(1)
    ar = jnp.arange(n_exp, dtype=jnp.int32)
    later = jnp.logical_and(ar[None, :] > ar[:, None], (counts > 0)[None, :])
    nxt_of = jnp.min(jnp.where(later, ar[None, :], n_exp), axis=1)
    nxt_of = jnp.where(nxt_of == n_exp, -1, nxt_of)
    nxt_e = jnp.sum(jnp.where(blk_e[:, None] == ar[None, :], nxt_of[None, :], 0), axis=1).astype(jnp.int32)
    return dest, blk_e, nxt_e, n_used, n_blocks * MOE_TM


def _sc_mesh():
    return plsc.VectorSubcoreMesh(core_axis_name="c", subcore_axis_name="s",
                                  num_cores=SC_CORES, num_subcores=SC_SUBCORES)


def _sc_worker():
    return lax.axis_index("s") * SC_CORES + lax.axis_index("c")


def _sc_scatter_rows(x, idx, n_rows):
    t, d = x.shape
    n_idx = idx.shape[0]
    workers = SC_CORES * SC_SUBCORES
    n_chunks = n_idx // (workers * SC_SCATTER_ROWS)
    assert n_chunks * workers * SC_SCATTER_ROWS == n_idx and t % SC_SCATTER_ROWS == 0 and n_chunks % 2 == 0
    idx3 = idx.reshape(workers, n_chunks, SC_SCATTER_ROWS)

    @pl.kernel(out_type=jax.ShapeDtypeStruct((n_rows, d), x.dtype), mesh=_sc_mesh(),
               scratch_types=[pltpu.VMEM((n_chunks, SC_SCATTER_ROWS), jnp.int32),
                              pltpu.VMEM((SC_SCATTER_ROWS, d), x.dtype),
                              pltpu.VMEM((SC_SCATTER_ROWS, d), x.dtype),
                              pltpu.SemaphoreType.DMA, pltpu.SemaphoreType.DMA, pltpu.SemaphoreType.DMA])
    def scatter_kernel(x_hbm, i_hbm, o_hbm, idx_v, buf0, buf1, sem0, sem1, out_sem):
        wid = _sc_worker()
        pltpu.sync_copy(i_hbm.at[wid], idx_v)
        bufs, sems = (buf0, buf1), (sem0, sem1)

        def load(chunk, b):
            src = pl.multiple_of(((wid * n_chunks + chunk) * SC_SCATTER_ROWS) % t, SC_SCATTER_ROWS)
            return pltpu.make_async_copy(x_hbm.at[pl.ds(src, SC_SCATTER_ROWS)], bufs[b], sems[b])

        load(0, 0).start()

        @pl.loop(0, n_chunks, step=2)
        def _(j):
            for b in range(2):
                chunk = j + b
                load(chunk, b).wait()
                if b == 0:
                    load(chunk + 1, 1).start()
                else:
                    @pl.when(chunk + 1 < n_chunks)
                    def _():
                        load(chunk + 1, 0).start()
                pltpu.async_copy(bufs[b], o_hbm.at[idx_v.at[chunk]], out_sem).wait()

    return scatter_kernel(x, idx3)


def _sc_gather_rows(x, idx):
    d = x.shape[1]
    n_idx = idx.shape[0]
    workers = SC_CORES * SC_SUBCORES
    per_worker = n_idx // workers
    n_chunks = per_worker // SC_GATHER_ROWS
    assert n_chunks * workers * SC_GATHER_ROWS == n_idx

    assert n_chunks % 2 == 0

    @pl.kernel(out_type=jax.ShapeDtypeStruct((n_idx, d), x.dtype), mesh=_sc_mesh(),
               scratch_types=[pltpu.VMEM((per_worker,), jnp.int32),
                              pltpu.VMEM((SC_GATHER_ROWS, d), x.dtype),
                              pltpu.VMEM((SC_GATHER_ROWS, d), x.dtype),
                              pltpu.SemaphoreType.DMA, pltpu.SemaphoreType.DMA])
    def gather_kernel(x_hbm, i_hbm, o_hbm, idx_v, buf0, buf1, sem0, sem1):
        base = _sc_worker() * per_worker
        pltpu.sync_copy(i_hbm.at[pl.ds(base, per_worker)], idx_v)
        bufs, sems = (buf0, buf1), (sem0, sem1)

        def gather(chunk, b):
            off = pl.multiple_of(chunk * SC_GATHER_ROWS, SC_GATHER_ROWS)
            return pltpu.make_async_copy(x_hbm.at[idx_v.at[pl.ds(off, SC_GATHER_ROWS)]], bufs[b], sems[b])

        gather(0, 0).start()

        @pl.loop(0, n_chunks, step=2)
        def _(j):
            for b in range(2):
                chunk = j + b
                gather(chunk, b).wait()
                if b == 0:
                    gather(chunk + 1, 1).start()
                else:
                    @pl.when(chunk + 1 < n_chunks)
                    def _():
                        gather(chunk + 1, 0).start()
                off = pl.multiple_of(chunk * SC_GATHER_ROWS, SC_GATHER_ROWS)
                pltpu.sync_copy(bufs[b], o_hbm.at[pl.ds(base + off, SC_GATHER_ROWS)])

    return gather_kernel(x, idx)


def _combine(y4_ref, route_ref):
    route = route_ref[0]
    y = None
    for k in range(TOP_K):
        t = route[:, TOP_K + k:TOP_K + k + 1] * _unpack_rows(y4_ref[k, 0]).astype(F32)
        y = t if y is None else y + t
    return y


def _final_kernel(x_ref, y4_ref, route_ref, mod_ref, g_ref, o_ref):
    x = x_ref[0] + mod_ref[0, 0][5:6] * _combine(y4_ref, route_ref)
    o_ref[0] = _rms(x) * g_ref[...]


def _final(li, x_all, y4, route, modsel, skip, final_g):
    b, l, d = x_all.shape
    off = skip // TM
    lo = l - skip
    return pl.pallas_call(
        _final_kernel, out_shape=jax.ShapeDtypeStruct((b, lo, d), F32), grid=(b, lo // TM),
        in_specs=[pl.BlockSpec((1, TM, d), lambda i, j: (i, j + off, 0)),
                  pl.BlockSpec((TOP_K, 1, TM, y4.shape[3]), lambda i, j: (0, i, j + off, 0)),
                  pl.BlockSpec((1, TM, LANES), lambda i, j: (i, j + off, 0)),
                  _mod_spec(li, off)(d),
                  pl.BlockSpec((1, d), lambda i, j: (0, 0))],
        out_specs=pl.BlockSpec((1, TM, d), lambda i, j: (i, j, 0)),
        compiler_params=_params(2), name="final_norm",
    )(x_all, y4, route, modsel, final_g)


def _rope_tables(n, ctx, rot_dim, group_pattern):
    t = jnp.arange(n, dtype=jnp.int32)
    row = (t // GRID_W).astype(F32)
    col = (t % GRID_W).astype(F32)
    per_axis = rot_dim // 2
    inv = ROPE_BASE ** (-jnp.arange(0, per_axis, 2, dtype=F32) / per_axis)
    ang = jnp.concatenate([row[:, None] * inv[None], col[:, None] * inv[None]], axis=-1)
    cos, sin = jnp.cos(ang), jnp.sin(ang)
    half = rot_dim // 2
    c = jnp.ones((n, LANES), F32)
    sdn = jnp.zeros((n, LANES), F32)
    sup = jnp.zeros((n, LANES), F32)
    for off in group_pattern:
        c = c.at[:, off:off + half].set(cos).at[:, off + half:off + rot_dim].set(cos)
        sdn = sdn.at[:, off:off + half].set(-sin)
        sup = sup.at[:, off + half:off + rot_dim].set(sin)
    tab = jnp.stack([c, sdn, sup])
    ident = jnp.stack([jnp.ones((ctx, LANES), F32), jnp.zeros((ctx, LANES), F32), jnp.zeros((ctx, LANES), F32)])
    return jnp.concatenate([ident, tab], axis=1)


def _wprep_kernel(w_ref, o_ref):
    x = w_ref[0]
    rows = x.shape[0]
    o = 3 * 512 + MLA_Q_LORA + MLA_KV_LORA
    zeros = lambda n: jnp.zeros((rows, n), F32)
    kr = jnp.concatenate([zeros(MLA_NOPE), x[:, o:o + MLA_ROPE], zeros(LANES - MLA_NOPE - MLA_ROPE)], axis=1)
    o += MLA_ROPE
    cq = x[:, o:o + 512]
    k0, k1 = x[:, o + 512:o + 576], x[:, o + 576:o + 640]
    v0, v1 = x[:, o + 640:o + 704], x[:, o + 704:o + 768]
    g = x[:, o + 768:]
    o_ref[0] = jnp.concatenate([x[:, :3 * 512 + MLA_Q_LORA + MLA_KV_LORA], kr, cq, k0, k0, k1, k1, v0, v0, v1, v1, g],
                               axis=1).astype(BF16)


def _wprep(w_in):
    depth, d, cols = w_in.shape
    out_cols = cols - MLA_ROPE + LANES + 2 * 128
    tr = 128
    return pl.pallas_call(
        _wprep_kernel,
        out_shape=jax.ShapeDtypeStruct((depth, d, out_cols), BF16),
        grid=(depth, d // tr),
        in_specs=[pl.BlockSpec((1, tr, cols), lambda l, i: (l, i, 0))],
        out_specs=pl.BlockSpec((1, tr, out_cols), lambda l, i: (l, i, 0)),
        compiler_params=_params(2),
        name="wprep",
    )(w_in)


def _prep_weights(w_in, mla_w_uq, mla_w_ukv):
    depth = w_in.shape[0]
    w = _wprep(w_in)
    uq = mla_w_uq.reshape(depth, MLA_Q_LORA, MLA_HEADS, MLA_NOPE + MLA_ROPE)
    uq = jnp.pad(uq, ((0, 0), (0, 0), (0, 0), (0, LANES - MLA_NOPE - MLA_ROPE)))
    uq = uq.reshape(depth, MLA_Q_LORA, MLA_HEADS * LANES).astype(BF16)
    ukv = mla_w_ukv.reshape(depth, MLA_KV_LORA, MLA_HEADS, MLA_NOPE + MLA_V)
    uk = jnp.pad(ukv[..., :MLA_NOPE], ((0, 0), (0, 0), (0, 0), (0, LANES - MLA_NOPE)))
    uk = uk.reshape(depth, MLA_KV_LORA, MLA_HEADS * LANES)
    uv = ukv[..., MLA_NOPE:].reshape(depth, MLA_KV_LORA, MLA_HEADS * MLA_V)
    ukv = jnp.concatenate([uk, uv], axis=-1).astype(BF16)
    return w, uq, ukv


def kernel(x, c, ctx, c_ctx, norm1_g, norm2_g, w_mod, b_mod, w_in, na_rpb, mla_q_norm_g, mla_kv_norm_g, mla_w_uq, mla_w_ukv, swa_sink, w_branch, w_out, router_w, router_b, expert_w_gate_up, expert_b_gate_up, expert_w_down, expert_b_down, final_norm_g):
    b, n, d = x.shape
    lc = ctx.shape[1]
    l = lc + n
    depth = w_in.shape[0]
    n_exp = router_w.shape[2]
    assert lc == TM and n % TM == 0 and TM == NA_QROWS * GRID_W

    cvec = jnp.zeros((8, d), F32).at[:b].set(c).at[b].set(c_ctx)
    mod = _modulation(cvec, w_mod, b_mod)
    mod_lat = mod[:, :b].reshape(depth, b, 1, 6, d)
    mod_ctx = jnp.broadcast_to(mod[:, b].reshape(depth, 1, 1, 6, d), (depth, b, 1, 6, d))
    modsel = jnp.concatenate([mod_ctx, mod_lat], axis=2)

    w_all, uq_all, ukv_all = _prep_weights(w_in, mla_w_uq, mla_w_ukv)
    wbr_all = w_branch.astype(BF16)
    wout_all = w_out.astype(BF16)
    rw_all = jnp.pad(router_w, ((0, 0), (0, 0), (0, LANES - n_exp))).astype(BF16)
    rb_all = jnp.pad(router_b, ((0, 0), (0, LANES - n_exp))).reshape(depth, 1, LANES)
    rope_b = _rope_tables(n, lc, MLA_ROPE, (MLA_NOPE,))
    rope_c = _rope_tables(n, lc, SWA_HEAD_DIM, (0, SWA_HEAD_DIM))
    bias_all = _na_bias_tables(na_rpb, n // GRID_W)

    g1_all = norm1_g.reshape(depth, 1, d)
    g2_all = norm2_g.reshape(depth, 1, d)
    gq_all = mla_q_norm_g.reshape(depth, 1, -1)
    gkv_all = mla_kv_norm_g.reshape(depth, 1, -1)

    x_all = jnp.concatenate([ctx, x], axis=1)
    moe = None
    for li in range(depth):
        ms = modsel
        outs = _inproj(li, x_all, moe, ms, g1_all, w_all, uq_all, ukv_all, gq_all, gkv_all, rope_b, rope_c)
        if moe is not None:
            x_all, outs = outs[0], outs[1:]
        aq, ak, av, bq, bk, bv, cq, ck, cv, gate = outs
        oa = _na_attention(li, aq, ak, av, bias_all, lc)
        ob = _mla_attention(bq, bk, bv, lc)
        oc = _swa_attention(swa_sink[li], cq, ck, cv, lc)
        x_all, h2, route, cnt = _merge(li, oa, ob, oc, gate, x_all, ms, wbr_all, wout_all,
                                       g2_all, rw_all, rb_all, n_exp)
        dest, blk_e, nxt_e, n_used, n_rows = _layout(route, cnt, n_exp)
        dest = dest.reshape(TOP_K * b * l)
        xs = _sc_scatter_rows(h2.reshape(b * l, d // 2), dest, n_rows)
        ys = _moe_experts(li, blk_e, nxt_e, n_used, xs, expert_w_gate_up, expert_b_gate_up,
                          expert_w_down, expert_b_down)
        moe = (_sc_gather_rows(ys, dest).reshape(TOP_K, b, l, d // 2), route)
    return _final(depth - 1, x_all, moe[0], moe[1], modsel, lc, final_norm_g.reshape(1, d))
```

```python
import functools

import numpy as np
import jax
import jax.numpy as jnp
from jax import lax
from jax.experimental import pallas as pl
from jax.experimental.pallas import tpu as pltpu
from jax.experimental.pallas import tpu_sc as plsc

GRID_W = 64
EPS = 1e-6
ROPE_BASE = 10000.0
NEG_INF = -1e30
LOG2E = 1.4426950408889634
LANES = 128

NA_HEADS = 8
NA_HEAD_DIM = 64
NA_KH = 8
NA_KW = 16
NA_QROWS = 4
NA_SLAB = 12
MLA_HEADS = 8
MLA_NOPE = 64
MLA_ROPE = 32
MLA_V = 64
MLA_KEY_PARTS = 4
MLA_Q_LORA = 256
MLA_KV_LORA = 128
SWA_HEADS = 8
SWA_KV_HEADS = 2
SWA_HEAD_DIM = 64
SWA_WINDOW = 128
N_BRANCH = 3
BRANCH_W = 512
N_EXPERTS = 32
TOP_K = 4
SWIGLU_ALPHA = 1.702
SWIGLU_LIMIT = 7.0

TM = 256
MOE_TM = 512
SC_CORES = 2
SC_SUBCORES = 16
SC_SCATTER_ROWS = 64
SC_GATHER_ROWS = 64
VMEM_LIMIT = 56 * 1024 * 1024

BF16 = jnp.bfloat16
F32 = jnp.float32


def _dot(a, b):
    return jnp.dot(a, b, preferred_element_type=F32)


def _dot_nt(a, b):
    return lax.dot_general(a, b, (((1,), (1,)), ((), ())), preferred_element_type=F32)


def _params(n_axes, vmem=VMEM_LIMIT):
    return pltpu.CompilerParams(dimension_semantics=("arbitrary",) * n_axes, vmem_limit_bytes=vmem)


def _layer_spec(a, li):
    return pl.BlockSpec((None,) + a.shape[1:], lambda *_: (li,) + (0,) * (a.ndim - 1))


def _mod_spec(li, off=0):
    return lambda d: pl.BlockSpec((None, 1, 1, 6, d), lambda i, j: (li, i, jnp.minimum(j + off, 1), 0, 0))


def _rms(x):
    return x * lax.rsqrt(jnp.mean(x * x, axis=-1, keepdims=True) + EPS)


def _sigmoid(x):
    return 1.0 / (1.0 + jnp.exp(-x))


def _pack_rows(xb):
    half = xb.shape[1] // 2
    lo = pltpu.bitcast(xb[:, :half].astype(F32), jnp.int32)
    hi = pltpu.bitcast(xb[:, half:].astype(F32), jnp.int32)
    return (hi & jnp.int32(-65536)) | lax.shift_right_logical(lo, 16)


def _unpack_rows(w):
    lo = pltpu.bitcast(lax.shift_left(w, 16), F32).astype(BF16)
    hi = pltpu.bitcast(w & jnp.int32(-65536), F32).astype(BF16)
    return jnp.concatenate([lo, hi], axis=1)


def _mod_kernel(c_ref, w_ref, b_ref, o_ref):
    c = c_ref[...]
    s = (c * _sigmoid(c)).astype(BF16)
    o_ref[0] = _dot(s, w_ref[0].astype(BF16)) + b_ref[0]


def _modulation(cvec, w_mod, b_mod):
    depth, d, n6 = w_mod.shape
    tn = n6 // 4
    return pl.pallas_call(
        _mod_kernel,
        out_shape=jax.ShapeDtypeStruct((depth, 8, n6), F32),
        grid=(depth, n6 // tn),
        in_specs=[pl.BlockSpec((8, d), lambda l, j: (0, 0)),
                  pl.BlockSpec((1, d, tn), lambda l, j: (l, 0, j)),
                  pl.BlockSpec((1, 1, tn), lambda l, j: (l, 0, j))],
        out_specs=pl.BlockSpec((1, 8, tn), lambda l, j: (l, 0, j)),
        compiler_params=_params(2),
        name="modulation",
    )(cvec, w_mod, b_mod.reshape(depth, 1, n6))


def _rope_groups(x, tab_ref, shift):
    cos, sdn, sup = tab_ref[0], tab_ref[1], tab_ref[2]
    outs = []
    for g in range(x.shape[1] // LANES):
        xg = x[:, g * LANES:(g + 1) * LANES]
        outs.append(xg * cos + pltpu.roll(xg, LANES - shift, 1) * sdn + pltpu.roll(xg, shift, 1) * sup)
    return outs[0] if len(outs) == 1 else jnp.concatenate(outs, axis=1)


def _inproj_kernel(*refs, with_moe):
    if with_moe:
        x_ref, y4_ref, route_ref, modp_ref = refs[:4]
        refs = refs[4:]
        (mod_ref, g1_ref, w_ref, wuq_ref, wukv_ref, gq_ref, gkv_ref, rb_ref, rc_ref, xo_ref,
         aq_ref, ak_ref, av_ref, bq_ref, bk_ref, bv_ref, cq_ref, ck_ref, cv_ref, gate_ref) = refs
        x = x_ref[0] + modp_ref[0, 0][5:6] * _combine(y4_ref, route_ref)
        xo_ref[0] = x
    else:
        (c_ref, x_ref, mod_ref, g1_ref, w_ref, wuq_ref, wukv_ref, gq_ref, gkv_ref, rb_ref, rc_ref, xo_ref,
         aq_ref, ak_ref, av_ref, bq_ref, bk_ref, bv_ref, cq_ref, ck_ref, cv_ref, gate_ref) = refs
        x = jnp.where(pl.program_id(1) == 0, c_ref[0], x_ref[0])
        xo_ref[0] = x
    mod = mod_ref[0, 0]
    h = (_rms(x) * g1_ref[...]) * (1.0 + mod[1:2]) + mod[0:1]
    hb = h.astype(BF16)
    acc_b = _dot(hb, w_ref[:, 1536:2048])
    acc_a = _dot(hb, w_ref[:, 0:1536])
    qn = (_rms(acc_b[:, 0:256]) * gq_ref[...]).astype(BF16)
    kvn = (_rms(acc_b[:, 256:384]) * gkv_ref[...]).astype(BF16)
    q = _dot(qn, wuq_ref[...])
    kv = _dot(kvn, wukv_ref[...])
    acc_c = _dot(hb, w_ref[:, 2048:3072])
    aq_ref[0] = (acc_a[:, 0:512] * (NA_HEAD_DIM ** -0.5 * LOG2E)).astype(BF16)
    ak_ref[0] = acc_a[:, 512:1024].astype(BF16)
    av_ref[0] = acc_a[:, 1024:1536].T.astype(BF16)
    acc_g = _dot(hb, w_ref[:, 3072:])
    kr = _rope_groups(acc_b[:, 384:512], rb_ref, MLA_ROPE // 2)
    q = _rope_groups(q, rb_ref, MLA_ROPE // 2)
    bq_ref[0] = (q * ((MLA_NOPE + MLA_ROPE) ** -0.5 * LOG2E)).astype(BF16)
    bk_ref[0] = (kv[:, 0:1024] + jnp.concatenate([kr] * MLA_HEADS, axis=1)).astype(BF16)
    bv_ref[0] = kv[:, 1024:1536].T.astype(BF16)
    cq_ref[0] = (_rope_groups(acc_c[:, 0:512], rc_ref, SWA_HEAD_DIM // 2) * (SWA_HEAD_DIM ** -0.5 * LOG2E)).astype(BF16)
    ck_ref[0] = _rope_groups(acc_c[:, 512:768], rc_ref, SWA_HEAD_DIM // 2).astype(BF16)
    cv = jnp.concatenate([acc_c[:, 768:832], acc_c[:, 896:960]], axis=1)
    cv_ref[0] = cv.T.astype(BF16)
    gate_ref[0] = _sigmoid(acc_g).astype(BF16)


def _inproj(li, x_in, moe, modsel, g1, w, wuq, wukv, gq, gkv, rope_b, rope_c):
    row = lambda width: pl.BlockSpec((1, TM, width), lambda i, j: (i, j, 0))
    full = lambda a: _layer_spec(a, li)
    if moe is None:
        ctx, x = x_in
        b, l, d = x.shape[0], ctx.shape[1] + x.shape[1], x.shape[2]
        in_specs = [pl.BlockSpec((1, TM, d), lambda i, j: (i, 0, 0)),
                    pl.BlockSpec((1, TM, d), lambda i, j: (i, jnp.maximum(j - 1, 0), 0))]
        args = [ctx, x]
    else:
        b, l, d = x_in.shape
        y4, route = moe
        in_specs = [row(d), pl.BlockSpec((TOP_K, 1, TM, y4.shape[3]), lambda i, j: (0, i, j, 0)), row(LANES),
                    _mod_spec(li - 1)(d)]
        args = [x_in, y4, route, modsel]
    nb = l // TM
    widths = (512, 512, 512, 1024, 1024, 512, 512, 256, SWA_KV_HEADS * SWA_HEAD_DIM, N_BRANCH * d)
    out_specs = [row(wd) for wd in widths]
    out_shape = [jax.ShapeDtypeStruct((b, l, wd), BF16) for wd in widths]
    for pos in (2, 5, 8):
        out_specs[pos] = pl.BlockSpec((1, widths[pos], TM), lambda i, j: (i, 0, j))
        out_shape[pos] = jax.ShapeDtypeStruct((b, widths[pos], l), BF16)
    out_specs = [row(d)] + out_specs
    out_shape = [jax.ShapeDtypeStruct((b, l, d), F32)] + out_shape
    in_specs += [_mod_spec(li)(d), full(g1), full(w), full(wuq), full(wukv), full(gq), full(gkv),
                 pl.BlockSpec((3, TM, LANES), lambda i, j: (0, j, 0)),
                 pl.BlockSpec((3, TM, LANES), lambda i, j: (0, j, 0))]
    args += [modsel, g1, w, wuq, wukv, gq, gkv, rope_b, rope_c]
    return pl.pallas_call(
        functools.partial(_inproj_kernel, with_moe=moe is not None),
        out_shape=out_shape,
        grid=(b, nb),
        in_specs=in_specs,
        out_specs=out_specs,
        compiler_params=_params(2),
        name="inproj",
    )(*args)


def _lane_lo():
    return lax.broadcasted_iota(jnp.int32, (1, LANES), 1) < (LANES // 2)


def _split_heads(qp, lo):
    zero = jnp.zeros_like(qp)
    return jnp.where(lo, qp, zero), jnp.where(lo, zero, qp)


def _softmax_pv_t(score_parts, vt_parts, extra_logit=None):
    m = score_parts[0].max(axis=0, keepdims=True)
    for s in score_parts[1:]:
        m = jnp.maximum(m, s.max(axis=0, keepdims=True))
    if extra_logit is not None:
        m = jnp.maximum(m, extra_logit)
    den = None
    acc = None
    for s, vt in zip(score_parts, vt_parts):
        e = jnp.exp2(s - m)
        d = e.sum(axis=0, keepdims=True)
        den = d if den is None else den + d
        o = _dot(vt, e.astype(BF16))
        acc = o if acc is None else acc + o
    if extra_logit is not None:
        den = den + jnp.exp2(extra_logit - m)
    return acc / den


def _heads_ahead(n_heads, ahead, scores, finish):
    pending = [scores(h) for h in range(ahead)]
    outs = []
    for h in range(n_heads):
        if h + ahead < n_heads:
            pending.append(scores(h + ahead))
        outs.append(finish(h, pending.pop(0)))
    return outs


def _na_kernel(q_ref, k_ref, vt_ref, bias_ref, o_ref, *, ctx, rows):
    j = pl.program_id(1)
    lo = _lane_lo()

    def head_q(h):
        sl = slice((h // 2) * LANES, (h // 2 + 1) * LANES)
        return _split_heads(q_ref[0, :, sl], lo)[h % 2], sl

    def vt(h, cols):
        return vt_ref[0, h * NA_HEAD_DIM:(h + 1) * NA_HEAD_DIM, cols]

    @pl.when(j == 0)
    def _():
        def scores(h):
            qm, sl = head_q(h)
            return [_dot_nt(k_ref[0, 0:ctx, sl], qm)]

        outs = _heads_ahead(NA_HEADS, 2, scores, lambda h, s: _softmax_pv_t(s, [vt(h, slice(0, ctx))]))
        o_ref[0] = jnp.concatenate(outs, axis=0).T.astype(BF16)

    @pl.when(j > 0)
    def _():
        r = (j - 1) * NA_QROWS
        s0 = jnp.clip(r - NA_KH // 2, 0, rows - NA_SLAB)
        slab = pl.ds(pl.multiple_of(ctx + s0 * GRID_W, NA_QROWS * GRID_W), NA_SLAB * GRID_W)

        def scores(h):
            qm, sl = head_q(h)
            return [_dot_nt(k_ref[0, 0:ctx, sl], qm), _dot_nt(k_ref[0, slab, sl], qm) + bias_ref[0, h]]

        outs = _heads_ahead(NA_HEADS, 2, scores,
                            lambda h, s: _softmax_pv_t(s, [vt(h, slice(0, ctx)), vt(h, slab)]))
        o_ref[0] = jnp.concatenate(outs, axis=0).T.astype(BF16)


def _na_attention(li, aq, ak, avt, bias, ctx):
    b, l, w = aq.shape
    assert ((l - ctx) // GRID_W) % NA_QROWS == 0 and ctx % (NA_QROWS * GRID_W) == 0
    nb = l // TM
    rows = (l - ctx) // GRID_W
    last = nb - 1

    def bias_map(i, j):
        return (li, jnp.where(j <= 1, 0, jnp.where(j == last, 2, 1)), 0, 0, 0)

    return pl.pallas_call(
        functools.partial(_na_kernel, ctx=ctx, rows=rows),
        out_shape=jax.ShapeDtypeStruct((b, l, w), BF16),
        grid=(b, nb),
        in_specs=[pl.BlockSpec((1, TM, w), lambda i, j: (i, j, 0)),
                  pl.BlockSpec((1, l, w), lambda i, j: (i, 0, 0)),
                  pl.BlockSpec((1, w, l), lambda i, j: (i, 0, 0)),
                  pl.BlockSpec((None, 1, NA_HEADS, NA_SLAB * GRID_W, TM), bias_map)],
        out_specs=pl.BlockSpec((1, TM, w), lambda i, j: (i, j, 0)),
        compiler_params=_params(2),
        name="na_attention",
    )(aq, ak, avt, bias)


def _na_bias_tables(na_rpb, rows):
    depth = na_rpb.shape[0]
    qc = np.arange(GRID_W)[:, None]
    kc = np.arange(GRID_W)[None, :]
    c0 = np.clip(qc - NA_KW // 2, 0, GRID_W - NA_KW)
    col_ok = (kc >= c0) & (kc < c0 + NA_KW)
    col_idx = np.clip(kc - qc + NA_KW - 1, 0, 2 * NA_KW - 2)
    n_off = 2 * NA_KH - 1
    bc = jnp.take(na_rpb, jnp.asarray(col_idx.reshape(-1)), axis=3).reshape(depth, NA_HEADS, n_off, GRID_W, GRID_W)
    bc = jnp.where(jnp.asarray(col_ok), bc * LOG2E, NEG_INF)
    bc = jnp.concatenate([bc, jnp.full((depth, NA_HEADS, 1, GRID_W, GRID_W), NEG_INF, F32)], axis=2)
    bc = bc.swapaxes(-1, -2)
    i_idx = np.zeros((3, NA_SLAB, NA_QROWS), np.int32)
    for case, (r, s) in enumerate(((0, 0), (NA_KH // 2, 0), (rows - NA_QROWS, rows - NA_SLAB))):
        for a in range(NA_QROWS):
            qr = r + a
            r0 = min(max(qr - NA_KH // 2, 0), rows - NA_KH)
            for c in range(NA_SLAB):
                kr = s + c
                i_idx[case, c, a] = kr - qr + NA_KH - 1 if r0 <= kr < r0 + NA_KH else n_off
    t = jnp.take(bc, jnp.asarray(i_idx.reshape(-1)), axis=2).reshape(
        depth, NA_HEADS, 3, NA_SLAB, NA_QROWS, GRID_W, GRID_W)
    t = t.transpose(0, 2, 1, 3, 5, 4, 6)
    return t.reshape(depth, 3, NA_HEADS, NA_SLAB * GRID_W, NA_QROWS * GRID_W)


def _mla_kernel(q_ref, k_ref, vt_ref, o_ref, *, ctx):
    j = pl.program_id(1)

    def run(nkeys):
        tiles = nkeys // 256
        nparts = min(MLA_KEY_PARTS, tiles)
        cuts = [(tiles * p // nparts) * 256 for p in range(nparts + 1)]
        parts = list(zip(cuts[:-1], cuts[1:]))

        def scores(h):
            hsl = slice(h * LANES, (h + 1) * LANES)
            return [_dot_nt(k_ref[0, a:b, hsl], q_ref[0, :, hsl]) for a, b in parts]

        def finish(h, s_parts):
            return _softmax_pv_t(s_parts, [vt_ref[0, h * MLA_V:(h + 1) * MLA_V, a:b] for a, b in parts])

        outs = _heads_ahead(MLA_HEADS, 3, scores, finish)
        o_ref[0] = jnp.concatenate(outs, axis=0).T.astype(BF16)

    @pl.when(j == 0)
    def _():
        run(ctx)

    @pl.when(j > 0)
    def _():
        run(k_ref.shape[1])


def _mla_attention(bq, bk, bvt, ctx):
    b, l, wq = bq.shape
    wv = bvt.shape[1]
    return pl.pallas_call(
        functools.partial(_mla_kernel, ctx=ctx),
        out_shape=jax.ShapeDtypeStruct((b, l, wv), BF16),
        grid=(b, l // TM),
        in_specs=[pl.BlockSpec((1, TM, wq), lambda i, j: (i, j, 0)),
                  pl.BlockSpec((1, l, wq), lambda i, j: (i, 0, 0)),
                  pl.BlockSpec((1, wv, l), lambda i, j: (i, 0, 0))],
        out_specs=pl.BlockSpec((1, TM, wv), lambda i, j: (i, j, 0)),
        compiler_params=_params(2),
        name="mla_attention",
    )(bq, bk, bvt)


def _swa_kernel(sink_ref, q_ref, k_ref, vt_ref, o_ref, *, ctx, n_lat):
    j = pl.program_id(1)
    lo = _lane_lo()
    group = SWA_HEADS // SWA_KV_HEADS
    band = TM + 2 * SWA_WINDOW

    def head_q(h):
        sl = slice((h // 2) * LANES, (h // 2 + 1) * LANES)
        return _split_heads(q_ref[0, :, sl], lo)[h % 2]

    def kv_lanes(h):
        return slice((h // group) * LANES, (h // group + 1) * LANES)

    def vt(h, cols):
        kv = h // group
        return vt_ref[0, kv * SWA_HEAD_DIM:(kv + 1) * SWA_HEAD_DIM, cols]

    def sink(h):
        return jnp.full((1, 1), sink_ref[h] * LOG2E, F32)

    @pl.when(j == 0)
    def _():
        outs = _heads_ahead(
            SWA_HEADS, 2, lambda h: [_dot_nt(k_ref[0, 0:ctx, kv_lanes(h)], head_q(h))],
            lambda h, s: _softmax_pv_t(s, [vt(h, slice(0, ctx))], extra_logit=sink(h)))
        o_ref[0] = jnp.concatenate(outs, axis=0).T.astype(BF16)

    @pl.when(j > 0)
    def _():
        q0 = (j - 1) * TM
        s0 = jnp.clip(q0 - SWA_WINDOW, 0, n_lat - band)
        rows = pl.ds(pl.multiple_of(ctx + s0, SWA_WINDOW), band)
        kpos = s0 + lax.broadcasted_iota(jnp.int32, (band, TM), 0)
        qpos = q0 + lax.broadcasted_iota(jnp.int32, (band, TM), 1)
        keep = jnp.abs(qpos - kpos) <= SWA_WINDOW

        def scores(h):
            qm = head_q(h)
            return [_dot_nt(k_ref[0, 0:ctx, kv_lanes(h)], qm),
                    jnp.where(keep, _dot_nt(k_ref[0, rows, kv_lanes(h)], qm), NEG_INF)]

        outs = _heads_ahead(
            SWA_HEADS, 2, scores,
            lambda h, s: _softmax_pv_t(s, [vt(h, slice(0, ctx)), vt(h, rows)], extra_logit=sink(h)))
        o_ref[0] = jnp.concatenate(outs, axis=0).T.astype(BF16)


def _swa_attention(sink, cq, ck, cvt, ctx):
    b, l, w = cq.shape
    wk = ck.shape[2]
    wv = cvt.shape[1]
    grid_spec = pltpu.PrefetchScalarGridSpec(
        num_scalar_prefetch=1,
        grid=(b, l // TM),
        in_specs=[pl.BlockSpec((1, TM, w), lambda i, j, s: (i, j, 0)),
                  pl.BlockSpec((1, l, wk), lambda i, j, s: (i, 0, 0)),
                  pl.BlockSpec((1, wv, l), lambda i, j, s: (i, 0, 0))],
        out_specs=pl.BlockSpec((1, TM, w), lambda i, j, s: (i, j, 0)),
    )
    return pl.pallas_call(
        functools.partial(_swa_kernel, ctx=ctx, n_lat=l - ctx),
        out_shape=jax.ShapeDtypeStruct((b, l, w), BF16),
        grid_spec=grid_spec,
        compiler_params=_params(2),
        name="swa_attention",
    )(sink, cq, ck, cvt)


def _merge_kernel(oa_ref, ob_ref, oc_ref, gate_ref, x_ref, mod_ref, wbr_ref, wout_ref, g2_ref,
                  rw_ref, rb_ref, xo_ref, h2_ref, route_ref, cnt_ref, run_ref, *, n_exp):
    d = x_ref.shape[2]

    @pl.when(jnp.logical_and(pl.program_id(0) == 0, pl.program_id(1) == 0))
    def _():
        run_ref[...] = jnp.zeros_like(run_ref)

    mod = mod_ref[0, 0]
    mix = None
    for i, o_ref in enumerate((oa_ref, ob_ref, oc_ref)):
        t = gate_ref[0, :, i * d:(i + 1) * d].astype(F32) * _dot(o_ref[0], wbr_ref[i])
        mix = t if mix is None else mix + t
    y = _dot(mix.astype(BF16), wout_ref[...])
    x = x_ref[0] + mod[2:3] * y
    xo_ref[0] = x
    h2 = (_rms(x) * g2_ref[...]) * (1.0 + mod[4:5]) + mod[3:4]
    h2b = h2.astype(BF16)
    h2_ref[0] = _pack_rows(h2b)
    logits = _dot(h2b, rw_ref[...]) + rb_ref[...]
    lane = lax.broadcasted_iota(jnp.int32, logits.shape, 1).astype(F32)
    work = jnp.where(lane < n_exp, logits, -jnp.inf)
    ids, vals = [], []
    for _ in range(TOP_K):
        m = work.max(axis=-1, keepdims=True)
        idx = jnp.where(work == m, lane, float(LANES)).min(axis=-1, keepdims=True)
        ids.append(idx)
        vals.append(m)
        work = jnp.where(lane == idx, -jnp.inf, work)
    ex = [jnp.exp(v - vals[0]) for v in vals]
    den = ex[0] + ex[1] + ex[2] + ex[3]
    hits = jnp.zeros(logits.shape, F32)
    for idx in ids:
        hits = hits + jnp.where(lane == idx, 1.0, 0.0)
    r = lax.broadcasted_iota(jnp.int32, (TM, TM), 0)
    c = lax.broadcasted_iota(jnp.int32, (TM, TM), 1)
    tri = jnp.where(c < r, 1.0, 0.0).astype(BF16)
    before = _dot(tri, hits.astype(BF16)) + run_ref[0:1]
    route = jnp.zeros(logits.shape, F32)
    for k in range(TOP_K):
        rank = jnp.where(lane == ids[k], before, 0.0).sum(axis=-1, keepdims=True)
        route = jnp.where(lane == k, ids[k], route)
        route = jnp.where(lane == TOP_K + k, ex[k] / den, route)
        route = jnp.where(lane == 2 * TOP_K + k, rank, route)
    route_ref[0] = route
    run_ref[...] = run_ref[...] + hits.sum(axis=0, keepdims=True)
    cnt_ref[...] = run_ref[...]


def _merge(li, oa, ob, oc, gate, x_all, modsel, wbr, wout, g2, rw, rb, n_exp):
    b, l, d = x_all.shape
    row = lambda width: pl.BlockSpec((1, TM, width), lambda i, j: (i, j, 0))
    full = lambda a: _layer_spec(a, li)
    return pl.pallas_call(
        functools.partial(_merge_kernel, n_exp=n_exp),
        out_shape=[jax.ShapeDtypeStruct((b, l, d), F32),
                   jax.ShapeDtypeStruct((b, l, d // 2), jnp.int32),
                   jax.ShapeDtypeStruct((b, l, LANES), F32),
                   jax.ShapeDtypeStruct((8, LANES), F32)],
        grid=(b, l // TM),
        in_specs=[row(BRANCH_W), row(BRANCH_W), row(BRANCH_W), row(N_BRANCH * d), row(d), _mod_spec(li)(d),
                  full(wbr), full(wout), full(g2), full(rw), full(rb)],
        out_specs=[row(d), row(d // 2), row(LANES), pl.BlockSpec((8, LANES), lambda i, j: (0, 0))],
        scratch_shapes=[pltpu.VMEM((8, LANES), F32)],
        compiler_params=_params(2),
        name="merge",
    )(oa, ob, oc, gate, x_all, modsel, wbr, wout, g2, rw, rb)


def _moe_kernel(be_ref, nx_ref, nu_ref, x_ref, wgu_hbm, bgu_ref, wdn_hbm, bdn_ref, y_ref,
                gu_stage, dn_stage, wgu_s, wdn_s, sem, *, li):
    i = pl.program_id(0)
    ff = wdn_s.shape[0]

    def fetch(e):
        return (pltpu.make_async_copy(wgu_hbm.at[li, e], gu_stage, sem.at[0]),
                pltpu.make_async_copy(wdn_hbm.at[li, e], dn_stage, sem.at[1]))

    @pl.when(i < nu_ref[0])
    def _():
        e = be_ref[i]

        @pl.when(i == 0)
        def _():
            for copy in fetch(e):
                copy.start()

        @pl.when(jnp.logical_or(i == 0, e != be_ref[jnp.maximum(i - 1, 0)]))
        def _():
            for copy in fetch(e):
                copy.wait()
            wgu_s[...] = gu_stage[...].astype(BF16)
            wdn_s[...] = dn_stage[...].astype(BF16)
            nxt = nx_ref[i]

            @pl.when(nxt >= 0)
            def _():
                for copy in fetch(nxt):
                    copy.start()

        gu = _dot(_unpack_rows(x_ref[...]), wgu_s[...]) + bgu_ref[0]
        glu = jnp.minimum(gu[:, :ff], SWIGLU_LIMIT)
        lin = jnp.clip(gu[:, ff:], -SWIGLU_LIMIT, SWIGLU_LIMIT)
        act = glu * _sigmoid(SWIGLU_ALPHA * glu) * (lin + 1.0)
        y_ref[...] = _pack_rows((_dot(act.astype(BF16), wdn_s[...]) + bdn_ref[0]).astype(BF16))


def _moe_experts(li, blk_e, nxt_e, n_used, xs, w_gu, b_gu, w_dn, b_dn):
    n_rows, packed_w = xs.shape
    depth, n_exp, d, ff2 = w_gu.shape
    ff = ff2 // 2
    n_blocks = n_rows // MOE_TM

    def row_map(i, be, nx, nu):
        return (jnp.minimum(i, nu[0] - 1), 0)

    def b_map(i, be, nx, nu):
        return (li, be[jnp.minimum(i, nu[0] - 1)], 0, 0)

    grid_spec = pltpu.PrefetchScalarGridSpec(
        num_scalar_prefetch=3,
        grid=(n_blocks,),
        in_specs=[pl.BlockSpec((MOE_TM, packed_w), row_map),
                  pl.BlockSpec(memory_space=pl.ANY),
                  pl.BlockSpec((None, 1, 1, ff2), b_map),
                  pl.BlockSpec(memory_space=pl.ANY),
                  pl.BlockSpec((None, 1, 1, d), b_map)],
        out_specs=pl.BlockSpec((MOE_TM, packed_w), row_map),
        scratch_shapes=[pltpu.VMEM((d, ff2), F32), pltpu.VMEM((ff, d), F32),
                        pltpu.VMEM((d, ff2), BF16), pltpu.VMEM((ff, d), BF16),
                        pltpu.SemaphoreType.DMA((2,))],
    )
    return pl.pallas_call(
        functools.partial(_moe_kernel, li=li),
        out_shape=jax.ShapeDtypeStruct((n_rows, packed_w), jnp.int32),
        grid_spec=grid_spec,
        compiler_params=_params(1),
        name="moe_experts",
    )(blk_e, nxt_e, n_used, xs, w_gu, b_gu.reshape(depth, n_exp, 1, ff2), w_dn, b_dn.reshape(depth, n_exp, 1, d))


def _layout(route, cnt, n_exp):
    b, l, _ = route.shape
    t = b * l
    ids = route[..., 0:TOP_K].astype(jnp.int32)
    rank = route[..., 2 * TOP_K:3 * TOP_K].astype(jnp.int32)
    counts = cnt[0, :n_exp].astype(jnp.int32)
    padded = (counts + MOE_TM - 1) // MOE_TM * MOE_TM
    pad_end = jnp.cumsum(padded)
    pad_start = pad_end - padded
    onehot = ids[..., None] == jnp.arange(n_exp, dtype=jnp.int32)
    dest = jnp.sum(jnp.where(onehot, pad_start, 0), axis=-1) + rank
    dest = dest.reshape(t, TOP_K).T
    n_blocks = -(-t * TOP_K // MOE_TM) + n_exp
    blk_start = jnp.arange(n_blocks, dtype=jnp.int32) * MOE_TM
    blk_e = jnp.minimum(jnp.sum(blk_start[:, None] >= pad_end[None, :], axis=1), n_exp - 1).astype(jnp.int32)
    n_used = (pad_end[-1] // MOE_TM).astype(jnp.int32).reshape(1)
    ar = jnp.arange(n_exp, dtype=jnp.int32)
    later = jnp.logical_and(ar[None, :] > ar[:, None], (counts > 0)[None, :])
    nxt_of = jnp.min(jnp.where(later, ar[None, :], n_exp), axis=1)
    nxt_of = jnp.where(nxt_of == n_exp, -1, nxt_of)
    nxt_e = jnp.sum(jnp.where(blk_e[:, None] == ar[None, :], nxt_of[None, :], 0), axis=1).astype(jnp.int32)
    return dest, blk_e, nxt_e, n_used, n_blocks * MOE_TM


def _sc_mesh():
    return plsc.VectorSubcoreMesh(core_axis_name="c", subcore_axis_name="s",
                                  num_cores=SC_CORES, num_subcores=SC_SUBCORES)


def _sc_worker():
    return lax.axis_index("s") * SC_CORES + lax.axis_index("c")


def _sc_scatter_rows(x, idx, n_rows):
    t, d = x.shape
    n_idx = idx.shape[0]
    workers = SC_CORES * SC_SUBCORES
    n_chunks = n_idx // (workers * SC_SCATTER_ROWS)
    assert n_chunks * workers * SC_SCATTER_ROWS == n_idx and t % SC_SCATTER_ROWS == 0 and n_chunks % 2 == 0
    idx3 = idx.reshape(workers, n_chunks, SC_SCATTER_ROWS)

    @pl.kernel(out_type=jax.ShapeDtypeStruct((n_rows, d), x.dtype), mesh=_sc_mesh(),
               scratch_types=[pltpu.VMEM((n_chunks, SC_SCATTER_ROWS), jnp.int32),
                              pltpu.VMEM((SC_SCATTER_ROWS, d), x.dtype),
                              pltpu.VMEM((SC_SCATTER_ROWS, d), x.dtype),
                              pltpu.SemaphoreType.DMA, pltpu.SemaphoreType.DMA, pltpu.SemaphoreType.DMA])
    def scatter_kernel(x_hbm, i_hbm, o_hbm, idx_v, buf0, buf1, sem0, sem1, out_sem):
        wid = _sc_worker()
        pltpu.sync_copy(i_hbm.at[wid], idx_v)
        bufs, sems = (buf0, buf1), (sem0, sem1)

        def load(chunk, b):
            src = pl.multiple_of(((wid * n_chunks + chunk) * SC_SCATTER_ROWS) % t, SC_SCATTER_ROWS)
            return pltpu.make_async_copy(x_hbm.at[pl.ds(src, SC_SCATTER_ROWS)], bufs[b], sems[b])

        load(0, 0).start()

        @pl.loop(0, n_chunks, step=2)
        def _(j):
            for b in range(2):
                chunk = j + b
                load(chunk, b).wait()
                if b == 0:
                    load(chunk + 1, 1).start()
                else:
                    @pl.when(chunk + 1 < n_chunks)
                    def _():
                        load(chunk + 1, 0).start()
                pltpu.async_copy(bufs[b], o_hbm.at[idx_v.at[chunk]], out_sem).wait()

    return scatter_kernel(x, idx3)


def _sc_gather_rows(x, idx):
    d = x.shape[1]
    n_idx = idx.shape[0]
    workers = SC_CORES * SC_SUBCORES
    per_worker = n_idx // workers
    n_chunks = per_worker // SC_GATHER_ROWS
    assert n_chunks * workers * SC_GATHER_ROWS == n_idx

    assert n_chunks % 2 == 0

    @pl.kernel(out_type=jax.ShapeDtypeStruct((n_idx, d), x.dtype), mesh=_sc_mesh(),
               scratch_types=[pltpu.VMEM((per_worker,), jnp.int32),
                              pltpu.VMEM((SC_GATHER_ROWS, d), x.dtype),
                              pltpu.VMEM((SC_GATHER_ROWS, d), x.dtype),
                              pltpu.SemaphoreType.DMA, pltpu.SemaphoreType.DMA])
    def gather_kernel(x_hbm, i_hbm, o_hbm, idx_v, buf0, buf1, sem0, sem1):
        base = _sc_worker() * per_worker
        pltpu.sync_copy(i_hbm.at[pl.ds(base, per_worker)], idx_v)
        bufs, sems = (buf0, buf1), (sem0, sem1)

        def gather(chunk, b):
            off = pl.multiple_of(chunk * SC_GATHER_ROWS, SC_GATHER_ROWS)
            return pltpu.make_async_copy(x_hbm.at[idx_v.at[pl.ds(off, SC_GATHER_ROWS)]], bufs[b], sems[b])

        gather(0, 0).start()

        @pl.loop(0, n_chunks, step=2)
        def _(j):
            for b in range(2):
                chunk = j + b
                gather(chunk, b).wait()
                if b == 0:
                    gather(chunk + 1, 1).start()
                else:
                    @pl.when(chunk + 1 < n_chunks)
                    def _():
                        gather(chunk + 1, 0).start()
                off = pl.multiple_of(chunk * SC_GATHER_ROWS, SC_GATHER_ROWS)
                pltpu.sync_copy(bufs[b], o_hbm.at[pl.ds(base + off, SC_GATHER_ROWS)])

    return gather_kernel(x, idx)


def _combine(y4_ref, route_ref):
    route = route_ref[0]
    y = None
    for k in range(TOP_K):
        t = route[:, TOP_K + k:TOP_K + k + 1] * _unpack_rows(y4_ref[k, 0]).astype(F32)
        y = t if y is None else y + t
    return y


def _final_kernel(x_ref, y4_ref, route_ref, mod_ref, g_ref, o_ref):
    x = x_ref[0] + mod_ref[0, 0][5:6] * _combine(y4_ref, route_ref)
    o_ref[0] = _rms(x) * g_ref[...]


def _final(li, x_all, y4, route, modsel, skip, final_g):
    b, l, d = x_all.shape
    off = skip // TM
    lo = l - skip
    return pl.pallas_call(
        _final_kernel, out_shape=jax.ShapeDtypeStruct((b, lo, d), F32), grid=(b, lo // TM),
        in_specs=[pl.BlockSpec((1, TM, d), lambda i, j: (i, j + off, 0)),
                  pl.BlockSpec((TOP_K, 1, TM, y4.shape[3]), lambda i, j: (0, i, j + off, 0)),
                  pl.BlockSpec((1, TM, LANES), lambda i, j: (i, j + off, 0)),
                  _mod_spec(li, off)(d),
                  pl.BlockSpec((1, d), lambda i, j: (0, 0))],
        out_specs=pl.BlockSpec((1, TM, d), lambda i, j: (i, j, 0)),
        compiler_params=_params(2), name="final_norm",
    )(x_all, y4, route, modsel, final_g)


def _rope_tables(n, ctx, rot_dim, group_pattern):
    t = np.arange(n, dtype=np.int32)
    row = (t // GRID_W).astype(np.float32)
    col = (t % GRID_W).astype(np.float32)
    per_axis = rot_dim // 2
    inv = (np.float32(ROPE_BASE) ** (-np.arange(0, per_axis, 2, dtype=np.float32) / np.float32(per_axis))).astype(np.float32)
    ang = np.concatenate([row[:, None] * inv[None], col[:, None] * inv[None]], axis=-1).astype(np.float32)
    cos, sin = np.cos(ang).astype(np.float32), np.sin(ang).astype(np.float32)
    half = rot_dim // 2
    tab = np.zeros((3, ctx + n, LANES), np.float32)
    tab[0] = 1.0
    for off in group_pattern:
        tab[0, ctx:, off:off + half] = cos
        tab[0, ctx:, off + half:off + rot_dim] = cos
        tab[1, ctx:, off:off + half] = -sin
        tab[2, ctx:, off + half:off + rot_dim] = sin
    return jnp.asarray(tab)


def _wprep_kernel(w_ref, o_ref):
    x = w_ref[0]
    rows = x.shape[0]
    o = 3 * 512 + MLA_Q_LORA + MLA_KV_LORA
    zeros = lambda n: jnp.zeros((rows, n), F32)
    kr = jnp.concatenate([zeros(MLA_NOPE), x[:, o:o + MLA_ROPE], zeros(LANES - MLA_NOPE - MLA_ROPE)], axis=1)
    o += MLA_ROPE
    cq = x[:, o:o + 512]
    k0, k1 = x[:, o + 512:o + 576], x[:, o + 576:o + 640]
    v0, v1 = x[:, o + 640:o + 704], x[:, o + 704:o + 768]
    g = x[:, o + 768:]
    o_ref[0] = jnp.concatenate([x[:, :3 * 512 + MLA_Q_LORA + MLA_KV_LORA], kr, cq, k0, k0, k1, k1, v0, v0, v1, v1, g],
                               axis=1).astype(BF16)


def _wprep(w_in):
    depth, d, cols = w_in.shape
    out_cols = cols - MLA_ROPE + LANES + 2 * 128
    tr = 128
    return pl.pallas_call(
        _wprep_kernel,
        out_shape=jax.ShapeDtypeStruct((depth, d, out_cols), BF16),
        grid=(depth, d // tr),
        in_specs=[pl.BlockSpec((1, tr, cols), lambda l, i: (l, i, 0))],
        out_specs=pl.BlockSpec((1, tr, out_cols), lambda l, i: (l, i, 0)),
        compiler_params=_params(2),
        name="wprep",
    )(w_in)


def _prep_weights(w_in, mla_w_uq, mla_w_ukv):
    depth = w_in.shape[0]
    w = _wprep(w_in)
    uq = mla_w_uq.reshape(depth, MLA_Q_LORA, MLA_HEADS, MLA_NOPE + MLA_ROPE)
    uq = jnp.pad(uq, ((0, 0), (0, 0), (0, 0), (0, LANES - MLA_NOPE - MLA_ROPE)))
    uq = uq.reshape(depth, MLA_Q_LORA, MLA_HEADS * LANES).astype(BF16)
    ukv = mla_w_ukv.reshape(depth, MLA_KV_LORA, MLA_HEADS, MLA_NOPE + MLA_V)
    uk = jnp.pad(ukv[..., :MLA_NOPE], ((0, 0), (0, 0), (0, 0), (0, LANES - MLA_NOPE)))
    uk = uk.reshape(depth, MLA_KV_LORA, MLA_HEADS * LANES)
    uv = ukv[..., MLA_NOPE:].reshape(depth, MLA_KV_LORA, MLA_HEADS * MLA_V)
    ukv = jnp.concatenate([uk, uv], axis=-1).astype(BF16)
    return w, uq, ukv


def kernel(x, c, ctx, c_ctx, norm1_g, norm2_g, w_mod, b_mod, w_in, na_rpb, mla_q_norm_g, mla_kv_norm_g, mla_w_uq, mla_w_ukv, swa_sink, w_branch, w_out, router_w, router_b, expert_w_gate_up, expert_b_gate_up, expert_w_down, expert_b_down, final_norm_g):
    b, n, d = x.shape
    lc = ctx.shape[1]
    l = lc + n
    depth = w_in.shape[0]
    n_exp = router_w.shape[2]
    assert lc == TM and n % TM == 0 and TM == NA_QROWS * GRID_W

    cvec = jnp.concatenate([c, c_ctx[None], jnp.zeros((8 - b - 1, d), F32)], axis=0)
    mod = _modulation(cvec, w_mod, b_mod)
    mod_lat = mod[:, :b].reshape(depth, b, 1, 6, d)
    mod_ctx = jnp.broadcast_to(mod[:, b].reshape(depth, 1, 1, 6, d), (depth, b, 1, 6, d))
    modsel = jnp.concatenate([mod_ctx, mod_lat], axis=2)

    w_all, uq_all, ukv_all = _prep_weights(w_in, mla_w_uq, mla_w_ukv)
    wbr_all = w_branch.astype(BF16)
    wout_all = w_out.astype(BF16)
    rw_all = jnp.pad(router_w, ((0, 0), (0, 0), (0, LANES - n_exp))).astype(BF16)
    rb_all = jnp.pad(router_b, ((0, 0), (0, LANES - n_exp))).reshape(depth, 1, LANES)
    rope_b = _rope_tables(n, lc, MLA_ROPE, (MLA_NOPE,))
    rope_c = _rope_tables(n, lc, SWA_HEAD_DIM, (0, SWA_HEAD_DIM))
    bias_all = _na_bias_tables(na_rpb, n // GRID_W)

    g1_all = norm1_g.reshape(depth, 1, d)
    g2_all = norm2_g.reshape(depth, 1, d)
    gq_all = mla_q_norm_g.reshape(depth, 1, -1)
    gkv_all = mla_kv_norm_g.reshape(depth, 1, -1)

    x_all = (ctx, x)
    moe = None
    for li in range(depth):
        ms = modsel
        x_all, aq, ak, av, bq, bk, bv, cq, ck, cv, gate = _inproj(
            li, x_all, moe, ms, g1_all, w_all, uq_all, ukv_all, gq_all, gkv_all, rope_b, rope_c)
        oa = _na_attention(li, aq, ak, av, bias_all, lc)
        ob = _mla_attention(bq, bk, bv, lc)
        oc = _swa_attention(swa_sink[li], cq, ck, cv, lc)
        x_all, h2, route, cnt = _merge(li, oa, ob, oc, gate, x_all, ms, wbr_all, wout_all,
                                       g2_all, rw_all, rb_all, n_exp)
        dest, blk_e, nxt_e, n_used, n_rows = _layout(route, cnt, n_exp)
        dest = dest.reshape(TOP_K * b * l)
        xs = _sc_scatter_rows(h2.reshape(b * l, d // 2), dest, n_rows)
        ys = _moe_experts(li, blk_e, nxt_e, n_used, xs, expert_w_gate_up, expert_b_gate_up,
                          expert_w_down, expert_b_down)
        moe = (_sc_gather_rows(ys, dest).reshape(TOP_K, b, l, d // 2), route)
    return _final(depth - 1, x_all, moe[0], moe[1], modsel, lc, final_norm_g.reshape(1, d))
```

```python
import functools

import numpy as np
import jax
import jax.numpy as jnp
from jax import lax
from jax.experimental import pallas as pl
from jax.experimental.pallas import tpu as pltpu
from jax.experimental.pallas import tpu_sc as plsc

GRID_W = 64
EPS = 1e-6
ROPE_BASE = 10000.0
NEG_INF = -1e30
LOG2E = 1.4426950408889634
LANES = 128

NA_HEADS = 8
NA_HEAD_DIM = 64
NA_KH = 8
NA_KW = 16
NA_QROWS = 4
NA_SLAB = 12
MLA_HEADS = 8
MLA_NOPE = 64
MLA_ROPE = 32
MLA_V = 64
MLA_KEY_PARTS = 4
MLA_Q_LORA = 256
MLA_KV_LORA = 128
SWA_HEADS = 8
SWA_KV_HEADS = 2
SWA_HEAD_DIM = 64
SWA_WINDOW = 128
N_BRANCH = 3
BRANCH_W = 512
N_EXPERTS = 32
TOP_K = 4
SWIGLU_ALPHA = 1.702
SWIGLU_LIMIT = 7.0

TM = 256
MOE_TM = 512
SC_CORES = 2
SC_SUBCORES = 16
SC_SCATTER_ROWS = 64
SC_GATHER_ROWS = 64
VMEM_LIMIT = 56 * 1024 * 1024

BF16 = jnp.bfloat16
F32 = jnp.float32


def _dot(a, b):
    return jnp.dot(a, b, preferred_element_type=F32)


def _dot_nt(a, b):
    return lax.dot_general(a, b, (((1,), (1,)), ((), ())), preferred_element_type=F32)


def _params(n_axes, vmem=VMEM_LIMIT):
    return pltpu.CompilerParams(dimension_semantics=("arbitrary",) * n_axes, vmem_limit_bytes=vmem)


def _layer_spec(a, li):
    return pl.BlockSpec((None,) + a.shape[1:], lambda *_: (li,) + (0,) * (a.ndim - 1))


def _mod_spec(li, off=0):
    return lambda d: pl.BlockSpec((None, 1, 1, 6, d), lambda i, j: (li, i, jnp.minimum(j + off, 1), 0, 0))


def _rms(x):
    return x * lax.rsqrt(jnp.mean(x * x, axis=-1, keepdims=True) + EPS)


def _sigmoid(x):
    return 1.0 / (1.0 + jnp.exp(-x))


def _pack_rows(xb):
    half = xb.shape[1] // 2
    lo = pltpu.bitcast(xb[:, :half].astype(F32), jnp.int32)
    hi = pltpu.bitcast(xb[:, half:].astype(F32), jnp.int32)
    return (hi & jnp.int32(-65536)) | lax.shift_right_logical(lo, 16)


def _unpack_rows(w):
    lo = pltpu.bitcast(lax.shift_left(w, 16), F32).astype(BF16)
    hi = pltpu.bitcast(w & jnp.int32(-65536), F32).astype(BF16)
    return jnp.concatenate([lo, hi], axis=1)


def _mod_kernel(c_ref, w_ref, b_ref, o_ref):
    c = c_ref[...]
    s = (c * _sigmoid(c)).astype(BF16)
    o_ref[0] = _dot(s, w_ref[0].astype(BF16)) + b_ref[0]


def _modulation(cvec, w_mod, b_mod):
    depth, d, n6 = w_mod.shape
    tn = n6 // 4
    return pl.pallas_call(
        _mod_kernel,
        out_shape=jax.ShapeDtypeStruct((depth, 8, n6), F32),
        grid=(depth, n6 // tn),
        in_specs=[pl.BlockSpec((8, d), lambda l, j: (0, 0)),
                  pl.BlockSpec((1, d, tn), lambda l, j: (l, 0, j)),
                  pl.BlockSpec((1, 1, tn), lambda l, j: (l, 0, j))],
        out_specs=pl.BlockSpec((1, 8, tn), lambda l, j: (l, 0, j)),
        compiler_params=_params(2),
        name="modulation",
    )(cvec, w_mod, b_mod.reshape(depth, 1, n6))


def _rope_groups(x, tab_ref, shift):
    cos, sdn, sup = tab_ref[0], tab_ref[1], tab_ref[2]
    outs = []
    for g in range(x.shape[1] // LANES):
        xg = x[:, g * LANES:(g + 1) * LANES]
        outs.append(xg * cos + pltpu.roll(xg, LANES - shift, 1) * sdn + pltpu.roll(xg, shift, 1) * sup)
    return outs[0] if len(outs) == 1 else jnp.concatenate(outs, axis=1)


def _inproj_kernel(*refs, with_moe):
    if with_moe:
        x_ref, y4_ref, route_ref, modp_ref = refs[:4]
        refs = refs[4:]
        (mod_ref, g1_ref, w_ref, wuq_ref, wukv_ref, gq_ref, gkv_ref, rb_ref, rc_ref, xo_ref,
         aq_ref, ak_ref, av_ref, bq_ref, bk_ref, bv_ref, cq_ref, ck_ref, cv_ref, gate_ref) = refs
        x = x_ref[0] + modp_ref[0, 0][5:6] * _combine(y4_ref, route_ref)
        xo_ref[0] = x
    else:
        (c_ref, x_ref, mod_ref, g1_ref, w_ref, wuq_ref, wukv_ref, gq_ref, gkv_ref, rb_ref, rc_ref, xo_ref,
         aq_ref, ak_ref, av_ref, bq_ref, bk_ref, bv_ref, cq_ref, ck_ref, cv_ref, gate_ref) = refs
        x = jnp.where(pl.program_id(1) == 0, c_ref[0], x_ref[0])
        xo_ref[0] = x
    mod = mod_ref[0, 0]
    h = (_rms(x) * g1_ref[...]) * (1.0 + mod[1:2]) + mod[0:1]
    hb = h.astype(BF16)
    acc_b = _dot(hb, w_ref[:, 1536:2048])
    acc_a = _dot(hb, w_ref[:, 0:1536])
    qn = (_rms(acc_b[:, 0:256]) * gq_ref[...]).astype(BF16)
    kvn = (_rms(acc_b[:, 256:384]) * gkv_ref[...]).astype(BF16)
    q = _dot(qn, wuq_ref[...])
    kv = _dot(kvn, wukv_ref[...])
    acc_c = _dot(hb, w_ref[:, 2048:3072])
    aq_ref[0] = (acc_a[:, 0:512] * (NA_HEAD_DIM ** -0.5 * LOG2E)).astype(BF16)
    ak_ref[0] = acc_a[:, 512:1024].astype(BF16)
    av_ref[0] = acc_a[:, 1024:1536].T.astype(BF16)
    acc_g = _dot(hb, w_ref[:, 3072:])
    kr = _rope_groups(acc_b[:, 384:512], rb_ref, MLA_ROPE // 2)
    q = _rope_groups(q, rb_ref, MLA_ROPE // 2)
    bq_ref[0] = (q * ((MLA_NOPE + MLA_ROPE) ** -0.5 * LOG2E)).astype(BF16)
    bk_ref[0] = (kv[:, 0:1024] + jnp.concatenate([kr] * MLA_HEADS, axis=1)).astype(BF16)
    bv_ref[0] = kv[:, 1024:1536].T.astype(BF16)
    cq_ref[0] = (_rope_groups(acc_c[:, 0:512], rc_ref, SWA_HEAD_DIM // 2) * (SWA_HEAD_DIM ** -0.5 * LOG2E)).astype(BF16)
    ck_ref[0] = _rope_groups(acc_c[:, 512:768], rc_ref, SWA_HEAD_DIM // 2).astype(BF16)
    cv = jnp.concatenate([acc_c[:, 768:832], acc_c[:, 896:960]], axis=1)
    cv_ref[0] = cv.T.astype(BF16)
    gate_ref[0] = _sigmoid(acc_g).astype(BF16)


def _inproj(li, x_in, moe, modsel, g1, w, wuq, wukv, gq, gkv, rope_b, rope_c):
    row = lambda width: pl.BlockSpec((1, TM, width), lambda i, j: (i, j, 0))
    full = lambda a: _layer_spec(a, li)
    if moe is None:
        ctx, x = x_in
        b, l, d = x.shape[0], ctx.shape[1] + x.shape[1], x.shape[2]
        in_specs = [pl.BlockSpec((1, TM, d), lambda i, j: (i, 0, 0)),
                    pl.BlockSpec((1, TM, d), lambda i, j: (i, jnp.maximum(j - 1, 0), 0))]
        args = [ctx, x]
    else:
        b, l, d = x_in.shape
        y4, route = moe
        in_specs = [row(d), pl.BlockSpec((TOP_K, 1, TM, y4.shape[3]), lambda i, j: (0, i, j, 0)), row(LANES),
                    _mod_spec(li - 1)(d)]
        args = [x_in, y4, route, modsel]
    nb = l // TM
    widths = (512, 512, 512, 1024, 1024, 512, 512, 256, SWA_KV_HEADS * SWA_HEAD_DIM, N_BRANCH * d)
    out_specs = [row(wd) for wd in widths]
    out_shape = [jax.ShapeDtypeStruct((b, l, wd), BF16) for wd in widths]
    for pos in (2, 5, 8):
        out_specs[pos] = pl.BlockSpec((1, widths[pos], TM), lambda i, j: (i, 0, j))
        out_shape[pos] = jax.ShapeDtypeStruct((b, widths[pos], l), BF16)
    out_specs = [row(d)] + out_specs
    out_shape = [jax.ShapeDtypeStruct((b, l, d), F32)] + out_shape
    in_specs += [_mod_spec(li)(d), full(g1), full(w), full(wuq), full(wukv), full(gq), full(gkv),
                 pl.BlockSpec((3, TM, LANES), lambda i, j: (0, j, 0)),
                 pl.BlockSpec((3, TM, LANES), lambda i, j: (0, j, 0))]
    args += [modsel, g1, w, wuq, wukv, gq, gkv, rope_b, rope_c]
    return pl.pallas_call(
        functools.partial(_inproj_kernel, with_moe=moe is not None),
        out_shape=out_shape,
        grid=(b, nb),
        in_specs=in_specs,
        out_specs=out_specs,
        compiler_params=_params(2),
        name="inproj",
    )(*args)


def _lane_lo():
    return lax.broadcasted_iota(jnp.int32, (1, LANES), 1) < (LANES // 2)


def _split_heads(qp, lo):
    zero = jnp.zeros_like(qp)
    return jnp.where(lo, qp, zero), jnp.where(lo, zero, qp)


def _softmax_pv_t(score_parts, vt_parts, extra_logit=None):
    m = score_parts[0].max(axis=0, keepdims=True)
    for s in score_parts[1:]:
        m = jnp.maximum(m, s.max(axis=0, keepdims=True))
    if extra_logit is not None:
        m = jnp.maximum(m, extra_logit)
    den = None
    acc = None
    for s, vt in zip(score_parts, vt_parts):
        e = jnp.exp2(s - m)
        d = e.sum(axis=0, keepdims=True)
        den = d if den is None else den + d
        o = _dot(vt, e.astype(BF16))
        acc = o if acc is None else acc + o
    if extra_logit is not None:
        den = den + jnp.exp2(extra_logit - m)
    return acc / den


def _heads_ahead(n_heads, ahead, scores, finish):
    pending = [scores(h) for h in range(ahead)]
    outs = []
    for h in range(n_heads):
        if h + ahead < n_heads:
            pending.append(scores(h + ahead))
        outs.append(finish(h, pending.pop(0)))
    return outs


def _na_kernel(q_ref, k_ref, vt_ref, bias_ref, o_ref, *, ctx, rows):
    j = pl.program_id(1)
    lo = _lane_lo()

    def head_q(h):
        sl = slice((h // 2) * LANES, (h // 2 + 1) * LANES)
        return _split_heads(q_ref[0, :, sl], lo)[h % 2], sl

    def vt(h, cols):
        return vt_ref[0, h * NA_HEAD_DIM:(h + 1) * NA_HEAD_DIM, cols]

    @pl.when(j == 0)
    def _():
        def scores(h):
            qm, sl = head_q(h)
            return [_dot_nt(k_ref[0, 0:ctx, sl], qm)]

        outs = _heads_ahead(NA_HEADS, 2, scores, lambda h, s: _softmax_pv_t(s, [vt(h, slice(0, ctx))]))
        o_ref[0] = jnp.concatenate(outs, axis=0).T.astype(BF16)

    @pl.when(j > 0)
    def _():
        r = (j - 1) * NA_QROWS
        s0 = jnp.clip(r - NA_KH // 2, 0, rows - NA_SLAB)
        slab = pl.ds(pl.multiple_of(ctx + s0 * GRID_W, NA_QROWS * GRID_W), NA_SLAB * GRID_W)

        def scores(h):
            qm, sl = head_q(h)
            return [_dot_nt(k_ref[0, 0:ctx, sl], qm), _dot_nt(k_ref[0, slab, sl], qm) + bias_ref[0, h]]

        outs = _heads_ahead(NA_HEADS, 2, scores,
                            lambda h, s: _softmax_pv_t(s, [vt(h, slice(0, ctx)), vt(h, slab)]))
        o_ref[0] = jnp.concatenate(outs, axis=0).T.astype(BF16)


def _na_attention(li, aq, ak, avt, bias, ctx):
    b, l, w = aq.shape
    assert ((l - ctx) // GRID_W) % NA_QROWS == 0 and ctx % (NA_QROWS * GRID_W) == 0
    nb = l // TM
    rows = (l - ctx) // GRID_W
    last = nb - 1

    def bias_map(i, j):
        return (li, jnp.where(j <= 1, 0, jnp.where(j == last, 2, 1)), 0, 0, 0)

    return pl.pallas_call(
        functools.partial(_na_kernel, ctx=ctx, rows=rows),
        out_shape=jax.ShapeDtypeStruct((b, l, w), BF16),
        grid=(b, nb),
        in_specs=[pl.BlockSpec((1, TM, w), lambda i, j: (i, j, 0)),
                  pl.BlockSpec((1, l, w), lambda i, j: (i, 0, 0)),
                  pl.BlockSpec((1, w, l), lambda i, j: (i, 0, 0)),
                  pl.BlockSpec((None, 1, NA_HEADS, NA_SLAB * GRID_W, TM), bias_map)],
        out_specs=pl.BlockSpec((1, TM, w), lambda i, j: (i, j, 0)),
        compiler_params=_params(2),
        name="na_attention",
    )(aq, ak, avt, bias)


def _na_bias_tables(na_rpb, rows):
    depth = na_rpb.shape[0]
    qc = np.arange(GRID_W)[:, None]
    kc = np.arange(GRID_W)[None, :]
    c0 = np.clip(qc - NA_KW // 2, 0, GRID_W - NA_KW)
    col_ok = (kc >= c0) & (kc < c0 + NA_KW)
    col_idx = np.clip(kc - qc + NA_KW - 1, 0, 2 * NA_KW - 2)
    n_off = 2 * NA_KH - 1
    bc = jnp.take(na_rpb, jnp.asarray(col_idx.reshape(-1)), axis=3).reshape(depth, NA_HEADS, n_off, GRID_W, GRID_W)
    bc = jnp.where(jnp.asarray(col_ok), bc * LOG2E, NEG_INF)
    bc = jnp.concatenate([bc, jnp.full((depth, NA_HEADS, 1, GRID_W, GRID_W), NEG_INF, F32)], axis=2)
    bc = bc.swapaxes(-1, -2)
    i_idx = np.zeros((3, NA_SLAB, NA_QROWS), np.int32)
    for case, (r, s) in enumerate(((0, 0), (NA_KH // 2, 0), (rows - NA_QROWS, rows - NA_SLAB))):
        for a in range(NA_QROWS):
            qr = r + a
            r0 = min(max(qr - NA_KH // 2, 0), rows - NA_KH)
            for c in range(NA_SLAB):
                kr = s + c
                i_idx[case, c, a] = kr - qr + NA_KH - 1 if r0 <= kr < r0 + NA_KH else n_off
    tab = jnp.concatenate([bc, bc], axis=-1)
    n_tab = n_off + 1

    def expand_kernel(idx_ref, tab_ref, o_ref):
        case = pl.program_id(1)
        lo = _lane_lo()
        for c in range(NA_SLAB):
            for ap in range(NA_QROWS // 2):
                at = (case * NA_SLAB + c) * NA_QROWS + 2 * ap
                o_ref[c * GRID_W:(c + 1) * GRID_W, ap * LANES:(ap + 1) * LANES] = jnp.where(
                    lo, tab_ref[idx_ref[at]], tab_ref[idx_ref[at + 1]])

    grid_spec = pltpu.PrefetchScalarGridSpec(
        num_scalar_prefetch=1,
        grid=(depth, 3, NA_HEADS),
        in_specs=[pl.BlockSpec((None, None, n_tab, GRID_W, LANES), lambda l, z, h, idx: (l, h, 0, 0, 0))],
        out_specs=pl.BlockSpec((None, None, None, NA_SLAB * GRID_W, NA_QROWS * GRID_W),
                               lambda l, z, h, idx: (l, z, h, 0, 0)),
    )
    return pl.pallas_call(
        expand_kernel,
        out_shape=jax.ShapeDtypeStruct((depth, 3, NA_HEADS, NA_SLAB * GRID_W, NA_QROWS * GRID_W), F32),
        grid_spec=grid_spec,
        compiler_params=_params(3),
        name="na_bias_expand",
    )(jnp.asarray(i_idx.reshape(-1)), tab)


def _mla_kernel(q_ref, k_ref, vt_ref, o_ref, *, ctx):
    j = pl.program_id(1)

    def run(nkeys):
        tiles = nkeys // 256
        nparts = min(MLA_KEY_PARTS, tiles)
        cuts = [(tiles * p // nparts) * 256 for p in range(nparts + 1)]
        parts = list(zip(cuts[:-1], cuts[1:]))

        def scores(h):
            hsl = slice(h * LANES, (h + 1) * LANES)
            return [_dot_nt(k_ref[0, a:b, hsl], q_ref[0, :, hsl]) for a, b in parts]

        def finish(h, s_parts):
            return _softmax_pv_t(s_parts, [vt_ref[0, h * MLA_V:(h + 1) * MLA_V, a:b] for a, b in parts])

        outs = _heads_ahead(MLA_HEADS, 3, scores, finish)
        o_ref[0] = jnp.concatenate(outs, axis=0).T.astype(BF16)

    @pl.when(j == 0)
    def _():
        run(ctx)

    @pl.when(j > 0)
    def _():
        run(k_ref.shape[1])


def _mla_attention(bq, bk, bvt, ctx):
    b, l, wq = bq.shape
    wv = bvt.shape[1]
    return pl.pallas_call(
        functools.partial(_mla_kernel, ctx=ctx),
        out_shape=jax.ShapeDtypeStruct((b, l, wv), BF16),
        grid=(b, l // TM),
        in_specs=[pl.BlockSpec((1, TM, wq), lambda i, j: (i, j, 0)),
                  pl.BlockSpec((1, l, wq), lambda i, j: (i, 0, 0)),
                  pl.BlockSpec((1, wv, l), lambda i, j: (i, 0, 0))],
        out_specs=pl.BlockSpec((1, TM, wv), lambda i, j: (i, j, 0)),
        compiler_params=_params(2),
        name="mla_attention",
    )(bq, bk, bvt)


def _swa_kernel(sink_ref, q_ref, k_ref, vt_ref, o_ref, *, ctx, n_lat):
    j = pl.program_id(1)
    lo = _lane_lo()
    group = SWA_HEADS // SWA_KV_HEADS
    band = TM + 2 * SWA_WINDOW

    def head_q(h):
        sl = slice((h // 2) * LANES, (h // 2 + 1) * LANES)
        return _split_heads(q_ref[0, :, sl], lo)[h % 2]

    def kv_lanes(h):
        return slice((h // group) * LANES, (h // group + 1) * LANES)

    def vt(h, cols):
        kv = h // group
        return vt_ref[0, kv * SWA_HEAD_DIM:(kv + 1) * SWA_HEAD_DIM, cols]

    def sink(h):
        return jnp.full((1, 1), sink_ref[h] * LOG2E, F32)

    @pl.when(j == 0)
    def _():
        outs = _heads_ahead(
            SWA_HEADS, 2, lambda h: [_dot_nt(k_ref[0, 0:ctx, kv_lanes(h)], head_q(h))],
            lambda h, s: _softmax_pv_t(s, [vt(h, slice(0, ctx))], extra_logit=sink(h)))
        o_ref[0] = jnp.concatenate(outs, axis=0).T.astype(BF16)

    @pl.when(j > 0)
    def _():
        q0 = (j - 1) * TM
        s0 = jnp.clip(q0 - SWA_WINDOW, 0, n_lat - band)
        rows = pl.ds(pl.multiple_of(ctx + s0, SWA_WINDOW), band)
        kpos = s0 + lax.broadcasted_iota(jnp.int32, (band, TM), 0)
        qpos = q0 + lax.broadcasted_iota(jnp.int32, (band, TM), 1)
        keep = jnp.abs(qpos - kpos) <= SWA_WINDOW

        def scores(h):
            qm = head_q(h)
            return [_dot_nt(k_ref[0, 0:ctx, kv_lanes(h)], qm),
                    jnp.where(keep, _dot_nt(k_ref[0, rows, kv_lanes(h)], qm), NEG_INF)]

        outs = _heads_ahead(
            SWA_HEADS, 2, scores,
            lambda h, s: _softmax_pv_t(s, [vt(h, slice(0, ctx)), vt(h, rows)], extra_logit=sink(h)))
        o_ref[0] = jnp.concatenate(outs, axis=0).T.astype(BF16)


def _swa_attention(sink, cq, ck, cvt, ctx):
    b, l, w = cq.shape
    wk = ck.shape[2]
    wv = cvt.shape[1]
    grid_spec = pltpu.PrefetchScalarGridSpec(
        num_scalar_prefetch=1,
        grid=(b, l // TM),
        in_specs=[pl.BlockSpec((1, TM, w), lambda i, j, s: (i, j, 0)),
                  pl.BlockSpec((1, l, wk), lambda i, j, s: (i, 0, 0)),
                  pl.BlockSpec((1, wv, l), lambda i, j, s: (i, 0, 0))],
        out_specs=pl.BlockSpec((1, TM, w), lambda i, j, s: (i, j, 0)),
    )
    return pl.pallas_call(
        functools.partial(_swa_kernel, ctx=ctx, n_lat=l - ctx),
        out_shape=jax.ShapeDtypeStruct((b, l, w), BF16),
        grid_spec=grid_spec,
        compiler_params=_params(2),
        name="swa_attention",
    )(sink, cq, ck, cvt)


def _merge_kernel(oa_ref, ob_ref, oc_ref, gate_ref, x_ref, mod_ref, wbr_ref, wout_ref, g2_ref,
                  rw_ref, rb_ref, xo_ref, h2_ref, route_ref, cnt_ref, run_ref, *, n_exp):
    d = x_ref.shape[2]

    @pl.when(jnp.logical_and(pl.program_id(0) == 0, pl.program_id(1) == 0))
    def _():
        run_ref[...] = jnp.zeros_like(run_ref)

    mod = mod_ref[0, 0]
    mix = None
    for i, o_ref in enumerate((oa_ref, ob_ref, oc_ref)):
        t = gate_ref[0, :, i * d:(i + 1) * d].astype(F32) * _dot(o_ref[0], wbr_ref[i])
        mix = t if mix is None else mix + t
    y = _dot(mix.astype(BF16), wout_ref[...])
    x = x_ref[0] + mod[2:3] * y
    xo_ref[0] = x
    h2 = (_rms(x) * g2_ref[...]) * (1.0 + mod[4:5]) + mod[3:4]
    h2b = h2.astype(BF16)
    h2_ref[0] = _pack_rows(h2b)
    logits = _dot(h2b, rw_ref[...]) + rb_ref[...]
    lane = lax.broadcasted_iota(jnp.int32, logits.shape, 1).astype(F32)
    work = jnp.where(lane < n_exp, logits, -jnp.inf)
    ids, vals = [], []
    for _ in range(TOP_K):
        m = work.max(axis=-1, keepdims=True)
        idx = jnp.where(work == m, lane, float(LANES)).min(axis=-1, keepdims=True)
        ids.append(idx)
        vals.append(m)
        work = jnp.where(lane == idx, -jnp.inf, work)
    ex = [jnp.exp(v - vals[0]) for v in vals]
    den = ex[0] + ex[1] + ex[2] + ex[3]
    hits = jnp.zeros(logits.shape, F32)
    for idx in ids:
        hits = hits + jnp.where(lane == idx, 1.0, 0.0)
    r = lax.broadcasted_iota(jnp.int32, (TM, TM), 0)
    c = lax.broadcasted_iota(jnp.int32, (TM, TM), 1)
    tri = jnp.where(c < r, 1.0, 0.0).astype(BF16)
    before = _dot(tri, hits.astype(BF16)) + run_ref[0:1]
    route = jnp.zeros(logits.shape, F32)
    for k in range(TOP_K):
        rank = jnp.where(lane == ids[k], before, 0.0).sum(axis=-1, keepdims=True)
        route = jnp.where(lane == k, ids[k], route)
        route = jnp.where(lane == TOP_K + k, ex[k] / den, route)
        route = jnp.where(lane == 2 * TOP_K + k, rank, route)
    route_ref[0] = route
    run_ref[...] = run_ref[...] + hits.sum(axis=0, keepdims=True)
    cnt_ref[...] = run_ref[...]


def _merge(li, oa, ob, oc, gate, x_all, modsel, wbr, wout, g2, rw, rb, n_exp):
    b, l, d = x_all.shape
    row = lambda width: pl.BlockSpec((1, TM, width), lambda i, j: (i, j, 0))
    full = lambda a: _layer_spec(a, li)
    return pl.pallas_call(
        functools.partial(_merge_kernel, n_exp=n_exp),
        out_shape=[jax.ShapeDtypeStruct((b, l, d), F32),
                   jax.ShapeDtypeStruct((b, l, d // 2), jnp.int32),
                   jax.ShapeDtypeStruct((b, l, LANES), F32),
                   jax.ShapeDtypeStruct((8, LANES), F32)],
        grid=(b, l // TM),
        in_specs=[row(BRANCH_W), row(BRANCH_W), row(BRANCH_W), row(N_BRANCH * d), row(d), _mod_spec(li)(d),
                  full(wbr), full(wout), full(g2), full(rw), full(rb)],
        out_specs=[row(d), row(d // 2), row(LANES), pl.BlockSpec((8, LANES), lambda i, j: (0, 0))],
        scratch_shapes=[pltpu.VMEM((8, LANES), F32)],
        compiler_params=_params(2),
        name="merge",
    )(oa, ob, oc, gate, x_all, modsel, wbr, wout, g2, rw, rb)


def _moe_kernel(be_ref, nx_ref, nu_ref, x_ref, wgu_hbm, bgu_ref, wdn_hbm, bdn_ref, y_ref,
                gu_stage, dn_stage, wgu_s, wdn_s, sem, *, li):
    i = pl.program_id(0)
    ff = wdn_s.shape[0]

    def fetch(e):
        return (pltpu.make_async_copy(wgu_hbm.at[li, e], gu_stage, sem.at[0]),
                pltpu.make_async_copy(wdn_hbm.at[li, e], dn_stage, sem.at[1]))

    @pl.when(i < nu_ref[0])
    def _():
        e = be_ref[i]

        @pl.when(i == 0)
        def _():
            for copy in fetch(e):
                copy.start()

        @pl.when(jnp.logical_or(i == 0, e != be_ref[jnp.maximum(i - 1, 0)]))
        def _():
            for copy in fetch(e):
                copy.wait()
            wgu_s[...] = gu_stage[...].astype(BF16)
            wdn_s[...] = dn_stage[...].astype(BF16)
            nxt = nx_ref[i]

            @pl.when(nxt >= 0)
            def _():
                for copy in fetch(nxt):
                    copy.start()

        gu = _dot(_unpack_rows(x_ref[...]), wgu_s[...]) + bgu_ref[0]
        glu = jnp.minimum(gu[:, :ff], SWIGLU_LIMIT)
        lin = jnp.clip(gu[:, ff:], -SWIGLU_LIMIT, SWIGLU_LIMIT)
        act = glu * _sigmoid(SWIGLU_ALPHA * glu) * (lin + 1.0)
        y_ref[...] = _pack_rows((_dot(act.astype(BF16), wdn_s[...]) + bdn_ref[0]).astype(BF16))


def _moe_experts(li, blk_e, nxt_e, n_used, xs, w_gu, b_gu, w_dn, b_dn):
    n_rows, packed_w = xs.shape
    depth, n_exp, d, ff2 = w_gu.shape
    ff = ff2 // 2
    n_blocks = n_rows // MOE_TM

    def row_map(i, be, nx, nu):
        return (jnp.minimum(i, nu[0] - 1), 0)

    def b_map(i, be, nx, nu):
        return (li, be[jnp.minimum(i, nu[0] - 1)], 0, 0)

    grid_spec = pltpu.PrefetchScalarGridSpec(
        num_scalar_prefetch=3,
        grid=(n_blocks,),
        in_specs=[pl.BlockSpec((MOE_TM, packed_w), row_map),
                  pl.BlockSpec(memory_space=pl.ANY),
                  pl.BlockSpec((None, 1, 1, ff2), b_map),
                  pl.BlockSpec(memory_space=pl.ANY),
                  pl.BlockSpec((None, 1, 1, d), b_map)],
        out_specs=pl.BlockSpec((MOE_TM, packed_w), row_map),
        scratch_shapes=[pltpu.VMEM((d, ff2), F32), pltpu.VMEM((ff, d), F32),
                        pltpu.VMEM((d, ff2), BF16), pltpu.VMEM((ff, d), BF16),
                        pltpu.SemaphoreType.DMA((2,))],
    )
    return pl.pallas_call(
        functools.partial(_moe_kernel, li=li),
        out_shape=jax.ShapeDtypeStruct((n_rows, packed_w), jnp.int32),
        grid_spec=grid_spec,
        compiler_params=_params(1),
        name="moe_experts",
    )(blk_e, nxt_e, n_used, xs, w_gu, b_gu.reshape(depth, n_exp, 1, ff2), w_dn, b_dn.reshape(depth, n_exp, 1, d))


def _layout(route, cnt, n_exp):
    b, l, _ = route.shape
    t = b * l
    ids = route[..., 0:TOP_K].astype(jnp.int32)
    rank = route[..., 2 * TOP_K:3 * TOP_K].astype(jnp.int32)
    counts = cnt[0, :n_exp].astype(jnp.int32)
    padded = (counts + MOE_TM - 1) // MOE_TM * MOE_TM
    pad_end = jnp.cumsum(padded)
    pad_start = pad_end - padded
    onehot = ids[..., None] == jnp.arange(n_exp, dtype=jnp.int32)
    dest = jnp.sum(jnp.where(onehot, pad_start, 0), axis=-1) + rank
    dest = dest.reshape(t, TOP_K).T
    n_blocks = -(-t * TOP_K // MOE_TM) + n_exp
    blk_start = jnp.arange(n_blocks, dtype=jnp.int32) * MOE_TM
    blk_e = jnp.minimum(jnp.sum(blk_start[:, None] >= pad_end[None, :], axis=1), n_exp - 1).astype(jnp.int32)
    n_used = (pad_end[-1] // MOE_TM).astype(jnp.int32).reshape(1)
    ar = jnp.arange(n_exp, dtype=jnp.int32)
    later = jnp.logical_and(ar[None, :] > ar[:, None], (counts > 0)[None, :])
    nxt_of = jnp.min(jnp.where(later, ar[None, :], n_exp), axis=1)
    nxt_of = jnp.where(nxt_of == n_exp, -1, nxt_of)
    nxt_e = jnp.sum(jnp.where(blk_e[:, None] == ar[None, :], nxt_of[None, :], 0), axis=1).astype(jnp.int32)
    return dest, blk_e, nxt_e, n_used, n_blocks * MOE_TM


def _sc_mesh():
    return plsc.VectorSubcoreMesh(core_axis_name="c", subcore_axis_name="s",
                                  num_cores=SC_CORES, num_subcores=SC_SUBCORES)


def _sc_worker():
    return lax.axis_index("s") * SC_CORES + lax.axis_index("c")


def _sc_scatter_rows(x, idx, n_rows):
    t, d = x.shape
    n_idx = idx.shape[0]
    workers = SC_CORES * SC_SUBCORES
    n_chunks = n_idx // (workers * SC_SCATTER_ROWS)
    assert n_chunks * workers * SC_SCATTER_ROWS == n_idx and t % SC_SCATTER_ROWS == 0 and n_chunks % 2 == 0
    idx3 = idx.reshape(workers, n_chunks, SC_SCATTER_ROWS)

    @pl.kernel(out_type=jax.ShapeDtypeStruct((n_rows, d), x.dtype), mesh=_sc_mesh(),
               scratch_types=[pltpu.VMEM((n_chunks, SC_SCATTER_ROWS), jnp.int32),
                              pltpu.VMEM((SC_SCATTER_ROWS, d), x.dtype),
                              pltpu.VMEM((SC_SCATTER_ROWS, d), x.dtype),
                              pltpu.SemaphoreType.DMA, pltpu.SemaphoreType.DMA, pltpu.SemaphoreType.DMA])
    def scatter_kernel(x_hbm, i_hbm, o_hbm, idx_v, buf0, buf1, sem0, sem1, out_sem):
        wid = _sc_worker()
        pltpu.sync_copy(i_hbm.at[wid], idx_v)
        bufs, sems = (buf0, buf1), (sem0, sem1)

        def load(chunk, b):
            src = pl.multiple_of(((wid * n_chunks + chunk) * SC_SCATTER_ROWS) % t, SC_SCATTER_ROWS)
            return pltpu.make_async_copy(x_hbm.at[pl.ds(src, SC_SCATTER_ROWS)], bufs[b], sems[b])

        load(0, 0).start()

        @pl.loop(0, n_chunks, step=2)
        def _(j):
            for b in range(2):
                chunk = j + b
                load(chunk, b).wait()
                if b == 0:
                    load(chunk + 1, 1).start()
                else:
                    @pl.when(chunk + 1 < n_chunks)
                    def _():
                        load(chunk + 1, 0).start()
                pltpu.async_copy(bufs[b], o_hbm.at[idx_v.at[chunk]], out_sem).wait()

    return scatter_kernel(x, idx3)


def _sc_gather_rows(x, idx):
    d = x.shape[1]
    n_idx = idx.shape[0]
    workers = SC_CORES * SC_SUBCORES
    per_worker = n_idx // workers
    n_chunks = per_worker // SC_GATHER_ROWS
    assert n_chunks * workers * SC_GATHER_ROWS == n_idx

    assert n_chunks % 2 == 0

    @pl.kernel(out_type=jax.ShapeDtypeStruct((n_idx, d), x.dtype), mesh=_sc_mesh(),
               scratch_types=[pltpu.VMEM((per_worker,), jnp.int32),
                              pltpu.VMEM((SC_GATHER_ROWS, d), x.dtype),
                              pltpu.VMEM((SC_GATHER_ROWS, d), x.dtype),
                              pltpu.SemaphoreType.DMA, pltpu.SemaphoreType.DMA])
    def gather_kernel(x_hbm, i_hbm, o_hbm, idx_v, buf0, buf1, sem0, sem1):
        base = _sc_worker() * per_worker
        pltpu.sync_copy(i_hbm.at[pl.ds(base, per_worker)], idx_v)
        bufs, sems = (buf0, buf1), (sem0, sem1)

        def gather(chunk, b):
            off = pl.multiple_of(chunk * SC_GATHER_ROWS, SC_GATHER_ROWS)
            return pltpu.make_async_copy(x_hbm.at[idx_v.at[pl.ds(off, SC_GATHER_ROWS)]], bufs[b], sems[b])

        gather(0, 0).start()

        @pl.loop(0, n_chunks, step=2)
        def _(j):
            for b in range(2):
                chunk = j + b
                gather(chunk, b).wait()
                if b == 0:
                    gather(chunk + 1, 1).start()
                else:
                    @pl.when(chunk + 1 < n_chunks)
                    def _():
                        gather(chunk + 1, 0).start()
                off = pl.multiple_of(chunk * SC_GATHER_ROWS, SC_GATHER_ROWS)
                pltpu.sync_copy(bufs[b], o_hbm.at[pl.ds(base + off, SC_GATHER_ROWS)])

    return gather_kernel(x, idx)


def _combine(y4_ref, route_ref):
    route = route_ref[0]
    y = None
    for k in range(TOP_K):
        t = route[:, TOP_K + k:TOP_K + k + 1] * _unpack_rows(y4_ref[k, 0]).astype(F32)
        y = t if y is None else y + t
    return y


def _final_kernel(x_ref, y4_ref, route_ref, mod_ref, g_ref, o_ref):
    x = x_ref[0] + mod_ref[0, 0][5:6] * _combine(y4_ref, route_ref)
    o_ref[0] = _rms(x) * g_ref[...]


def _final(li, x_all, y4, route, modsel, skip, final_g):
    b, l, d = x_all.shape
    off = skip // TM
    lo = l - skip
    return pl.pallas_call(
        _final_kernel, out_shape=jax.ShapeDtypeStruct((b, lo, d), F32), grid=(b, lo // TM),
        in_specs=[pl.BlockSpec((1, TM, d), lambda i, j: (i, j + off, 0)),
                  pl.BlockSpec((TOP_K, 1, TM, y4.shape[3]), lambda i, j: (0, i, j + off, 0)),
                  pl.BlockSpec((1, TM, LANES), lambda i, j: (i, j + off, 0)),
                  _mod_spec(li, off)(d),
                  pl.BlockSpec((1, d), lambda i, j: (0, 0))],
        out_specs=pl.BlockSpec((1, TM, d), lambda i, j: (i, j, 0)),
        compiler_params=_params(2), name="final_norm",
    )(x_all, y4, route, modsel, final_g)


def _rope_tables(n, ctx, rot_dim, group_pattern):
    t = np.arange(n, dtype=np.int32)
    row = (t // GRID_W).astype(np.float32)
    col = (t % GRID_W).astype(np.float32)
    per_axis = rot_dim // 2
    inv = (np.float32(ROPE_BASE) ** (-np.arange(0, per_axis, 2, dtype=np.float32) / np.float32(per_axis))).astype(np.float32)
    ang = np.concatenate([row[:, None] * inv[None], col[:, None] * inv[None]], axis=-1).astype(np.float32)
    cos, sin = np.cos(ang).astype(np.float32), np.sin(ang).astype(np.float32)
    half = rot_dim // 2
    tab = np.zeros((3, ctx + n, LANES), np.float32)
    tab[0] = 1.0
    for off in group_pattern:
        tab[0, ctx:, off:off + half] = cos
        tab[0, ctx:, off + half:off + rot_dim] = cos
        tab[1, ctx:, off:off + half] = -sin
        tab[2, ctx:, off + half:off + rot_dim] = sin
    return jnp.asarray(tab)


def _wprep_kernel(w_ref, o_ref):
    x = w_ref[0]
    rows = x.shape[0]
    o = 3 * 512 + MLA_Q_LORA + MLA_KV_LORA
    zeros = lambda n: jnp.zeros((rows, n), F32)
    kr = jnp.concatenate([zeros(MLA_NOPE), x[:, o:o + MLA_ROPE], zeros(LANES - MLA_NOPE - MLA_ROPE)], axis=1)
    o += MLA_ROPE
    cq = x[:, o:o + 512]
    k0, k1 = x[:, o + 512:o + 576], x[:, o + 576:o + 640]
    v0, v1 = x[:, o + 640:o + 704], x[:, o + 704:o + 768]
    g = x[:, o + 768:]
    o_ref[0] = jnp.concatenate([x[:, :3 * 512 + MLA_Q_LORA + MLA_KV_LORA], kr, cq, k0, k0, k1, k1, v0, v0, v1, v1, g],
                               axis=1).astype(BF16)


def _wprep(w_in):
    depth, d, cols = w_in.shape
    out_cols = cols - MLA_ROPE + LANES + 2 * 128
    tr = 128
    return pl.pallas_call(
        _wprep_kernel,
        out_shape=jax.ShapeDtypeStruct((depth, d, out_cols), BF16),
        grid=(depth, d // tr),
        in_specs=[pl.BlockSpec((1, tr, cols), lambda l, i: (l, i, 0))],
        out_specs=pl.BlockSpec((1, tr, out_cols), lambda l, i: (l, i, 0)),
        compiler_params=_params(2),
        name="wprep",
    )(w_in)


def _prep_weights(w_in, mla_w_uq, mla_w_ukv):
    depth = w_in.shape[0]
    w = _wprep(w_in)
    uq = mla_w_uq.reshape(depth, MLA_Q_LORA, MLA_HEADS, MLA_NOPE + MLA_ROPE)
    uq = jnp.pad(uq, ((0, 0), (0, 0), (0, 0), (0, LANES - MLA_NOPE - MLA_ROPE)))
    uq = uq.reshape(depth, MLA_Q_LORA, MLA_HEADS * LANES).astype(BF16)
    ukv = mla_w_ukv.reshape(depth, MLA_KV_LORA, MLA_HEADS, MLA_NOPE + MLA_V)
    uk = jnp.pad(ukv[..., :MLA_NOPE], ((0, 0), (0, 0), (0, 0), (0, LANES - MLA_NOPE)))
    uk = uk.reshape(depth, MLA_KV_LORA, MLA_HEADS * LANES)
    uv = ukv[..., MLA_NOPE:].reshape(depth, MLA_KV_LORA, MLA_HEADS * MLA_V)
    ukv = jnp.concatenate([uk, uv], axis=-1).astype(BF16)
    return w, uq, ukv


def kernel(x, c, ctx, c_ctx, norm1_g, norm2_g, w_mod, b_mod, w_in, na_rpb, mla_q_norm_g, mla_kv_norm_g, mla_w_uq, mla_w_ukv, swa_sink, w_branch, w_out, router_w, router_b, expert_w_gate_up, expert_b_gate_up, expert_w_down, expert_b_down, final_norm_g):
    b, n, d = x.shape
    lc = ctx.shape[1]
    l = lc + n
    depth = w_in.shape[0]
    n_exp = router_w.shape[2]
    assert lc == TM and n % TM == 0 and TM == NA_QROWS * GRID_W

    cvec = jnp.concatenate([c, c_ctx[None], jnp.zeros((8 - b - 1, d), F32)], axis=0)
    mod = _modulation(cvec, w_mod, b_mod)
    mod_lat = mod[:, :b].reshape(depth, b, 1, 6, d)
    mod_ctx = jnp.broadcast_to(mod[:, b].reshape(depth, 1, 1, 6, d), (depth, b, 1, 6, d))
    modsel = jnp.concatenate([mod_ctx, mod_lat], axis=2)

    w_all, uq_all, ukv_all = _prep_weights(w_in, mla_w_uq, mla_w_ukv)
    wbr_all = w_branch.astype(BF16)
    wout_all = w_out.astype(BF16)
    rw_all = jnp.pad(router_w, ((0, 0), (0, 0), (0, LANES - n_exp))).astype(BF16)
    rb_all = jnp.pad(router_b, ((0, 0), (0, LANES - n_exp))).reshape(depth, 1, LANES)
    rope_b = _rope_tables(n, lc, MLA_ROPE, (MLA_NOPE,))
    rope_c = _rope_tables(n, lc, SWA_HEAD_DIM, (0, SWA_HEAD_DIM))
    bias_all = _na_bias_tables(na_rpb, n // GRID_W)

    g1_all = norm1_g.reshape(depth, 1, d)
    g2_all = norm2_g.reshape(depth, 1, d)
    gq_all = mla_q_norm_g.reshape(depth, 1, -1)
    gkv_all = mla_kv_norm_g.reshape(depth, 1, -1)

    x_all = (ctx, x)
    moe = None
    for li in range(depth):
        ms = modsel
        x_all, aq, ak, av, bq, bk, bv, cq, ck, cv, gate = _inproj(
            li, x_all, moe, ms, g1_all, w_all, uq_all, ukv_all, gq_all, gkv_all, rope_b, rope_c)
        oa = _na_attention(li, aq, ak, av, bias_all, lc)
        ob = _mla_attention(bq, bk, bv, lc)
        oc = _swa_attention(swa_sink[li], cq, ck, cv, lc)
        x_all, h2, route, cnt = _merge(li, oa, ob, oc, gate, x_all, ms, wbr_all, wout_all,
                                       g2_all, rw_all, rb_all, n_exp)
        dest, blk_e, nxt_e, n_used, n_rows = _layout(route, cnt, n_exp)
        dest = dest.reshape(TOP_K * b * l)
        xs = _sc_scatter_rows(h2.reshape(b * l, d // 2), dest, n_rows)
        ys = _moe_experts(li, blk_e, nxt_e, n_used, xs, expert_w_gate_up, expert_b_gate_up,
                          expert_w_down, expert_b_down)
        moe = (_sc_gather_rows(ys, dest).reshape(TOP_K, b, l, d // 2), route)
    return _final(depth - 1, x_all, moe[0], moe[1], modsel, lc, final_norm_g.reshape(1, d))
```

```python
import functools

import numpy as np
import jax
import jax.numpy as jnp
from jax import lax
from jax.experimental import pallas as pl
from jax.experimental.pallas import tpu as pltpu
from jax.experimental.pallas import tpu_sc as plsc

GRID_W = 64
EPS = 1e-6
ROPE_BASE = 10000.0
NEG_INF = -1e30
LOG2E = 1.4426950408889634
LANES = 128

NA_HEADS = 8
NA_HEAD_DIM = 64
NA_KH = 8
NA_KW = 16
NA_QROWS = 4
NA_SLAB = 12
MLA_HEADS = 8
MLA_NOPE = 64
MLA_ROPE = 32
MLA_V = 64
MLA_KEY_PARTS = 4
MLA_Q_LORA = 256
MLA_KV_LORA = 128
SWA_HEADS = 8
SWA_KV_HEADS = 2
SWA_HEAD_DIM = 64
SWA_WINDOW = 128
N_BRANCH = 3
BRANCH_W = 512
N_EXPERTS = 32
TOP_K = 4
SWIGLU_ALPHA = 1.702
SWIGLU_LIMIT = 7.0

TM = 256
MOE_TM = 512
SC_CORES = 2
SC_SUBCORES = 16
SC_SCATTER_ROWS = 64
SC_GATHER_ROWS = 64
VMEM_LIMIT = 56 * 1024 * 1024

BF16 = jnp.bfloat16
F32 = jnp.float32


def _dot(a, b):
    return jnp.dot(a, b, preferred_element_type=F32)


def _dot_nt(a, b):
    return lax.dot_general(a, b, (((1,), (1,)), ((), ())), preferred_element_type=F32)


def _params(n_axes, vmem=VMEM_LIMIT):
    return pltpu.CompilerParams(dimension_semantics=("arbitrary",) * n_axes, vmem_limit_bytes=vmem)


def _layer_spec(a, li):
    return pl.BlockSpec((None,) + a.shape[1:], lambda *_: (li,) + (0,) * (a.ndim - 1))


def _mod_spec(li, off=0):
    return lambda d: pl.BlockSpec((None, 1, 1, 6, d), lambda i, j: (li, i, jnp.minimum(j + off, 1), 0, 0))


def _rms(x):
    return x * lax.rsqrt(jnp.mean(x * x, axis=-1, keepdims=True) + EPS)


def _sigmoid(x):
    return 1.0 / (1.0 + jnp.exp(-x))


def _pack_rows(xb):
    half = xb.shape[1] // 2
    lo = pltpu.bitcast(xb[:, :half].astype(F32), jnp.int32)
    hi = pltpu.bitcast(xb[:, half:].astype(F32), jnp.int32)
    return (hi & jnp.int32(-65536)) | lax.shift_right_logical(lo, 16)


def _unpack_rows(w):
    lo = pltpu.bitcast(lax.shift_left(w, 16), F32).astype(BF16)
    hi = pltpu.bitcast(w & jnp.int32(-65536), F32).astype(BF16)
    return jnp.concatenate([lo, hi], axis=1)


def _mod_kernel(c_ref, w_ref, b_ref, o_ref):
    c = c_ref[...]
    s = (c * _sigmoid(c)).astype(BF16)
    o_ref[0] = _dot(s, w_ref[0].astype(BF16)) + b_ref[0]


def _modulation(cvec, w_mod, b_mod):
    depth, d, n6 = w_mod.shape
    tn = n6 // 4
    return pl.pallas_call(
        _mod_kernel,
        out_shape=jax.ShapeDtypeStruct((depth, 8, n6), F32),
        grid=(depth, n6 // tn),
        in_specs=[pl.BlockSpec((8, d), lambda l, j: (0, 0)),
                  pl.BlockSpec((1, d, tn), lambda l, j: (l, 0, j)),
                  pl.BlockSpec((1, 1, tn), lambda l, j: (l, 0, j))],
        out_specs=pl.BlockSpec((1, 8, tn), lambda l, j: (l, 0, j)),
        compiler_params=_params(2),
        name="modulation",
    )(cvec, w_mod, b_mod.reshape(depth, 1, n6))


def _rope_groups(x, tab_ref, shift):
    cos, sdn, sup = tab_ref[0], tab_ref[1], tab_ref[2]
    outs = []
    for g in range(x.shape[1] // LANES):
        xg = x[:, g * LANES:(g + 1) * LANES]
        outs.append(xg * cos + pltpu.roll(xg, LANES - shift, 1) * sdn + pltpu.roll(xg, shift, 1) * sup)
    return outs[0] if len(outs) == 1 else jnp.concatenate(outs, axis=1)


def _inproj_kernel(*refs, with_moe):
    if with_moe:
        x_ref, y4_ref, route_ref, modp_ref = refs[:4]
        refs = refs[4:]
        (mod_ref, g1_ref, w_ref, wuq_ref, wukv_ref, gq_ref, gkv_ref, rb_ref, rc_ref, xo_ref,
         aq_ref, ak_ref, av_ref, bq_ref, bk_ref, bv_ref, cq_ref, ck_ref, cv_ref, gate_ref) = refs
        x = x_ref[0] + modp_ref[0, 0][5:6] * _combine(y4_ref, route_ref)
        xo_ref[0] = x
    else:
        (c_ref, x_ref, mod_ref, g1_ref, w_ref, wuq_ref, wukv_ref, gq_ref, gkv_ref, rb_ref, rc_ref, xo_ref,
         aq_ref, ak_ref, av_ref, bq_ref, bk_ref, bv_ref, cq_ref, ck_ref, cv_ref, gate_ref) = refs
        x = jnp.where(pl.program_id(1) == 0, c_ref[0], x_ref[0])
        xo_ref[0] = x
    mod = mod_ref[0, 0]
    h = (_rms(x) * g1_ref[...]) * (1.0 + mod[1:2]) + mod[0:1]
    hb = h.astype(BF16)
    acc_b = _dot(hb, w_ref[:, 1536:2048])
    acc_a = _dot(hb, w_ref[:, 0:1536])
    qn = (_rms(acc_b[:, 0:256]) * gq_ref[...]).astype(BF16)
    kvn = (_rms(acc_b[:, 256:384]) * gkv_ref[...]).astype(BF16)
    q = _dot(qn, wuq_ref[...])
    kv = _dot(kvn, wukv_ref[...])
    acc_c = _dot(hb, w_ref[:, 2048:3072])
    aq_ref[0] = (acc_a[:, 0:512] * (NA_HEAD_DIM ** -0.5 * LOG2E)).astype(BF16)
    ak_ref[0] = acc_a[:, 512:1024].astype(BF16)
    av_ref[0] = acc_a[:, 1024:1536].T.astype(BF16)
    acc_g = _dot(hb, w_ref[:, 3072:])
    kr = _rope_groups(acc_b[:, 384:512], rb_ref, MLA_ROPE // 2)
    q = _rope_groups(q, rb_ref, MLA_ROPE // 2)
    bq_ref[0] = (q * ((MLA_NOPE + MLA_ROPE) ** -0.5 * LOG2E)).astype(BF16)
    bk_ref[0] = (kv[:, 0:1024] + jnp.concatenate([kr] * MLA_HEADS, axis=1)).astype(BF16)
    bv_ref[0] = kv[:, 1024:1536].T.astype(BF16)
    cq_ref[0] = (_rope_groups(acc_c[:, 0:512], rc_ref, SWA_HEAD_DIM // 2) * (SWA_HEAD_DIM ** -0.5 * LOG2E)).astype(BF16)
    ck_ref[0] = _rope_groups(acc_c[:, 512:768], rc_ref, SWA_HEAD_DIM // 2).astype(BF16)
    cv = jnp.concatenate([acc_c[:, 768:832], acc_c[:, 896:960]], axis=1)
    cv_ref[0] = cv.T.astype(BF16)
    gate_ref[0] = _sigmoid(acc_g).astype(BF16)


def _inproj(li, x_in, moe, modsel, g1, w, wuq, wukv, gq, gkv, rope_b, rope_c):
    row = lambda width: pl.BlockSpec((1, TM, width), lambda i, j: (i, j, 0))
    full = lambda a: _layer_spec(a, li)
    if moe is None:
        ctx, x = x_in
        b, l, d = x.shape[0], ctx.shape[1] + x.shape[1], x.shape[2]
        in_specs = [pl.BlockSpec((1, TM, d), lambda i, j: (i, 0, 0)),
                    pl.BlockSpec((1, TM, d), lambda i, j: (i, jnp.maximum(j - 1, 0), 0))]
        args = [ctx, x]
    else:
        b, l, d = x_in.shape
        y4, route = moe
        in_specs = [row(d), pl.BlockSpec((TOP_K, 1, TM, y4.shape[3]), lambda i, j: (0, i, j, 0)), row(LANES),
                    _mod_spec(li - 1)(d)]
        args = [x_in, y4, route, modsel]
    nb = l // TM
    widths = (512, 512, 512, 1024, 1024, 512, 512, 256, SWA_KV_HEADS * SWA_HEAD_DIM, N_BRANCH * d)
    out_specs = [row(wd) for wd in widths]
    out_shape = [jax.ShapeDtypeStruct((b, l, wd), BF16) for wd in widths]
    for pos in (2, 5, 8):
        out_specs[pos] = pl.BlockSpec((1, widths[pos], TM), lambda i, j: (i, 0, j))
        out_shape[pos] = jax.ShapeDtypeStruct((b, widths[pos], l), BF16)
    out_specs = [row(d)] + out_specs
    out_shape = [jax.ShapeDtypeStruct((b, l, d), F32)] + out_shape
    in_specs += [_mod_spec(li)(d), full(g1), full(w), full(wuq), full(wukv), full(gq), full(gkv),
                 pl.BlockSpec((3, TM, LANES), lambda i, j: (0, j, 0)),
                 pl.BlockSpec((3, TM, LANES), lambda i, j: (0, j, 0))]
    args += [modsel, g1, w, wuq, wukv, gq, gkv, rope_b, rope_c]
    return pl.pallas_call(
        functools.partial(_inproj_kernel, with_moe=moe is not None),
        out_shape=out_shape,
        grid=(b, nb),
        in_specs=in_specs,
        out_specs=out_specs,
        compiler_params=_params(2),
        name="inproj",
    )(*args)


def _lane_lo():
    return lax.broadcasted_iota(jnp.int32, (1, LANES), 1) < (LANES // 2)


def _split_heads(qp, lo):
    zero = jnp.zeros_like(qp)
    return jnp.where(lo, qp, zero), jnp.where(lo, zero, qp)


def _softmax_pv_t(score_parts, vt_parts, extra_logit=None):
    m = score_parts[0].max(axis=0, keepdims=True)
    for s in score_parts[1:]:
        m = jnp.maximum(m, s.max(axis=0, keepdims=True))
    if extra_logit is not None:
        m = jnp.maximum(m, extra_logit)
    den = None
    acc = None
    for s, vt in zip(score_parts, vt_parts):
        e = jnp.exp2(s - m)
        d = e.sum(axis=0, keepdims=True)
        den = d if den is None else den + d
        o = _dot(vt, e.astype(BF16))
        acc = o if acc is None else acc + o
    if extra_logit is not None:
        den = den + jnp.exp2(extra_logit - m)
    return acc / den


def _heads_ahead(n_heads, ahead, scores, finish):
    pending = [scores(h) for h in range(ahead)]
    outs = []
    for h in range(n_heads):
        if h + ahead < n_heads:
            pending.append(scores(h + ahead))
        outs.append(finish(h, pending.pop(0)))
    return outs


def _na_kernel(q_ref, k_ref, vt_ref, bias_ref, o_ref, *, ctx, rows):
    j = pl.program_id(1)
    lo = _lane_lo()

    def head_q(h):
        sl = slice((h // 2) * LANES, (h // 2 + 1) * LANES)
        return _split_heads(q_ref[0, :, sl], lo)[h % 2], sl

    def vt(h, cols):
        return vt_ref[0, h * NA_HEAD_DIM:(h + 1) * NA_HEAD_DIM, cols]

    @pl.when(j == 0)
    def _():
        def scores(h):
            qm, sl = head_q(h)
            return [_dot_nt(k_ref[0, 0:ctx, sl], qm)]

        outs = _heads_ahead(NA_HEADS, 2, scores, lambda h, s: _softmax_pv_t(s, [vt(h, slice(0, ctx))]))
        o_ref[0] = jnp.concatenate(outs, axis=0).T.astype(BF16)

    @pl.when(j > 0)
    def _():
        r = (j - 1) * NA_QROWS
        s0 = jnp.clip(r - NA_KH // 2, 0, rows - NA_SLAB)
        slab = pl.ds(pl.multiple_of(ctx + s0 * GRID_W, NA_QROWS * GRID_W), NA_SLAB * GRID_W)

        def scores(h):
            qm, sl = head_q(h)
            return [_dot_nt(k_ref[0, 0:ctx, sl], qm), _dot_nt(k_ref[0, slab, sl], qm) + bias_ref[0, h]]

        outs = _heads_ahead(NA_HEADS, 2, scores,
                            lambda h, s: _softmax_pv_t(s, [vt(h, slice(0, ctx)), vt(h, slab)]))
        o_ref[0] = jnp.concatenate(outs, axis=0).T.astype(BF16)


def _na_attention(li, aq, ak, avt, bias, ctx):
    b, l, w = aq.shape
    assert ((l - ctx) // GRID_W) % NA_QROWS == 0 and ctx % (NA_QROWS * GRID_W) == 0
    nb = l // TM
    rows = (l - ctx) // GRID_W
    last = nb - 1

    def bias_map(i, j):
        return (li, jnp.where(j <= 1, 0, jnp.where(j == last, 2, 1)), 0, 0, 0)

    return pl.pallas_call(
        functools.partial(_na_kernel, ctx=ctx, rows=rows),
        out_shape=jax.ShapeDtypeStruct((b, l, w), BF16),
        grid=(b, nb),
        in_specs=[pl.BlockSpec((1, TM, w), lambda i, j: (i, j, 0)),
                  pl.BlockSpec((1, l, w), lambda i, j: (i, 0, 0)),
                  pl.BlockSpec((1, w, l), lambda i, j: (i, 0, 0)),
                  pl.BlockSpec((None, 1, NA_HEADS, NA_SLAB * GRID_W, TM), bias_map)],
        out_specs=pl.BlockSpec((1, TM, w), lambda i, j: (i, j, 0)),
        compiler_params=_params(2),
        name="na_attention",
    )(aq, ak, avt, bias)


def _na_bias_tables(na_rpb, rows):
    depth = na_rpb.shape[0]
    qc = np.arange(GRID_W)[:, None]
    kc = np.arange(GRID_W)[None, :]
    c0 = np.clip(qc - NA_KW // 2, 0, GRID_W - NA_KW)
    col_ok = (kc >= c0) & (kc < c0 + NA_KW)
    col_idx = np.clip(kc - qc + NA_KW - 1, 0, 2 * NA_KW - 2)
    n_off = 2 * NA_KH - 1
    bc = jnp.take(na_rpb, jnp.asarray(col_idx.reshape(-1)), axis=3).reshape(depth, NA_HEADS, n_off, GRID_W, GRID_W)
    bc = jnp.where(jnp.asarray(col_ok), bc * LOG2E, NEG_INF)
    bc = jnp.concatenate([bc, jnp.full((depth, NA_HEADS, 1, GRID_W, GRID_W), NEG_INF, F32)], axis=2)
    bc = bc.swapaxes(-1, -2)
    i_idx = np.zeros((3, NA_SLAB, NA_QROWS), np.int32)
    for case, (r, s) in enumerate(((0, 0), (NA_KH // 2, 0), (rows - NA_QROWS, rows - NA_SLAB))):
        for a in range(NA_QROWS):
            qr = r + a
            r0 = min(max(qr - NA_KH // 2, 0), rows - NA_KH)
            for c in range(NA_SLAB):
                kr = s + c
                i_idx[case, c, a] = kr - qr + NA_KH - 1 if r0 <= kr < r0 + NA_KH else n_off
    tab = jnp.concatenate([bc, bc], axis=-1)
    n_tab = n_off + 1

    def expand_kernel(idx_ref, tab_ref, o_ref):
        case = pl.program_id(1)
        lo = _lane_lo()
        for c in range(NA_SLAB):
            for ap in range(NA_QROWS // 2):
                at = (case * NA_SLAB + c) * NA_QROWS + 2 * ap
                o_ref[c * GRID_W:(c + 1) * GRID_W, ap * LANES:(ap + 1) * LANES] = jnp.where(
                    lo, tab_ref[idx_ref[at]], tab_ref[idx_ref[at + 1]])

    grid_spec = pltpu.PrefetchScalarGridSpec(
        num_scalar_prefetch=1,
        grid=(depth, 3, NA_HEADS),
        in_specs=[pl.BlockSpec((None, None, n_tab, GRID_W, LANES), lambda l, z, h, idx: (l, h, 0, 0, 0))],
        out_specs=pl.BlockSpec((None, None, None, NA_SLAB * GRID_W, NA_QROWS * GRID_W),
                               lambda l, z, h, idx: (l, z, h, 0, 0)),
    )
    return pl.pallas_call(
        expand_kernel,
        out_shape=jax.ShapeDtypeStruct((depth, 3, NA_HEADS, NA_SLAB * GRID_W, NA_QROWS * GRID_W), F32),
        grid_spec=grid_spec,
        compiler_params=_params(3),
        name="na_bias_expand",
    )(jnp.asarray(i_idx.reshape(-1)), tab)


def _mla_kernel(q_ref, k_ref, vt_ref, o_ref, *, ctx):
    j = pl.program_id(1)

    def run(nkeys):
        tiles = nkeys // 256
        nparts = min(MLA_KEY_PARTS, tiles)
        cuts = [(tiles * p // nparts) * 256 for p in range(nparts + 1)]
        parts = list(zip(cuts[:-1], cuts[1:]))

        def scores(h):
            hsl = slice(h * LANES, (h + 1) * LANES)
            return [_dot_nt(k_ref[0, a:b, hsl], q_ref[0, :, hsl]) for a, b in parts]

        def finish(h, s_parts):
            return _softmax_pv_t(s_parts, [vt_ref[0, h * MLA_V:(h + 1) * MLA_V, a:b] for a, b in parts])

        outs = _heads_ahead(MLA_HEADS, 3, scores, finish)
        o_ref[0] = jnp.concatenate(outs, axis=0).T.astype(BF16)

    @pl.when(j == 0)
    def _():
        run(ctx)

    @pl.when(j > 0)
    def _():
        run(k_ref.shape[1])


def _mla_attention(bq, bk, bvt, ctx):
    b, l, wq = bq.shape
    wv = bvt.shape[1]
    return pl.pallas_call(
        functools.partial(_mla_kernel, ctx=ctx),
        out_shape=jax.ShapeDtypeStruct((b, l, wv), BF16),
        grid=(b, l // TM),
        in_specs=[pl.BlockSpec((1, TM, wq), lambda i, j: (i, j, 0)),
                  pl.BlockSpec((1, l, wq), lambda i, j: (i, 0, 0)),
                  pl.BlockSpec((1, wv, l), lambda i, j: (i, 0, 0))],
        out_specs=pl.BlockSpec((1, TM, wv), lambda i, j: (i, j, 0)),
        compiler_params=_params(2),
        name="mla_attention",
    )(bq, bk, bvt)


def _swa_kernel(sink_ref, q_ref, k_ref, vt_ref, o_ref, *, ctx, n_lat):
    j = pl.program_id(1)
    lo = _lane_lo()
    group = SWA_HEADS // SWA_KV_HEADS
    band = TM + 2 * SWA_WINDOW

    def head_q(h):
        sl = slice((h // 2) * LANES, (h // 2 + 1) * LANES)
        return _split_heads(q_ref[0, :, sl], lo)[h % 2]

    def kv_lanes(h):
        return slice((h // group) * LANES, (h // group + 1) * LANES)

    def vt(h, cols):
        kv = h // group
        return vt_ref[0, kv * SWA_HEAD_DIM:(kv + 1) * SWA_HEAD_DIM, cols]

    def sink(h):
        return jnp.full((1, 1), sink_ref[h] * LOG2E, F32)

    @pl.when(j == 0)
    def _():
        outs = _heads_ahead(
            SWA_HEADS, 2, lambda h: [_dot_nt(k_ref[0, 0:ctx, kv_lanes(h)], head_q(h))],
            lambda h, s: _softmax_pv_t(s, [vt(h, slice(0, ctx))], extra_logit=sink(h)))
        o_ref[0] = jnp.concatenate(outs, axis=0).T.astype(BF16)

    @pl.when(j > 0)
    def _():
        q0 = (j - 1) * TM
        s0 = jnp.clip(q0 - SWA_WINDOW, 0, n_lat - band)
        rows = pl.ds(pl.multiple_of(ctx + s0, SWA_WINDOW), band)
        kpos = s0 + lax.broadcasted_iota(jnp.int32, (band, TM), 0)
        qpos = q0 + lax.broadcasted_iota(jnp.int32, (band, TM), 1)
        keep = jnp.abs(qpos - kpos) <= SWA_WINDOW

        def scores(h):
            qm = head_q(h)
            return [_dot_nt(k_ref[0, 0:ctx, kv_lanes(h)], qm),
                    jnp.where(keep, _dot_nt(k_ref[0, rows, kv_lanes(h)], qm), NEG_INF)]

        outs = _heads_ahead(
            SWA_HEADS, 2, scores,
            lambda h, s: _softmax_pv_t(s, [vt(h, slice(0, ctx)), vt(h, rows)], extra_logit=sink(h)))
        o_ref[0] = jnp.concatenate(outs, axis=0).T.astype(BF16)


def _swa_attention(sink, cq, ck, cvt, ctx):
    b, l, w = cq.shape
    wk = ck.shape[2]
    wv = cvt.shape[1]
    grid_spec = pltpu.PrefetchScalarGridSpec(
        num_scalar_prefetch=1,
        grid=(b, l // TM),
        in_specs=[pl.BlockSpec((1, TM, w), lambda i, j, s: (i, j, 0)),
                  pl.BlockSpec((1, l, wk), lambda i, j, s: (i, 0, 0)),
                  pl.BlockSpec((1, wv, l), lambda i, j, s: (i, 0, 0))],
        out_specs=pl.BlockSpec((1, TM, w), lambda i, j, s: (i, j, 0)),
    )
    return pl.pallas_call(
        functools.partial(_swa_kernel, ctx=ctx, n_lat=l - ctx),
        out_shape=jax.ShapeDtypeStruct((b, l, w), BF16),
        grid_spec=grid_spec,
        compiler_params=_params(2),
        name="swa_attention",
    )(sink, cq, ck, cvt)


def _merge_kernel(oa_ref, ob_ref, oc_ref, gate_ref, x_ref, mod_ref, wbr_ref, wout_ref, g2_ref,
                  rw_ref, rb_ref, xo_ref, h2_ref, route_ref, cnt_ref, run_ref, lg_ref, *, n_exp):
    d = x_ref.shape[2]
    t = pl.program_id(0)

    @pl.when(t == 0)
    def _():
        run_ref[...] = jnp.zeros_like(run_ref)
        lg_ref[...] = jnp.zeros_like(lg_ref)

    logits = lg_ref[(t + 1) % 2]

    mod = mod_ref[0, 0]
    mix = None
    for i, o_ref in enumerate((oa_ref, ob_ref, oc_ref)):
        p = gate_ref[0, :, i * d:(i + 1) * d].astype(F32) * _dot(o_ref[0], wbr_ref[i])
        mix = p if mix is None else mix + p
    y = _dot(mix.astype(BF16), wout_ref[...])
    x = x_ref[0] + mod[2:3] * y
    xo_ref[0] = x
    h2 = (_rms(x) * g2_ref[...]) * (1.0 + mod[4:5]) + mod[3:4]
    h2b = h2.astype(BF16)
    h2_ref[0] = _pack_rows(h2b)
    lg_ref[t % 2] = _dot(h2b, rw_ref[...]) + rb_ref[...]

    lane = lax.broadcasted_iota(jnp.int32, logits.shape, 1).astype(F32)
    work = jnp.where(lane < n_exp, logits, -jnp.inf)
    ids, vals = [], []
    for _ in range(TOP_K):
        m = work.max(axis=-1, keepdims=True)
        idx = jnp.where(work == m, lane, float(LANES)).min(axis=-1, keepdims=True)
        ids.append(idx)
        vals.append(m)
        work = jnp.where(lane == idx, -jnp.inf, work)
    ex = [jnp.exp(v - vals[0]) for v in vals]
    den = ex[0] + ex[1] + ex[2] + ex[3]
    counted = jnp.where(t >= 1, 1.0, 0.0)
    hits = jnp.zeros(logits.shape, F32)
    for idx in ids:
        hits = hits + jnp.where(lane == idx, counted, 0.0)
    r = lax.broadcasted_iota(jnp.int32, (TM, TM), 0)
    c = lax.broadcasted_iota(jnp.int32, (TM, TM), 1)
    tri = jnp.where(c < r, 1.0, 0.0).astype(BF16)
    before = _dot(tri, hits.astype(BF16)) + run_ref[0:1]
    route = jnp.zeros(logits.shape, F32)
    for k in range(TOP_K):
        rank = jnp.where(lane == ids[k], before, 0.0).sum(axis=-1, keepdims=True)
        route = jnp.where(lane == k, ids[k], route)
        route = jnp.where(lane == TOP_K + k, ex[k] / den, route)
        route = jnp.where(lane == 2 * TOP_K + k, rank, route)
    route_ref[0] = route
    run_ref[...] = run_ref[...] + hits.sum(axis=0, keepdims=True)
    cnt_ref[...] = run_ref[...]


def _merge(li, oa, ob, oc, gate, x_all, modsel, wbr, wout, g2, rw, rb, n_exp):
    b, l, d = x_all.shape
    nb = l // TM
    n_steps = b * nb

    def cur(t):
        tc = jnp.minimum(t, n_steps - 1)
        return tc // nb, tc % nb

    def prev(t):
        tp = jnp.maximum(t - 1, 0)
        return tp // nb, tp % nb

    row = lambda width: pl.BlockSpec((1, TM, width), lambda t: cur(t) + (0,))
    full = lambda a: _layer_spec(a, li)
    mod_spec = pl.BlockSpec((None, 1, 1, 6, d), lambda t: (li, cur(t)[0], jnp.minimum(cur(t)[1], 1), 0, 0))
    return pl.pallas_call(
        functools.partial(_merge_kernel, n_exp=n_exp),
        out_shape=[jax.ShapeDtypeStruct((b, l, d), F32),
                   jax.ShapeDtypeStruct((b, l, d // 2), jnp.int32),
                   jax.ShapeDtypeStruct((b, l, LANES), F32),
                   jax.ShapeDtypeStruct((8, LANES), F32)],
        grid=(n_steps + 1,),
        in_specs=[row(BRANCH_W), row(BRANCH_W), row(BRANCH_W), row(N_BRANCH * d), row(d), mod_spec,
                  full(wbr), full(wout), full(g2), full(rw), full(rb)],
        out_specs=[row(d), row(d // 2), pl.BlockSpec((1, TM, LANES), lambda t: prev(t) + (0,)),
                   pl.BlockSpec((8, LANES), lambda t: (0, 0))],
        scratch_shapes=[pltpu.VMEM((8, LANES), F32), pltpu.VMEM((2, TM, LANES), F32)],
        compiler_params=_params(1),
        name="merge",
    )(oa, ob, oc, gate, x_all, modsel, wbr, wout, g2, rw, rb)


def _moe_kernel(be_ref, nx_ref, nu_ref, x_ref, wgu_hbm, bgu_ref, wdn_hbm, bdn_ref, y_ref,
                gu_stage, dn_stage, wgu_s, wdn_s, sem, *, li):
    i = pl.program_id(0)
    ff = wdn_s.shape[0]

    def fetch(e):
        return (pltpu.make_async_copy(wgu_hbm.at[li, e], gu_stage, sem.at[0]),
                pltpu.make_async_copy(wdn_hbm.at[li, e], dn_stage, sem.at[1]))

    @pl.when(i < nu_ref[0])
    def _():
        e = be_ref[i]

        @pl.when(i == 0)
        def _():
            for copy in fetch(e):
                copy.start()

        @pl.when(jnp.logical_or(i == 0, e != be_ref[jnp.maximum(i - 1, 0)]))
        def _():
            for copy in fetch(e):
                copy.wait()
            wgu_s[...] = gu_stage[...].astype(BF16)
            wdn_s[...] = dn_stage[...].astype(BF16)
            nxt = nx_ref[i]

            @pl.when(nxt >= 0)
            def _():
                for copy in fetch(nxt):
                    copy.start()

        gu = _dot(_unpack_rows(x_ref[...]), wgu_s[...]) + bgu_ref[0]
        glu = jnp.minimum(gu[:, :ff], SWIGLU_LIMIT)
        lin = jnp.clip(gu[:, ff:], -SWIGLU_LIMIT, SWIGLU_LIMIT)
        act = glu * _sigmoid(SWIGLU_ALPHA * glu) * (lin + 1.0)
        y_ref[...] = _pack_rows((_dot(act.astype(BF16), wdn_s[...]) + bdn_ref[0]).astype(BF16))


def _moe_experts(li, blk_e, nxt_e, n_used, xs, w_gu, b_gu, w_dn, b_dn):
    n_rows, packed_w = xs.shape
    depth, n_exp, d, ff2 = w_gu.shape
    ff = ff2 // 2
    n_blocks = n_rows // MOE_TM

    def row_map(i, be, nx, nu):
        return (jnp.minimum(i, nu[0] - 1), 0)

    def b_map(i, be, nx, nu):
        return (li, be[jnp.minimum(i, nu[0] - 1)], 0, 0)

    grid_spec = pltpu.PrefetchScalarGridSpec(
        num_scalar_prefetch=3,
        grid=(n_blocks,),
        in_specs=[pl.BlockSpec((MOE_TM, packed_w), row_map),
                  pl.BlockSpec(memory_space=pl.ANY),
                  pl.BlockSpec((None, 1, 1, ff2), b_map),
                  pl.BlockSpec(memory_space=pl.ANY),
                  pl.BlockSpec((None, 1, 1, d), b_map)],
        out_specs=pl.BlockSpec((MOE_TM, packed_w), row_map),
        scratch_shapes=[pltpu.VMEM((d, ff2), F32), pltpu.VMEM((ff, d), F32),
                        pltpu.VMEM((d, ff2), BF16), pltpu.VMEM((ff, d), BF16),
                        pltpu.SemaphoreType.DMA((2,))],
    )
    return pl.pallas_call(
        functools.partial(_moe_kernel, li=li),
        out_shape=jax.ShapeDtypeStruct((n_rows, packed_w), jnp.int32),
        grid_spec=grid_spec,
        compiler_params=_params(1),
        name="moe_experts",
    )(blk_e, nxt_e, n_used, xs, w_gu, b_gu.reshape(depth, n_exp, 1, ff2), w_dn, b_dn.reshape(depth, n_exp, 1, d))


def _layout(route, cnt, n_exp):
    b, l, _ = route.shape
    t = b * l
    ids = route[..., 0:TOP_K].astype(jnp.int32)
    rank = route[..., 2 * TOP_K:3 * TOP_K].astype(jnp.int32)
    counts = cnt[0, :n_exp].astype(jnp.int32)
    padded = (counts + MOE_TM - 1) // MOE_TM * MOE_TM
    pad_end = jnp.cumsum(padded)
    pad_start = pad_end - padded
    onehot = ids[..., None] == jnp.arange(n_exp, dtype=jnp.int32)
    dest = jnp.sum(jnp.where(onehot, pad_start, 0), axis=-1) + rank
    dest = dest.reshape(t, TOP_K).T
    n_blocks = -(-t * TOP_K // MOE_TM) + n_exp
    blk_start = jnp.arange(n_blocks, dtype=jnp.int32) * MOE_TM
    blk_e = jnp.minimum(jnp.sum(blk_start[:, None] >= pad_end[None, :], axis=1), n_exp - 1).astype(jnp.int32)
    n_used = (pad_end[-1] // MOE_TM).astype(jnp.int32).reshape(1)
    ar = jnp.arange(n_exp, dtype=jnp.int32)
    later = jnp.logical_and(ar[None, :] > ar[:, None], (counts > 0)[None, :])
    nxt_of = jnp.min(jnp.where(later, ar[None, :], n_exp), axis=1)
    nxt_of = jnp.where(nxt_of == n_exp, -1, nxt_of)
    nxt_e = jnp.sum(jnp.where(blk_e[:, None] == ar[None, :], nxt_of[None, :], 0), axis=1).astype(jnp.int32)
    return dest, blk_e, nxt_e, n_used, n_blocks * MOE_TM


def _sc_mesh():
    return plsc.VectorSubcoreMesh(core_axis_name="c", subcore_axis_name="s",
                                  num_cores=SC_CORES, num_subcores=SC_SUBCORES)


def _sc_worker():
    return lax.axis_index("s") * SC_CORES + lax.axis_index("c")


def _sc_scatter_rows(x, idx, n_rows):
    t, d = x.shape
    n_idx = idx.shape[0]
    workers = SC_CORES * SC_SUBCORES
    n_chunks = n_idx // (workers * SC_SCATTER_ROWS)
    assert n_chunks * workers * SC_SCATTER_ROWS == n_idx and t % SC_SCATTER_ROWS == 0 and n_chunks % 2 == 0
    idx3 = idx.reshape(workers, n_chunks, SC_SCATTER_ROWS)

    @pl.kernel(out_type=jax.ShapeDtypeStruct((n_rows, d), x.dtype), mesh=_sc_mesh(),
               scratch_types=[pltpu.VMEM((n_chunks, SC_SCATTER_ROWS), jnp.int32),
                              pltpu.VMEM((SC_SCATTER_ROWS, d), x.dtype),
                              pltpu.VMEM((SC_SCATTER_ROWS, d), x.dtype),
                              pltpu.SemaphoreType.DMA, pltpu.SemaphoreType.DMA, pltpu.SemaphoreType.DMA])
    def scatter_kernel(x_hbm, i_hbm, o_hbm, idx_v, buf0, buf1, sem0, sem1, out_sem):
        wid = _sc_worker()
        pltpu.sync_copy(i_hbm.at[wid], idx_v)
        bufs, sems = (buf0, buf1), (sem0, sem1)

        def load(chunk, b):
            src = pl.multiple_of(((wid * n_chunks + chunk) * SC_SCATTER_ROWS) % t, SC_SCATTER_ROWS)
            return pltpu.make_async_copy(x_hbm.at[pl.ds(src, SC_SCATTER_ROWS)], bufs[b], sems[b])

        load(0, 0).start()

        @pl.loop(0, n_chunks, step=2)
        def _(j):
            for b in range(2):
                chunk = j + b
                load(chunk, b).wait()
                if b == 0:
                    load(chunk + 1, 1).start()
                else:
                    @pl.when(chunk + 1 < n_chunks)
                    def _():
                        load(chunk + 1, 0).start()
                pltpu.async_copy(bufs[b], o_hbm.at[idx_v.at[chunk]], out_sem).wait()

    return scatter_kernel(x, idx3)


def _sc_gather_rows(x, idx):
    d = x.shape[1]
    n_idx = idx.shape[0]
    workers = SC_CORES * SC_SUBCORES
    per_worker = n_idx // workers
    n_chunks = per_worker // SC_GATHER_ROWS
    assert n_chunks * workers * SC_GATHER_ROWS == n_idx

    assert n_chunks % 2 == 0

    @pl.kernel(out_type=jax.ShapeDtypeStruct((n_idx, d), x.dtype), mesh=_sc_mesh(),
               scratch_types=[pltpu.VMEM((per_worker,), jnp.int32),
                              pltpu.VMEM((SC_GATHER_ROWS, d), x.dtype),
                              pltpu.VMEM((SC_GATHER_ROWS, d), x.dtype),
                              pltpu.SemaphoreType.DMA, pltpu.SemaphoreType.DMA])
    def gather_kernel(x_hbm, i_hbm, o_hbm, idx_v, buf0, buf1, sem0, sem1):
        base = _sc_worker() * per_worker
        pltpu.sync_copy(i_hbm.at[pl.ds(base, per_worker)], idx_v)
        bufs, sems = (buf0, buf1), (sem0, sem1)

        def gather(chunk, b):
            off = pl.multiple_of(chunk * SC_GATHER_ROWS, SC_GATHER_ROWS)
            return pltpu.make_async_copy(x_hbm.at[idx_v.at[pl.ds(off, SC_GATHER_ROWS)]], bufs[b], sems[b])

        gather(0, 0).start()

        @pl.loop(0, n_chunks, step=2)
        def _(j):
            for b in range(2):
                chunk = j + b
                gather(chunk, b).wait()
                if b == 0:
                    gather(chunk + 1, 1).start()
                else:
                    @pl.when(chunk + 1 < n_chunks)
                    def _():
                        gather(chunk + 1, 0).start()
                off = pl.multiple_of(chunk * SC_GATHER_ROWS, SC_GATHER_ROWS)
                pltpu.sync_copy(bufs[b], o_hbm.at[pl.ds(base + off, SC_GATHER_ROWS)])

    return gather_kernel(x, idx)


def _combine(y4_ref, route_ref):
    route = route_ref[0]
    y = None
    for k in range(TOP_K):
        t = route[:, TOP_K + k:TOP_K + k + 1] * _unpack_rows(y4_ref[k, 0]).astype(F32)
        y = t if y is None else y + t
    return y


def _final_kernel(x_ref, y4_ref, route_ref, mod_ref, g_ref, o_ref):
    x = x_ref[0] + mod_ref[0, 0][5:6] * _combine(y4_ref, route_ref)
    o_ref[0] = _rms(x) * g_ref[...]


def _final(li, x_all, y4, route, modsel, skip, final_g):
    b, l, d = x_all.shape
    off = skip // TM
    lo = l - skip
    return pl.pallas_call(
        _final_kernel, out_shape=jax.ShapeDtypeStruct((b, lo, d), F32), grid=(b, lo // TM),
        in_specs=[pl.BlockSpec((1, TM, d), lambda i, j: (i, j + off, 0)),
                  pl.BlockSpec((TOP_K, 1, TM, y4.shape[3]), lambda i, j: (0, i, j + off, 0)),
                  pl.BlockSpec((1, TM, LANES), lambda i, j: (i, j + off, 0)),
                  _mod_spec(li, off)(d),
                  pl.BlockSpec((1, d), lambda i, j: (0, 0))],
        out_specs=pl.BlockSpec((1, TM, d), lambda i, j: (i, j, 0)),
        compiler_params=_params(2), name="final_norm",
    )(x_all, y4, route, modsel, final_g)


def _rope_tables(n, ctx, rot_dim, group_pattern):
    t = np.arange(n, dtype=np.int32)
    row = (t // GRID_W).astype(np.float32)
    col = (t % GRID_W).astype(np.float32)
    per_axis = rot_dim // 2
    inv = (np.float32(ROPE_BASE) ** (-np.arange(0, per_axis, 2, dtype=np.float32) / np.float32(per_axis))).astype(np.float32)
    ang = np.concatenate([row[:, None] * inv[None], col[:, None] * inv[None]], axis=-1).astype(np.float32)
    cos, sin = np.cos(ang).astype(np.float32), np.sin(ang).astype(np.float32)
    half = rot_dim // 2
    tab = np.zeros((3, ctx + n, LANES), np.float32)
    tab[0] = 1.0
    for off in group_pattern:
        tab[0, ctx:, off:off + half] = cos
        tab[0, ctx:, off + half:off + rot_dim] = cos
        tab[1, ctx:, off:off + half] = -sin
        tab[2, ctx:, off + half:off + rot_dim] = sin
    return jnp.asarray(tab)


def _wprep_kernel(w_ref, o_ref):
    x = w_ref[0]
    rows = x.shape[0]
    o = 3 * 512 + MLA_Q_LORA + MLA_KV_LORA
    zeros = lambda n: jnp.zeros((rows, n), F32)
    kr = jnp.concatenate([zeros(MLA_NOPE), x[:, o:o + MLA_ROPE], zeros(LANES - MLA_NOPE - MLA_ROPE)], axis=1)
    o += MLA_ROPE
    cq = x[:, o:o + 512]
    k0, k1 = x[:, o + 512:o + 576], x[:, o + 576:o + 640]
    v0, v1 = x[:, o + 640:o + 704], x[:, o + 704:o + 768]
    g = x[:, o + 768:]
    o_ref[0] = jnp.concatenate([x[:, :3 * 512 + MLA_Q_LORA + MLA_KV_LORA], kr, cq, k0, k0, k1, k1, v0, v0, v1, v1, g],
                               axis=1).astype(BF16)


def _wprep(w_in):
    depth, d, cols = w_in.shape
    out_cols = cols - MLA_ROPE + LANES + 2 * 128
    tr = 128
    return pl.pallas_call(
        _wprep_kernel,
        out_shape=jax.ShapeDtypeStruct((depth, d, out_cols), BF16),
        grid=(depth, d // tr),
        in_specs=[pl.BlockSpec((1, tr, cols), lambda l, i: (l, i, 0))],
        out_specs=pl.BlockSpec((1, tr, out_cols), lambda l, i: (l, i, 0)),
        compiler_params=_params(2),
        name="wprep",
    )(w_in)


def _prep_weights(w_in, mla_w_uq, mla_w_ukv):
    depth = w_in.shape[0]
    w = _wprep(w_in)
    uq = mla_w_uq.reshape(depth, MLA_Q_LORA, MLA_HEADS, MLA_NOPE + MLA_ROPE)
    uq = jnp.pad(uq, ((0, 0), (0, 0), (0, 0), (0, LANES - MLA_NOPE - MLA_ROPE)))
    uq = uq.reshape(depth, MLA_Q_LORA, MLA_HEADS * LANES).astype(BF16)
    ukv = mla_w_ukv.reshape(depth, MLA_KV_LORA, MLA_HEADS, MLA_NOPE + MLA_V)
    uk = jnp.pad(ukv[..., :MLA_NOPE], ((0, 0), (0, 0), (0, 0), (0, LANES - MLA_NOPE)))
    uk = uk.reshape(depth, MLA_KV_LORA, MLA_HEADS * LANES)
    uv = ukv[..., MLA_NOPE:].reshape(depth, MLA_KV_LORA, MLA_HEADS * MLA_V)
    ukv = jnp.concatenate([uk, uv], axis=-1).astype(BF16)
    return w, uq, ukv


def kernel(x, c, ctx, c_ctx, norm1_g, norm2_g, w_mod, b_mod, w_in, na_rpb, mla_q_norm_g, mla_kv_norm_g, mla_w_uq, mla_w_ukv, swa_sink, w_branch, w_out, router_w, router_b, expert_w_gate_up, expert_b_gate_up, expert_w_down, expert_b_down, final_norm_g):
    b, n, d = x.shape
    lc = ctx.shape[1]
    l = lc + n
    depth = w_in.shape[0]
    n_exp = router_w.shape[2]
    assert lc == TM and n % TM == 0 and TM == NA_QROWS * GRID_W

    cvec = jnp.concatenate([c, c_ctx[None], jnp.zeros((8 - b - 1, d), F32)], axis=0)
    mod = _modulation(cvec, w_mod, b_mod)
    mod_lat = mod[:, :b].reshape(depth, b, 1, 6, d)
    mod_ctx = jnp.broadcast_to(mod[:, b].reshape(depth, 1, 1, 6, d), (depth, b, 1, 6, d))
    modsel = jnp.concatenate([mod_ctx, mod_lat], axis=2)

    w_all, uq_all, ukv_all = _prep_weights(w_in, mla_w_uq, mla_w_ukv)
    wbr_all = w_branch.astype(BF16)
    wout_all = w_out.astype(BF16)
    rw_all = jnp.pad(router_w, ((0, 0), (0, 0), (0, LANES - n_exp))).astype(BF16)
    rb_all = jnp.pad(router_b, ((0, 0), (0, LANES - n_exp))).reshape(depth, 1, LANES)
    rope_b = _rope_tables(n, lc, MLA_ROPE, (MLA_NOPE,))
    rope_c = _rope_tables(n, lc, SWA_HEAD_DIM, (0, SWA_HEAD_DIM))
    bias_all = _na_bias_tables(na_rpb, n // GRID_W)

    g1_all = norm1_g.reshape(depth, 1, d)
    g2_all = norm2_g.reshape(depth, 1, d)
    gq_all = mla_q_norm_g.reshape(depth, 1, -1)
    gkv_all = mla_kv_norm_g.reshape(depth, 1, -1)

    x_all = (ctx, x)
    moe = None
    for li in range(depth):
        ms = modsel
        x_all, aq, ak, av, bq, bk, bv, cq, ck, cv, gate = _inproj(
            li, x_all, moe, ms, g1_all, w_all, uq_all, ukv_all, gq_all, gkv_all, rope_b, rope_c)
        oa = _na_attention(li, aq, ak, av, bias_all, lc)
        ob = _mla_attention(bq, bk, bv, lc)
        oc = _swa_attention(swa_sink[li], cq, ck, cv, lc)
        x_all, h2, route, cnt = _merge(li, oa, ob, oc, gate, x_all, ms, wbr_all, wout_all,
                                       g2_all, rw_all, rb_all, n_exp)
        dest, blk_e, nxt_e, n_used, n_rows = _layout(route, cnt, n_exp)
        dest = dest.reshape(TOP_K * b * l)
        xs = _sc_scatter_rows(h2.reshape(b * l, d // 2), dest, n_rows)
        ys = _moe_experts(li, blk_e, nxt_e, n_used, xs, expert_w_gate_up, expert_b_gate_up,
                          expert_w_down, expert_b_down)
        moe = (_sc_gather_rows(ys, dest).reshape(TOP_K, b, l, d // 2), route)
    return _final(depth - 1, x_all, moe[0], moe[1], modsel, lc, final_norm_g.reshape(1, d))
```
